```python
import math
import jax, jax.numpy as jnp
from jax import lax
import numpy as np

D_MODEL = 1024
BATCH = 4
SEQ = 4096
DEPTH = 1

NSA_HEADS = 16
NSA_KV_GROUPS = 2
NSA_HEADS_PER_GROUP = NSA_HEADS // NSA_KV_GROUPS
NSA_HEAD_DIM = 64
CMP_BLOCK = 32
CMP_STRIDE = 16
CMP_HIDDEN = 256
SLC_BLOCK = 64
SLC_TOPN = 16
WINDOW = 512
Q_BLOCK = 128
ROPE_THETA = 500000.0
ROPE_DIM = NSA_HEAD_DIM // 4
FORCED_SCORE = 1.0e4

RET_HEADS = 8
RET_KEY_DIM = 64
RET_VAL_DIM = 128
RET_CHUNK = 128
RET_ROPE_THETA = 10000.0

D_FF = 2816
EPS = 1e-6
GN_EPS = 1e-5
NEG_INF = -1e30

NSA_Q_W = NSA_HEADS * NSA_HEAD_DIM
NSA_KV_W = NSA_KV_GROUPS * NSA_HEAD_DIM
NSA_GATE_W = 3 * NSA_HEADS
RET_QK_W = RET_HEADS * RET_KEY_DIM
RET_V_W = RET_HEADS * RET_VAL_DIM
IN_WIDTHS = (NSA_Q_W, NSA_KV_W, NSA_KV_W, NSA_KV_W, NSA_KV_W, NSA_KV_W, NSA_KV_W,
             NSA_GATE_W, RET_QK_W, RET_QK_W, RET_V_W, RET_V_W, 2 * D_MODEL)
D_IN = sum(IN_WIDTHS)

kernel_name = "hybrid_nsa_retention_macaron"


def _rmsnorm(x, g):
    xf = x.astype(jnp.float32)
    y = xf * lax.rsqrt(jnp.mean(xf * xf, axis=-1, keepdims=True) + EPS)
    return (y * g.astype(jnp.float32)).astype(x.dtype)


def _swiglu(x, w_gate, w_up, w_down):
    return (jax.nn.silu(x @ w_gate) * (x @ w_up)) @ w_down


def _rope(x, pos, rot_dim, theta):
    half = rot_dim // 2
    freqs = theta ** (-(jnp.arange(half, dtype=jnp.float32) * 2.0 / rot_dim))
    ang = pos.astype(jnp.float32)[:, None] * freqs[None, :]
    cos = jnp.cos(ang).astype(x.dtype)
    sin = jnp.sin(ang).astype(x.dtype)
    x1 = x[..., :half]
    x2 = x[..., half:rot_dim]
    return jnp.concatenate([x1 * cos - x2 * sin, x2 * cos + x1 * sin, x[..., rot_dim:]], axis=-1)


def _masked_softmax(s, mask):
    s = jnp.where(mask, s.astype(jnp.float32), NEG_INF)
    m = jnp.max(s, axis=-1, keepdims=True)
    p = jnp.where(mask, jnp.exp(s - m), 0.0)
    return p / jnp.maximum(jnp.sum(p, axis=-1, keepdims=True), 1e-20)


def _nsa(q, k_cmp, v_cmp, k_slc, v_slc, k_win, v_win, gates, pos_emb, ck_w1, ck_w2, cv_w1, cv_w2):
    B, G, Hg, S, dh = q.shape
    scale = dh ** -0.5
    pos = jnp.arange(S)
    q = _rope(q, pos, ROPE_DIM, ROPE_THETA)
    k_slc = _rope(k_slc, pos, ROPE_DIM, ROPE_THETA)
    k_win = _rope(k_win, pos, ROPE_DIM, ROPE_THETA)

    n_cmp = (S - CMP_BLOCK) // CMP_STRIDE + 1
    cmp_start = jnp.arange(n_cmp) * CMP_STRIDE
    cmp_end = cmp_start + CMP_BLOCK - 1
    cmp_idx = cmp_start[:, None] + jnp.arange(CMP_BLOCK)[None, :]

    def compress(tok, w1, w2):
        blk = tok[:, :, cmp_idx] + pos_emb
        blk = blk.reshape(B, G, n_cmp, CMP_BLOCK * dh)
        return jax.nn.gelu(blk @ w1) @ w2

    kc = _rope(compress(k_cmp, ck_w1, ck_w2), cmp_end, ROPE_DIM, ROPE_THETA)
    vc = compress(v_cmp, cv_w1, cv_w2)

    n_slc = S // SLC_BLOCK
    s_start = jnp.arange(n_slc) * SLC_BLOCK
    overlap = jnp.clip(jnp.minimum(cmp_start[:, None] + CMP_BLOCK, s_start[None, :] + SLC_BLOCK)
                       - jnp.maximum(cmp_start[:, None], s_start[None, :]), 0, None)
    cmp_to_slc = overlap.astype(jnp.float32) / CMP_BLOCK
    top_n = min(SLC_TOPN, n_slc)

    k_blocks = k_slc.reshape(B, G, n_slc, SLC_BLOCK, dh)
    v_blocks = v_slc.reshape(B, G, n_slc, SLC_BLOCK, dh)
    k_pad = jnp.pad(k_win, ((0, 0), (0, 0), (WINDOW, 0), (0, 0)))
    v_pad = jnp.pad(v_win, ((0, 0), (0, 0), (WINDOW, 0), (0, 0)))
    b_ix = jnp.arange(B)[:, None, None, None]
    g_ix = jnp.arange(G)[None, :, None, None]
    sb = jnp.arange(n_slc)

    def one_block(c):
        q0 = c * Q_BLOCK
        qc = lax.dynamic_slice_in_dim(q, q0, Q_BLOCK, axis=3)
        gc = lax.dynamic_slice_in_dim(gates, q0, Q_BLOCK, axis=3)
        t = q0 + jnp.arange(Q_BLOCK)

        s_c = jnp.einsum('bghqd,bgnd->bghqn', qc, kc) * scale
        p_c = _masked_softmax(s_c, cmp_end[None, :] <= t[:, None])
        o_c = jnp.einsum('bghqn,bgnd->bghqd', p_c.astype(vc.dtype), vc)

        imp = jnp.einsum('bghqn,ns->bgqs', p_c, cmp_to_slc)
        cur = t // SLC_BLOCK
        forced = (sb[None, :] == 0) | (sb[None, :] == cur[:, None]) | (sb[None, :] == cur[:, None] - 1)
        future = sb[None, :] > cur[:, None]
        imp = jnp.where(forced, FORCED_SCORE, jnp.where(future, -FORCED_SCORE, imp))
        _, sel = lax.top_k(imp, top_n)
        k_sel = k_blocks[b_ix, g_ix, sel]
        v_sel = v_blocks[b_ix, g_ix, sel]
        kpos = sel[..., None] * SLC_BLOCK + jnp.arange(SLC_BLOCK)
        s_s = jnp.einsum('bghqd,bgqnkd->bghqnk', qc, k_sel) * scale
        s_s = s_s.reshape(B, G, Hg, Q_BLOCK, top_n * SLC_BLOCK)
        mask_s = (kpos <= t[:, None, None]).reshape(B, G, 1, Q_BLOCK, top_n * SLC_BLOCK)
        p_s = _masked_softmax(s_s, mask_s).reshape(B, G, Hg, Q_BLOCK, top_n, SLC_BLOCK)
        o_s = jnp.einsum('bghqnk,bgqnkd->bghqd', p_s.astype(v_sel.dtype), v_sel)

        kw = lax.dynamic_slice_in_dim(k_pad, q0, WINDOW + Q_BLOCK, axis=2)
        vw = lax.dynamic_slice_in_dim(v_pad, q0, WINDOW + Q_BLOCK, axis=2)
        wpos = q0 - WINDOW + jnp.arange(WINDOW + Q_BLOCK)
        dist = t[:, None] - wpos[None, :]
        mask_w = (wpos[None, :] >= 0) & (dist >= 0) & (dist < WINDOW)
        s_w = jnp.einsum('bghqd,bgkd->bghqk', qc, kw) * scale
        p_w = _masked_softmax(s_w, mask_w)
        o_w = jnp.einsum('bghqk,bgkd->bghqd', p_w.astype(vw.dtype), vw)

        return gc[..., 0:1] * o_c + gc[..., 1:2] * o_s + gc[..., 2:3] * o_w

    out = lax.map(one_block, jnp.arange(S // Q_BLOCK))
    return out.transpose(1, 0, 4, 2, 3, 5).reshape(B, S, G * Hg * dh)


def _retention(q, k, v, gn_gain):
    B, H, S, dk = q.shape
    dv = v.shape[-1]
    C = RET_CHUNK
    N = S // C
    dt = q.dtype
    pos = jnp.arange(S)
    q = _rope(q, pos, dk, RET_ROPE_THETA) * (dk ** -0.5)
    k = _rope(k, pos, dk, RET_ROPE_THETA)
    log_g = jnp.log(1.0 - 2.0 ** (-5.0 - jnp.arange(H, dtype=jnp.float32)))
    i = jnp.arange(C, dtype=jnp.float32)
    diff = i[:, None] - i[None, :]
    decay = jnp.where(diff >= 0, jnp.exp(jnp.maximum(diff, 0.0) * log_g[:, None, None]), 0.0)
    zeta = jnp.exp((C - 1.0 - i)[None, :] * log_g[:, None])
    xi = jnp.exp((i + 1.0)[None, :] * log_g[:, None])
    g_chunk = jnp.exp(C * log_g)

    qc = q.reshape(B, H, N, C, dk)
    kc = k.reshape(B, H, N, C, dk)
    vc = v.reshape(B, H, N, C, dv)
    inner = jnp.einsum('bhncd,bhnkd->bhnck', qc, kc) * decay[None, :, None].astype(dt)
    o_inner = jnp.einsum('bhnck,bhnke->bhnce', inner, vc)
    kv = jnp.einsum('bhncd,bhnce->nbhde', kc * zeta[None, :, None, :, None].astype(dt), vc)
    kv = kv.astype(jnp.float32)

    def step(R, kv_n):
        return g_chunk[None, :, None, None] * R + kv_n, R

    _, R_prev = lax.scan(step, jnp.zeros((B, H, dk, dv), jnp.float32), kv)
    o_cross = jnp.einsum('bhncd,nbhde->bhnce',
                         (qc * xi[None, :, None, :, None].astype(dt)).astype(jnp.float32), R_prev)
    o = (o_inner.astype(jnp.float32) + o_cross).reshape(B, H, S, dv)
    mu = jnp.mean(o, axis=-1, keepdims=True)
    var = jnp.mean(jnp.square(o - mu), axis=-1, keepdims=True)
    on = (o - mu) * lax.rsqrt(var + GN_EPS) * gn_gain.astype(jnp.float32)[None, :, None, :]
    return on.astype(dt).transpose(0, 2, 1, 3).reshape(B, S, H * dv)


def setup_inputs(seed: int = 0) -> dict:
    key = jax.random.key(seed)
    ks = jax.random.split(key, 24)
    f32 = jnp.float32

    def w(k, shape, fan_in):
        return jax.random.normal(k, shape, f32) * (fan_in ** -0.5)

    def gain(k, shape):
        return 1.0 + 0.02 * jax.random.normal(k, shape, f32)

    L = DEPTH
    return {
        "x": jax.random.normal(ks[0], (BATCH, SEQ, D_MODEL), f32),
        "ffn1_norm": gain(ks[1], (L, D_MODEL)),
        "ffn1_w_gate": w(ks[2], (L, D_MODEL, D_FF), D_MODEL),
        "ffn1_w_up": w(ks[3], (L, D_MODEL, D_FF), D_MODEL),
        "ffn1_w_down": w(ks[4], (L, D_FF, D_MODEL), D_FF),
        "mix_norm": gain(ks[5], (L, D_MODEL)),
        "w_in": w(ks[6], (L, D_MODEL, D_IN), D_MODEL),
        "cmp_pos_emb": 0.1 * jax.random.normal(ks[7], (L, CMP_BLOCK, NSA_HEAD_DIM), f32),
        "cmp_k_w1": w(ks[8], (L, CMP_BLOCK * NSA_HEAD_DIM, CMP_HIDDEN), CMP_BLOCK * NSA_HEAD_DIM),
        "cmp_k_w2": w(ks[9], (L, CMP_HIDDEN, NSA_HEAD_DIM), CMP_HIDDEN),
        "cmp_v_w1": w(ks[10], (L, CMP_BLOCK * NSA_HEAD_DIM, CMP_HIDDEN), CMP_BLOCK * NSA_HEAD_DIM),
        "cmp_v_w2": w(ks[11], (L, CMP_HIDDEN, NSA_HEAD_DIM), CMP_HIDDEN),
        "ret_gn_gain": gain(ks[12], (L, RET_HEADS, RET_VAL_DIM)),
        "w_branch_nsa": w(ks[13], (L, NSA_Q_W, D_MODEL), NSA_Q_W),
        "w_branch_ret": w(ks[14], (L, RET_V_W, D_MODEL), RET_V_W),
        "w_out": w(ks[15], (L, D_MODEL, D_MODEL), D_MODEL),
        "ffn2_norm": gain(ks[16], (L, D_MODEL)),
        "ffn2_w_gate": w(ks[17], (L, D_MODEL, D_FF), D_MODEL),
        "ffn2_w_up": w(ks[18], (L, D_MODEL, D_FF), D_MODEL),
        "ffn2_w_down": w(ks[19], (L, D_FF, D_MODEL), D_FF),
        "final_norm": gain(ks[20], (D_MODEL,)),
    }


def reference(x, ffn1_norm, ffn1_w_gate, ffn1_w_up, ffn1_w_down, mix_norm, w_in,
              cmp_pos_emb, cmp_k_w1, cmp_k_w2, cmp_v_w1, cmp_v_w2, ret_gn_gain,
              w_branch_nsa, w_branch_ret, w_out, ffn2_norm, ffn2_w_gate, ffn2_w_up,
              ffn2_w_down, final_norm):
    B, S, _ = x.shape
    G, Hg, dh = NSA_KV_GROUPS, NSA_HEADS_PER_GROUP, NSA_HEAD_DIM
    split_points = [int(p) for p in np.cumsum(IN_WIDTHS)[:-1]]

    def nsa_heads(t):
        return t.reshape(B, S, G, Hg, dh).transpose(0, 2, 3, 1, 4)

    def nsa_kv(t):
        return t.reshape(B, S, G, dh).transpose(0, 2, 1, 3)

    def ret_heads(t, d):
        return t.reshape(B, S, RET_HEADS, d).transpose(0, 2, 1, 3)

    for layer in range(DEPTH):
        h = _rmsnorm(x, ffn1_norm[layer])
        x = x + 0.5 * _swiglu(h, ffn1_w_gate[layer], ffn1_w_up[layer], ffn1_w_down[layer])

        h = _rmsnorm(x, mix_norm[layer])
        proj = h @ w_in[layer]
        (q_a, kc_a, vc_a, ks_a, vs_a, kw_a, vw_a, g_a,
         q_r, k_r, v_r, g_r, g_merge) = jnp.split(proj, split_points, axis=-1)

        nsa_gates = jax.nn.sigmoid(g_a.reshape(B, S, G, Hg, 3).transpose(0, 2, 3, 1, 4))
        a_out = _nsa(nsa_heads(q_a), nsa_kv(kc_a), nsa_kv(vc_a), nsa_kv(ks_a), nsa_kv(vs_a),
                     nsa_kv(kw_a), nsa_kv(vw_a), nsa_gates, cmp_pos_emb[layer],
                     cmp_k_w1[layer], cmp_k_w2[layer], cmp_v_w1[layer], cmp_v_w2[layer])
        r_out = _retention(ret_heads(q_r, RET_KEY_DIM), ret_heads(k_r, RET_KEY_DIM),
                           ret_heads(v_r, RET_VAL_DIM), ret_gn_gain[layer])
        r_out = jax.nn.silu(g_r) * r_out

        gate_a, gate_r = jnp.split(jax.nn.sigmoid(g_merge), 2, axis=-1)
        mixed = gate_a * (a_out @ w_branch_nsa[layer]) + gate_r * (r_out @ w_branch_ret[layer])
        x = x + mixed @ w_out[layer]

        h = _rmsnorm(x, ffn2_norm[layer])
        x = x + 0.5 * _swiglu(h, ffn2_w_gate[layer], ffn2_w_up[layer], ffn2_w_down[layer])

    return _rmsnorm(x, final_norm)
```

```python
import functools
import math

import numpy as np
import jax
import jax.numpy as jnp
from jax import lax
from jax.experimental import pallas as pl
from jax.experimental.pallas import tpu as pltpu

F32 = jnp.float32
BF16 = jnp.bfloat16

NSA_HEADS = 16
NSA_GROUPS = 2
NSA_HG = NSA_HEADS // NSA_GROUPS
NSA_DH = 64
CMP_BLOCK = 32
CMP_STRIDE = 16
SLC_BLOCK = 64
SLC_TOPN = 16
WINDOW = 512
Q_BLOCK = 128
ROPE_THETA = 500000.0
ROPE_DIM = NSA_DH // 4
FORCED_SCORE = 1.0e4
RET_HEADS = 8
RET_DK = 64
RET_DV = 128
RET_CHUNK = 128
RET_ROPE_THETA = 10000.0
EPS = 1e-6
GN_EPS = 1e-5
NEG_INF = -1e30

LANES = 128
MXU_N = 256
VMEM_LIMIT = 56 * 1024 * 1024

NT_DIMS = (((1,), (1,)), ((), ()))


def _rms(x, g, eps=EPS):
    return x * lax.rsqrt(jnp.mean(x * x, axis=-1, keepdims=True) + eps) * g


def _resident(shape):
    nd = len(shape)
    return pl.BlockSpec(shape, lambda *_: (0,) * nd, pipeline_mode=pl.Buffered(1))


def _ffn_body(x_ref, g_ref, wg_ref, wu_ref, wd_ref, fg_ref, o_ref, *, tf, final):
    x = x_ref[...]
    h = _rms(x, g_ref[...]).astype(BF16)
    acc = jnp.zeros(x.shape, F32)
    for c in range(wg_ref.shape[1] // tf):
        sl = slice(c * tf, (c + 1) * tf)
        g = jnp.dot(h, wg_ref[:, sl], preferred_element_type=F32)
        u = jnp.dot(h, wu_ref[:, sl], preferred_element_type=F32)
        a = (g * jax.nn.sigmoid(g) * u).astype(BF16)
        acc = acc + jnp.dot(a, wd_ref[sl, :], preferred_element_type=F32)
    y = x + 0.5 * acc
    if final:
        y = _rms(y, fg_ref[...])
    o_ref[...] = y


def _ffn(x2d, gain, wg, wu, wd, final_gain, *, final, tm=512):
    T, D = x2d.shape
    F = wg.shape[1]
    tf = F // 2 if (F // 2) % LANES == 0 else F
    tok = pl.BlockSpec((tm, D), lambda i: (i, 0))
    return pl.pallas_call(
        functools.partial(_ffn_body, tf=tf, final=final),
        grid=(T // tm,),
        in_specs=[tok, _resident((1, D)), _resident((D, F)), _resident((D, F)), _resident((F, D)),
                  _resident((1, D))],
        out_specs=tok,
        out_shape=jax.ShapeDtypeStruct((T, D), F32),
        compiler_params=pltpu.CompilerParams(dimension_semantics=("parallel",),
                                             vmem_limit_bytes=VMEM_LIMIT),
        name="ffn_final" if final else "ffn",
    )(x2d, gain.reshape(1, D), wg, wu, wd, final_gain.reshape(1, D))


def _rope_slab(y, tab_ref, shift):
    return (y * tab_ref[0] + pltpu.roll(y, LANES - shift, 1) * tab_ref[1]
            + pltpu.roll(y, shift, 1) * tab_ref[2])


def _proj_body(x_ref, g_ref, wa_ref, wb_ref, wc_ref, wd_ref, ta_ref, tb_ref,
               oa_ref, ob_ref, oc_ref, od_ref):
    h = _rms(x_ref[...], g_ref[...]).astype(BF16)

    def group(w_ref, o_ref, tab_ref, shift):
        for c in range(w_ref.shape[1] // MXU_N):
            y = jnp.dot(h, w_ref[:, c * MXU_N:(c + 1) * MXU_N], preferred_element_type=F32)
            for s in range(MXU_N // LANES):
                ys = y[:, s * LANES:(s + 1) * LANES]
                if tab_ref is not None:
                    ys = _rope_slab(ys, tab_ref, shift)
                lo = c * MXU_N + s * LANES
                o_ref[:, lo:lo + LANES] = ys.astype(o_ref.dtype)

    group(wa_ref, oa_ref, ta_ref, ROPE_DIM // 2)
    group(wb_ref, ob_ref, tb_ref, RET_DK // 2)
    group(wc_ref, oc_ref, None, 0)
    group(wd_ref, od_ref, None, 0)


def _proj(x2d, gain, wa, wb, wc, wd, tab_a, tab_b, S, *, tm=512):
    T, D = x2d.shape
    spt = S // tm
    tok = lambda n: pl.BlockSpec((tm, n), lambda i: (i, 0))
    tab = pl.BlockSpec((3, tm, LANES), lambda i: (0, i % spt, 0))
    return pl.pallas_call(
        _proj_body,
        grid=(T // tm,),
        in_specs=[tok(D), _resident((1, D)), _resident(wa.shape), _resident(wb.shape),
                  _resident(wc.shape), _resident(wd.shape), tab, tab],
        out_specs=[tok(wa.shape[1]), tok(wb.shape[1]), tok(wc.shape[1]), tok(wd.shape[1])],
        out_shape=[jax.ShapeDtypeStruct((T, wa.shape[1]), BF16),
                   jax.ShapeDtypeStruct((T, wb.shape[1]), BF16),
                   jax.ShapeDtypeStruct((T, wc.shape[1]), BF16),
                   jax.ShapeDtypeStruct((T, wd.shape[1]), F32)],
        compiler_params=pltpu.CompilerParams(dimension_semantics=("parallel",),
                                             vmem_limit_bytes=VMEM_LIMIT),
        name="proj",
    )(x2d, gain.reshape(1, D), wa, wb, wc, wd, tab_a, tab_b)


def _rope_tables(pos, rot_dim, head_dim, theta):
    half = rot_dim // 2
    freqs = theta ** (-(jnp.arange(half, dtype=F32) * 2.0 / rot_dim))
    ang = pos.astype(F32)[:, None] * freqs[None, :]
    cos, sin = jnp.cos(ang), jnp.sin(ang)
    n = pos.shape[0]
    rest = head_dim - rot_dim
    cos_h = jnp.concatenate([cos, cos, jnp.ones((n, rest), F32)], axis=-1)
    sl_h = jnp.concatenate([-sin, jnp.zeros((n, half + rest), F32)], axis=-1)
    sr_h = jnp.concatenate([jnp.zeros((n, half), F32), sin, jnp.zeros((n, rest), F32)], axis=-1)
    rep = LANES // head_dim
    return jnp.stack([jnp.tile(t, (1, rep)) for t in (cos_h, sl_h, sr_h)])


def _gelu_tanh(x):
    return 0.5 * x * (1.0 + jnp.tanh(math.sqrt(2.0 / math.pi) * (x + 0.044715 * (x * x * x))))


def _compress_body(kt_ref, vt_ref, pe_ref, kw1_ref, kw2_ref, vw1_ref, vw2_ref, tab_ref,
                   kc_ref, vc_ref):
    half = kw1_ref.shape[0] // 2
    pe = pe_ref[...]
    n = kt_ref.shape[2]

    def mlp(tok_ref, w1_ref, w2_ref):
        c = tok_ref[0, 0].astype(F32)
        top = jnp.dot((c + pe[0:1]).astype(BF16), w1_ref[:half, :], preferred_element_type=F32)
        bot = jnp.dot((c + pe[1:2]).astype(BF16), w1_ref[half:, :], preferred_element_type=F32)
        hid = _gelu_tanh(top + pltpu.roll(bot, n - 1, 0))
        return jnp.dot(hid.astype(BF16), w2_ref[...], preferred_element_type=F32)

    kc = _rope_slab(mlp(kt_ref, kw1_ref, kw2_ref), tab_ref, ROPE_DIM // 2)
    vc = mlp(vt_ref, vw1_ref, vw2_ref)
    kc_ref[0, 0] = kc[:, :NSA_DH].astype(kc_ref.dtype)
    vc_ref[0, 0] = vc[:, :NSA_DH].astype(vc_ref.dtype)


def _compress(k_tok, v_tok, pe, kw1, kw2, vw1, vw2, tab):
    B, G, n, w = k_tok.shape
    tok = pl.BlockSpec((1, 1, n, w), lambda b, g: (b, g, 0, 0))
    out = pl.BlockSpec((1, 1, n, NSA_DH), lambda b, g: (b, g, 0, 0))
    return pl.pallas_call(
        _compress_body,
        grid=(B, G),
        in_specs=[tok, tok, _resident(pe.shape), _resident(kw1.shape), _resident(kw2.shape),
                  _resident(vw1.shape), _resident(vw2.shape), _resident(tab.shape)],
        out_specs=[out, out],
        out_shape=[jax.ShapeDtypeStruct((B, G, n, NSA_DH), BF16)] * 2,
        compiler_params=pltpu.CompilerParams(dimension_semantics=("parallel", "parallel"),
                                             vmem_limit_bytes=VMEM_LIMIT),
        name="compress",
    )(k_tok, v_tok, pe, kw1, kw2, vw1, vw2, tab)


def _softmax_rows(s3, mask):
    s3 = jnp.where(mask, s3, NEG_INF)
    m = jnp.max(s3, axis=-1, keepdims=True)
    p = jnp.where(mask, jnp.exp(s3 - m), 0.0)
    return p / jnp.maximum(jnp.sum(p, axis=-1, keepdims=True), 1e-20)


def _nsa_body(q_ref, gate_ref, kc_ref, vc_ref, ks_ref, vs_ref, kw_ref, vw_ref, c2s_ref, et_ref,
              o_ref, *, kt, top_n, n_slc):
    hg, qb, dh = q_ref.shape[2:]
    rows = hg * qb
    ncp = kc_ref.shape[2]
    S = ks_ref.shape[2]
    q0 = pl.program_id(2) * qb
    q = q_ref[0, 0].reshape(rows, dh)

    def tpos(n):
        return q0 + lax.broadcasted_iota(jnp.int32, (qb, n), 0)

    s_c = lax.dot_general(q, kc_ref[0, 0], NT_DIMS, preferred_element_type=F32)
    cend = lax.broadcasted_iota(jnp.int32, (qb, ncp), 1) * CMP_STRIDE + (CMP_BLOCK - 1)
    p_c = _softmax_rows(s_c.reshape(hg, qb, ncp), (cend <= tpos(ncp))[None])
    o_c = jnp.dot(p_c.reshape(rows, ncp).astype(BF16), vc_ref[0, 0], preferred_element_type=F32)

    p_sum = jnp.sum(p_c, axis=0)
    imp = jnp.zeros((qb, LANES), F32)
    rem = p_sum
    for _ in range(3):
        part = rem.astype(BF16)
        imp = imp + jnp.dot(part, c2s_ref[...], preferred_element_type=F32)
        rem = rem - part.astype(F32)

    sb = lax.broadcasted_iota(jnp.int32, (qb, LANES), 1)
    cur = jnp.right_shift(tpos(LANES), int(math.log2(SLC_BLOCK)))
    forced = (sb == 0) | (sb == cur) | (sb == cur - 1)
    future = sb > cur
    imp = jnp.where(forced, FORCED_SCORE, jnp.where(future, -FORCED_SCORE, imp))
    rank = jnp.zeros((qb, LANES), jnp.int32)
    for sp in range(n_slc):
        col = imp[:, sp:sp + 1]
        beats = (col > imp) | ((col == imp) & (sb > sp))
        rank = rank + beats.astype(jnp.int32)
    sel = jnp.where((rank < top_n) & jnp.logical_not(future), 1.0, 0.0).astype(BF16)

    def sel_tile(j, carry):
        m, l, acc = carry
        k0 = pl.multiple_of(j * kt, kt)
        k = ks_ref[0, 0, pl.ds(k0, kt), :]
        v = vs_ref[0, 0, pl.ds(k0, kt), :]
        picked = lax.dot_general(sel, et_ref[pl.ds(k0, kt), :], NT_DIMS, preferred_element_type=F32)
        kpos = k0 + lax.broadcasted_iota(jnp.int32, (qb, kt), 1)
        bias = jnp.where((picked > 0.5) & (kpos <= tpos(kt)), 0.0, NEG_INF)
        s = lax.dot_general(q, k, NT_DIMS, preferred_element_type=F32).reshape(hg, qb, kt) + bias[None]
        m_new = jnp.maximum(m, jnp.max(s, axis=-1, keepdims=True))
        alpha = jnp.exp(m - m_new)
        p = jnp.exp(s - m_new)
        l = alpha * l + jnp.sum(p, axis=-1, keepdims=True)
        pv = jnp.dot(p.reshape(rows, kt).astype(BF16), v, preferred_element_type=F32)
        return m_new, l, alpha * acc + pv.reshape(hg, qb, dh)

    n_tiles = (q0 + qb + kt - 1) // kt
    init = (jnp.full((hg, qb, 1), NEG_INF, F32), jnp.zeros((hg, qb, 1), F32), jnp.zeros((hg, qb, dh), F32))
    _, l_s, acc_s = lax.fori_loop(0, n_tiles, sel_tile, init)
    o_s = (acc_s / l_s).reshape(rows, dh)

    span = WINDOW + qb
    w0 = pl.multiple_of(jnp.maximum(q0 - WINDOW, 0), qb)
    kw = kw_ref[0, 0, pl.ds(w0, span), :]
    vw = vw_ref[0, 0, pl.ds(w0, span), :]
    dist = tpos(span) - (w0 + lax.broadcasted_iota(jnp.int32, (qb, span), 1))
    s_w = lax.dot_general(q, kw, NT_DIMS, preferred_element_type=F32)
    p_w = _softmax_rows(s_w.reshape(hg, qb, span), ((dist >= 0) & (dist < WINDOW))[None])
    o_w = jnp.dot(p_w.reshape(rows, span).astype(BF16), vw, preferred_element_type=F32)

    gts = jax.nn.sigmoid(gate_ref[0, 0].reshape(rows, 3))
    out = gts[:, 0:1] * o_c + gts[:, 1:2] * o_s + gts[:, 2:3] * o_w
    o_ref[0, 0] = out.reshape(hg, qb, dh).astype(o_ref.dtype)


def _nsa(q, gates, kc, vc, ks, vs, kw, vw, c2s, et, *, kt=256):
    B, G, Hg, S, dh = q.shape
    ncp = kc.shape[2]
    n_slc = S // SLC_BLOCK
    kt = min(kt, S)
    qspec = lambda w: pl.BlockSpec((1, 1, Hg, Q_BLOCK, w), lambda b, g, c: (b, g, 0, c, 0))
    seq = lambda n: pl.BlockSpec((1, 1, n, dh), lambda b, g, c: (b, g, 0, 0))
    return pl.pallas_call(
        functools.partial(_nsa_body, kt=kt, top_n=min(SLC_TOPN, n_slc), n_slc=n_slc),
        grid=(B, G, S // Q_BLOCK),
        in_specs=[qspec(dh), qspec(3), seq(ncp), seq(ncp), seq(S), seq(S), seq(S), seq(S),
                  _resident(c2s.shape), _resident(et.shape)],
        out_specs=qspec(dh),
        out_shape=jax.ShapeDtypeStruct((B, G, Hg, S, dh), BF16),
        compiler_params=pltpu.CompilerParams(dimension_semantics=("parallel", "parallel", "arbitrary"),
                                             vmem_limit_bytes=VMEM_LIMIT),
        name="nsa",
    )(q, gates, kc, vc, ks, vs, kw, vw, c2s, et)


def _nsa_constants(S):
    ncp = S // CMP_STRIDE
    n_slc = S // SLC_BLOCK
    cmp_start = np.arange(ncp) * CMP_STRIDE
    s_start = np.arange(n_slc) * SLC_BLOCK
    overlap = np.clip(np.minimum(cmp_start[:, None] + CMP_BLOCK, s_start[None, :] + SLC_BLOCK)
                      - np.maximum(cmp_start[:, None], s_start[None, :]), 0, None)
    c2s = np.zeros((ncp, LANES), np.float32)
    c2s[:, :n_slc] = overlap.astype(np.float32) / CMP_BLOCK
    c2s[ncp - 1] = 0.0
    et = np.zeros((S, LANES), np.float32)
    et[np.arange(S), np.arange(S) // SLC_BLOCK] = 1.0
    return jnp.asarray(c2s, BF16), jnp.asarray(et, BF16)


def _ret_body(q_ref, k_ref, kT_ref, v_ref, decay_ref, zeta_ref, xi_ref, gch_ref, gn_ref, o_ref, *, C):
    S = q_ref.shape[2]
    decay = decay_ref[0]
    zeta = zeta_ref[0]
    xi = xi_ref[0]
    gch = gch_ref[0]
    gn = gn_ref[0]
    R = jnp.zeros((kT_ref.shape[2], v_ref.shape[3]), F32)
    for n in range(S // C):
        sl = slice(n * C, (n + 1) * C)
        q = q_ref[0, 0, sl, :]
        k = k_ref[0, 0, sl, :]
        v = v_ref[0, 0, sl, :]
        inner = lax.dot_general(q, k, NT_DIMS, preferred_element_type=F32) * decay
        o = jnp.dot(inner.astype(BF16), v, preferred_element_type=F32)
        o = o + jnp.dot((q.astype(F32) * xi).astype(BF16), R.astype(BF16), preferred_element_type=F32)
        kz = (kT_ref[0, 0, :, sl].astype(F32) * zeta).astype(BF16)
        R = gch * R + jnp.dot(kz, v, preferred_element_type=F32)
        mu = jnp.mean(o, axis=-1, keepdims=True)
        d = o - mu
        var = jnp.mean(d * d, axis=-1, keepdims=True)
        o_ref[0, sl, :] = d * lax.rsqrt(var + GN_EPS) * gn


def _retention(q, k, kT, v, gn_gain, *, C=RET_CHUNK):
    B, H, S, dk = q.shape
    dv = v.shape[3]
    log_g = jnp.log(1.0 - 2.0 ** (-5.0 - jnp.arange(H, dtype=F32)))
    i = jnp.arange(C, dtype=F32)
    diff = i[:, None] - i[None, :]
    decay = jnp.where(diff >= 0, jnp.exp(jnp.maximum(diff, 0.0) * log_g[:, None, None]), 0.0)
    zeta = jnp.exp((C - 1.0 - i)[None, :] * log_g[:, None]).reshape(H, 1, C)
    xi = jnp.broadcast_to(jnp.exp((i + 1.0)[None, :] * log_g[:, None])[:, :, None], (H, C, dk))
    gch = jnp.broadcast_to(jnp.exp(C * log_g)[:, None, None], (H, 1, dv))
    per_head = lambda shape: pl.BlockSpec((1,) + shape, lambda b, h: (h, 0, 0))
    seq = lambda shape: pl.BlockSpec((1, 1) + shape, lambda b, h: (b, h, 0, 0))
    return pl.pallas_call(
        functools.partial(_ret_body, C=C),
        grid=(B, H),
        in_specs=[seq((S, dk)), seq((S, dk)), seq((dk, S)), seq((S, dv)),
                  per_head((C, C)), per_head((1, C)), per_head((C, dk)), per_head((1, dv)),
                  per_head((1, dv))],
        out_specs=pl.BlockSpec((1, S, dv), lambda b, h: (b, 0, h)),
        out_shape=jax.ShapeDtypeStruct((B, S, H * dv), F32),
        compiler_params=pltpu.CompilerParams(dimension_semantics=("parallel", "parallel"),
                                             vmem_limit_bytes=VMEM_LIMIT),
        name="retention",
    )(q, k, kT, v, decay, zeta, xi, gch, gn_gain.reshape(H, 1, dv))


def _merge_body(x_ref, a_ref, r_ref, gr_ref, ga_ref, gb_ref, wa_ref, wr_ref, wo_ref, o_ref):
    g = gr_ref[...]
    r = (g * jax.nn.sigmoid(g) * r_ref[...]).astype(BF16)
    ya = jnp.dot(a_ref[...], wa_ref[...], preferred_element_type=F32)
    yr = jnp.dot(r, wr_ref[...], preferred_element_type=F32)
    mixed = jax.nn.sigmoid(ga_ref[...]) * ya + jax.nn.sigmoid(gb_ref[...]) * yr
    o_ref[...] = x_ref[...] + jnp.dot(mixed.astype(BF16), wo_ref[...], preferred_element_type=F32)


def _merge(x2d, a, r, gates, wa, wr, wo, *, tm=512):
    T, D = x2d.shape
    col = lambda j: pl.BlockSpec((tm, D), lambda i: (i, j))
    return pl.pallas_call(
        _merge_body,
        grid=(T // tm,),
        in_specs=[col(0), col(0), col(0), col(0), col(1), col(2),
                  _resident(wa.shape), _resident(wr.shape), _resident(wo.shape)],
        out_specs=col(0),
        out_shape=jax.ShapeDtypeStruct((T, D), F32),
        compiler_params=pltpu.CompilerParams(dimension_semantics=("parallel",),
                                             vmem_limit_bytes=VMEM_LIMIT),
        name="merge",
    )(x2d, a, r, gates, gates, gates, wa, wr, wo)


def _layer(x, p, final_norm):
    B, S, D = x.shape
    T = B * S
    G, Hg, dh = NSA_GROUPS, NSA_HG, NSA_DH
    H, dk, dv = RET_HEADS, RET_DK, RET_DV
    bf = lambda w: w.astype(BF16)

    x1 = _ffn(x.reshape(T, D), p["ffn1_norm"], bf(p["ffn1_w_gate"]), bf(p["ffn1_w_up"]),
              bf(p["ffn1_w_down"]), final_norm, final=False)

    w = p["w_in"]
    widths = (G * Hg * dh,) + (G * dh,) * 6 + (3 * G * Hg, H * dk, H * dk, H * dv, H * dv, 2 * D)
    offs = np.concatenate([[0], np.cumsum(widths)])
    seg = lambda i: w[:, int(offs[i]):int(offs[i + 1])]
    (q_a, kc_a, vc_a, ks_a, vs_a, kw_a, vw_a, g_a, q_r, k_r, v_r, g_r, g_m) = [seg(i) for i in range(13)]
    n_gate = g_a.shape[1]
    gate_pad = (-(g_r.shape[1] + g_m.shape[1] + n_gate)) % MXU_N
    wa = bf(jnp.concatenate([q_a * dh ** -0.5, ks_a, kw_a], axis=1))
    wb = bf(jnp.concatenate([q_r * dk ** -0.5, k_r], axis=1))
    wc = bf(jnp.concatenate([kc_a, vc_a, vs_a, vw_a, v_r], axis=1))
    wd = bf(jnp.concatenate([g_r, g_m, g_a, jnp.zeros((D, gate_pad), F32)], axis=1))
    pos = jnp.arange(S)
    tab_a = _rope_tables(pos, ROPE_DIM, dh, ROPE_THETA)
    tab_b = _rope_tables(pos, dk, dk, RET_ROPE_THETA)
    oa, ob, oc, od = _proj(x1, p["mix_norm"], wa, wb, wc, wd, tab_a, tab_b, S)

    nq = G * Hg * dh
    gd = G * dh
    kv_heads = lambda t: t.reshape(B, S, G, dh).transpose(0, 2, 1, 3)
    q_n = oa[:, :nq].reshape(B, S, G, Hg, dh).transpose(0, 2, 3, 1, 4)
    ks_n, kw_n = kv_heads(oa[:, nq:nq + gd]), kv_heads(oa[:, nq + gd:])
    kc_t, vc_t, vs_n, vw_n = [kv_heads(oc[:, i * gd:(i + 1) * gd]) for i in range(4)]
    gates_n = od[:, 3 * D:3 * D + n_gate].reshape(B, S, G, Hg, 3).transpose(0, 2, 3, 1, 4)

    chunk = lambda t: t.reshape(B, G, S // CMP_STRIDE, CMP_STRIDE * dh)
    pe = p["cmp_pos_emb"].reshape(2, CMP_STRIDE * dh)
    pad_w2 = lambda w2: bf(jnp.pad(w2, ((0, 0), (0, LANES - dh))))
    cmp_end = jnp.arange(S // CMP_STRIDE) * CMP_STRIDE + (CMP_BLOCK - 1)
    kc, vc = _compress(chunk(kc_t), chunk(vc_t), pe, bf(p["cmp_k_w1"]), pad_w2(p["cmp_k_w2"]),
                       bf(p["cmp_v_w1"]), pad_w2(p["cmp_v_w2"]),
                       _rope_tables(cmp_end, ROPE_DIM, dh, ROPE_THETA))
    c2s, et = _nsa_constants(S)
    a_heads = _nsa(q_n, gates_n, kc, vc, ks_n, vs_n, kw_n, vw_n, c2s, et)
    a_out = a_heads.transpose(0, 3, 1, 2, 4).reshape(T, nq)

    nr = H * dk
    r_heads = lambda t, d: t.reshape(B, S, H, d).transpose(0, 2, 1, 3)
    q_h, k_h = r_heads(ob[:, :nr], dk), r_heads(ob[:, nr:], dk)
    v_h = r_heads(oc[:, 4 * gd:], dv)
    r_out = _retention(q_h, k_h, k_h.transpose(0, 1, 3, 2), v_h, p["ret_gn_gain"]).reshape(T, H * dv)

    x2 = _merge(x1, a_out, r_out, od, bf(p["w_branch_nsa"]), bf(p["w_branch_ret"]), bf(p["w_out"]))
    x3 = _ffn(x2, p["ffn2_norm"], bf(p["ffn2_w_gate"]), bf(p["ffn2_w_up"]), bf(p["ffn2_w_down"]),
              final_norm, final=True)
    return x3.reshape(B, S, D)


def kernel(x, ffn1_norm, ffn1_w_gate, ffn1_w_up, ffn1_w_down, mix_norm, w_in, cmp_pos_emb, cmp_k_w1,
           cmp_k_w2, cmp_v_w1, cmp_v_w2, ret_gn_gain, w_branch_nsa, w_branch_ret, w_out, ffn2_norm,
           ffn2_w_gate, ffn2_w_up, ffn2_w_down, final_norm):
    assert ffn1_norm.shape[0] == 1, "single-layer stack"
    names = ("ffn1_norm", "ffn1_w_gate", "ffn1_w_up", "ffn1_w_down", "mix_norm", "w_in", "cmp_pos_emb",
             "cmp_k_w1", "cmp_k_w2", "cmp_v_w1", "cmp_v_w2", "ret_gn_gain", "w_branch_nsa",
             "w_branch_ret", "w_out", "ffn2_norm", "ffn2_w_gate", "ffn2_w_up", "ffn2_w_down")
    vals = (ffn1_norm, ffn1_w_gate, ffn1_w_up, ffn1_w_down, mix_norm, w_in, cmp_pos_emb, cmp_k_w1,
            cmp_k_w2, cmp_v_w1, cmp_v_w2, ret_gn_gain, w_branch_nsa, w_branch_ret, w_out, ffn2_norm,
            ffn2_w_gate, ffn2_w_up, ffn2_w_down)
    p = {n: v[0] for n, v in zip(names, vals)}
    return _layer(x, p, final_norm)
```

```python
import functools
import math

import numpy as np
import jax
import jax.numpy as jnp
from jax import lax
from jax.experimental import pallas as pl
from jax.experimental.pallas import tpu as pltpu

F32 = jnp.float32
BF16 = jnp.bfloat16

NSA_HEADS = 16
NSA_GROUPS = 2
NSA_HG = NSA_HEADS // NSA_GROUPS
NSA_DH = 64
CMP_BLOCK = 32
CMP_STRIDE = 16
SLC_BLOCK = 64
SLC_TOPN = 16
WINDOW = 512
Q_BLOCK = 128
ROPE_THETA = 500000.0
ROPE_DIM = NSA_DH // 4
FORCED_SCORE = 1.0e4
RET_HEADS = 8
RET_DK = 64
RET_DV = 128
RET_CHUNK = 128
RET_ROPE_THETA = 10000.0
EPS = 1e-6
GN_EPS = 1e-5
NEG_INF = -1e30
LOG2E = math.log2(math.e)

LANES = 128
BF16_ROWS = 16
MXU_N = 256
VMEM_LIMIT = 56 * 1024 * 1024

NT_DIMS = (((1,), (1,)), ((), ()))


def _rms(x, g, eps=EPS):
    return x * lax.rsqrt(jnp.mean(x * x, axis=-1, keepdims=True) + eps) * g


def _resident(shape):
    nd = len(shape)
    return pl.BlockSpec(shape, lambda *_: (0,) * nd, pipeline_mode=pl.Buffered(1))


def _ffn_body(x_ref, g_ref, wg_ref, wu_ref, wd_ref, fg_ref, o_ref, *, tf, final):
    x = x_ref[...]
    h = _rms(x, g_ref[...]).astype(BF16)
    acc = jnp.zeros(x.shape, F32)
    for c in range(wg_ref.shape[1] // tf):
        sl = slice(c * tf, (c + 1) * tf)
        g = jnp.dot(h, wg_ref[:, sl], preferred_element_type=F32)
        u = jnp.dot(h, wu_ref[:, sl], preferred_element_type=F32)
        a = (g * jax.nn.sigmoid(g) * u).astype(BF16)
        acc = acc + jnp.dot(a, wd_ref[sl, :], preferred_element_type=F32)
    y = x + 0.5 * acc
    if final:
        y = _rms(y, fg_ref[...])
    o_ref[...] = y


def _ffn(x2d, gain, wg, wu, wd, final_gain, *, final, tm=512):
    T, D = x2d.shape
    F = wg.shape[1]
    tf = F // 2 if (F // 2) % LANES == 0 else F
    tok = pl.BlockSpec((tm, D), lambda i: (i, 0))
    return pl.pallas_call(
        functools.partial(_ffn_body, tf=tf, final=final),
        grid=(T // tm,),
        in_specs=[tok, _resident((1, D)), _resident((D, F)), _resident((D, F)), _resident((F, D)),
                  _resident((1, D))],
        out_specs=tok,
        out_shape=jax.ShapeDtypeStruct((T, D), F32),
        compiler_params=pltpu.CompilerParams(dimension_semantics=("parallel",),
                                             vmem_limit_bytes=VMEM_LIMIT),
        name="ffn_final" if final else "ffn",
    )(x2d, gain.reshape(1, D), wg, wu, wd, final_gain.reshape(1, D))


def _rope_slab(y, tab_ref, shift):
    return (y * tab_ref[0] + pltpu.roll(y, LANES - shift, 1) * tab_ref[1]
            + pltpu.roll(y, shift, 1) * tab_ref[2])


def _proj_body(x_ref, g_ref, wa_ref, wb_ref, wc_ref, wd_ref, ta_ref, tb_ref,
               oa_ref, ob_ref, ov_ref, okc_ref, ovc_ref, ovsT_ref, ovwT_ref, od_ref):
    h = _rms(x_ref[...], g_ref[...]).astype(BF16)
    tm = h.shape[0]
    nv = ov_ref.shape[1] // LANES

    def slabs(w_ref):
        for c in range(w_ref.shape[1] // MXU_N):
            y = jnp.dot(h, w_ref[:, c * MXU_N:(c + 1) * MXU_N], preferred_element_type=F32)
            for s in range(MXU_N // LANES):
                yield c * (MXU_N // LANES) + s, y[:, s * LANES:(s + 1) * LANES]

    def put(o_ref, i, ys):
        o_ref[:, i * LANES:(i + 1) * LANES] = ys.astype(o_ref.dtype)

    def put_t(o_ref, ys):
        for r in range(tm // LANES):
            o_ref[0, r] = ys[r * LANES:(r + 1) * LANES, :].T.astype(o_ref.dtype)

    for i, ys in slabs(wa_ref):
        put(oa_ref, i, _rope_slab(ys, ta_ref, ROPE_DIM // 2))
    for i, ys in slabs(wb_ref):
        put(ob_ref, i, _rope_slab(ys, tb_ref, RET_DK // 2))
    for i, ys in slabs(wc_ref):
        if i < nv:
            put(ov_ref, i, ys)
        elif i == nv:
            put(okc_ref, 0, ys)
        elif i == nv + 1:
            put(ovc_ref, 0, ys)
        elif i == nv + 2:
            put_t(ovsT_ref, ys)
        else:
            put_t(ovwT_ref, ys)
    for i, ys in slabs(wd_ref):
        put(od_ref, i, ys)


def _proj(x2d, gain, wa, wb, wc, wd, tab_a, tab_b, B, S, *, tm=512):
    T, D = x2d.shape
    spt = S // tm
    nv = wc.shape[1] - 4 * LANES
    tok = lambda n: pl.BlockSpec((tm, n), lambda i: (i, 0))
    tab = pl.BlockSpec((3, tm, LANES), lambda i: (0, i % spt, 0))
    tr = pl.BlockSpec((1, tm // LANES, LANES, LANES), lambda i: (i // spt, i % spt, 0, 0))
    return pl.pallas_call(
        _proj_body,
        grid=(T // tm,),
        in_specs=[tok(D), _resident((1, D)), _resident(wa.shape), _resident(wb.shape),
                  _resident(wc.shape), _resident(wd.shape), tab, tab],
        out_specs=[tok(wa.shape[1]), tok(wb.shape[1]), tok(nv), tok(LANES), tok(LANES), tr, tr,
                   tok(wd.shape[1])],
        out_shape=[jax.ShapeDtypeStruct((T, wa.shape[1]), BF16),
                   jax.ShapeDtypeStruct((T, wb.shape[1]), BF16),
                   jax.ShapeDtypeStruct((T, nv), BF16),
                   jax.ShapeDtypeStruct((T, LANES), BF16),
                   jax.ShapeDtypeStruct((T, LANES), BF16),
                   jax.ShapeDtypeStruct((B, S // LANES, LANES, LANES), BF16),
                   jax.ShapeDtypeStruct((B, S // LANES, LANES, LANES), BF16),
                   jax.ShapeDtypeStruct((T, wd.shape[1]), F32)],
        compiler_params=pltpu.CompilerParams(dimension_semantics=("parallel",),
                                             vmem_limit_bytes=VMEM_LIMIT),
        name="proj",
    )(x2d, gain.reshape(1, D), wa, wb, wc, wd, tab_a, tab_b)


def _rope_tables(pos, rot_dim, head_dim, theta):
    half = rot_dim // 2
    freqs = theta ** (-(jnp.arange(half, dtype=F32) * 2.0 / rot_dim))
    ang = pos.astype(F32)[:, None] * freqs[None, :]
    cos, sin = jnp.cos(ang), jnp.sin(ang)
    n = pos.shape[0]
    rest = head_dim - rot_dim
    cos_h = jnp.concatenate([cos, cos, jnp.ones((n, rest), F32)], axis=-1)
    sl_h = jnp.concatenate([-sin, jnp.zeros((n, half + rest), F32)], axis=-1)
    sr_h = jnp.concatenate([jnp.zeros((n, half), F32), sin, jnp.zeros((n, rest), F32)], axis=-1)
    rep = LANES // head_dim
    return jnp.stack([jnp.tile(t, (1, rep)) for t in (cos_h, sl_h, sr_h)])


def _gelu_tanh(x):
    return 0.5 * x * (1.0 + jnp.tanh(math.sqrt(2.0 / math.pi) * (x + 0.044715 * (x * x * x))))


def _compress_body(kt_ref, vt_ref, pe_ref, kw1_ref, kw2_ref, vw1_ref, vw2_ref, tab_ref,
                   kc_ref, vcT_ref):
    pe = pe_ref[...]
    n = kt_ref.shape[1]

    def mlp(tok_ref, w1_ref, w2_ref):
        c = tok_ref[0].astype(F32)
        lo = (c + pe[0:1]).astype(BF16)
        hi = (c + pe[1:2]).astype(BF16)
        out = jnp.zeros((n, LANES), F32)
        for g in range(NSA_GROUPS):
            top = jnp.dot(lo, w1_ref[g, 0], preferred_element_type=F32)
            bot = jnp.dot(hi, w1_ref[g, 1], preferred_element_type=F32)
            hid = _gelu_tanh(top + pltpu.roll(bot, n - 1, 0))
            out = out + jnp.dot(hid.astype(BF16), w2_ref[g], preferred_element_type=F32)
        return out

    kc_ref[0] = _rope_slab(mlp(kt_ref, kw1_ref, kw2_ref), tab_ref, ROPE_DIM // 2).astype(kc_ref.dtype)
    vc = mlp(vt_ref, vw1_ref, vw2_ref)
    for r in range(n // LANES):
        vcT_ref[0, :, r * LANES:(r + 1) * LANES] = vc[r * LANES:(r + 1) * LANES, :].T.astype(vcT_ref.dtype)


def _compress(k_tok, v_tok, pe, kw1, kw2, vw1, vw2, tab):
    B, n, w = k_tok.shape
    tok = pl.BlockSpec((1, n, w), lambda b: (b, 0, 0))
    return pl.pallas_call(
        _compress_body,
        grid=(B,),
        in_specs=[tok, tok, _resident(pe.shape), _resident(kw1.shape), _resident(kw2.shape),
                  _resident(vw1.shape), _resident(vw2.shape), _resident(tab.shape)],
        out_specs=[pl.BlockSpec((1, n, LANES), lambda b: (b, 0, 0)),
                   pl.BlockSpec((1, LANES, n), lambda b: (b, 0, 0))],
        out_shape=[jax.ShapeDtypeStruct((B, n, LANES), BF16), jax.ShapeDtypeStruct((B, LANES, n), BF16)],
        compiler_params=pltpu.CompilerParams(dimension_semantics=("parallel",),
                                             vmem_limit_bytes=VMEM_LIMIT),
        name="compress",
    )(k_tok, v_tok, pe, kw1, kw2, vw1, vw2, tab)


def _compress_weights(w1, w2, pe):
    G, dh = NSA_GROUPS, NSA_DH
    hid = w1.shape[1]
    halves = w1.reshape(2, CMP_STRIDE, dh, hid)
    w1e = jnp.zeros((G, 2, CMP_STRIDE, G, dh, hid), F32)
    w2e = jnp.zeros((G, hid, G, dh), F32)
    for g in range(G):
        w1e = w1e.at[g, :, :, g].set(halves)
        w2e = w2e.at[g, :, g].set(w2)
    pe_e = jnp.broadcast_to(pe.reshape(2, CMP_STRIDE, 1, dh), (2, CMP_STRIDE, G, dh))
    return (w1e.reshape(G, 2, CMP_STRIDE * G * dh, hid).astype(BF16), w2e.reshape(G, hid, G * dh).astype(BF16),
            pe_e.reshape(2, CMP_STRIDE * G * dh))


def _nsa_body(q_ref, gate_ref, kc_ref, vcT_ref, ks_ref, vsT_ref, kw_ref, vwT_ref, c2sT_ref, et_ref,
              o_ref, qT_scr, gT_scr, m_scr, acc_scr, *, kt, top_n, n_slc, hg, dh):
    qb = q_ref.shape[0]
    cols = hg * qb
    ncp = kc_ref.shape[1]
    g = pl.program_id(1)
    c = pl.program_id(2)
    q0 = c * qb
    goff = pl.multiple_of(g * dh, dh)
    ones_rows = jnp.ones((BF16_ROWS, 1), BF16)

    def v_tiles(vT_ref, k0, n):
        t = vT_ref[0, pl.ds(k0 // LANES, n // LANES), pl.ds(goff, dh), :]
        vT = jnp.concatenate([t[i] for i in range(n // LANES)], axis=1)
        return jnp.concatenate([vT, jnp.broadcast_to(ones_rows, (BF16_ROWS, n))], axis=0)

    def tq(n):
        return q0 + lax.broadcasted_iota(jnp.int32, (n, qb), 1)

    def krow(n, k0=0):
        return k0 + lax.broadcasted_iota(jnp.int32, (n, qb), 0)

    qT_scr[...] = jnp.zeros(qT_scr.shape, BF16)
    for j in range(hg // 2):
        t = q_ref[:, j * LANES:(j + 1) * LANES].astype(F32).T.astype(BF16)
        qT_scr[pl.ds(goff, dh), (2 * j) * qb:(2 * j + 1) * qb] = t[:dh]
        qT_scr[pl.ds(goff, dh), (2 * j + 1) * qb:(2 * j + 2) * qb] = t[dh:]
    gT_scr[...] = jax.nn.sigmoid(gate_ref[:, :LANES].T)

    qT = qT_scr[:LANES, :]
    s_c = jnp.dot(kc_ref[0], qT, preferred_element_type=F32)
    cend = krow(ncp) * CMP_STRIDE + (CMP_BLOCK - 1)
    bias_c = jnp.where(cend <= tq(ncp), 0.0, NEG_INF)
    bias_c = jnp.concatenate([bias_c, bias_c], axis=1)
    vcT = vcT_ref[0, pl.ds(goff, dh), :]
    p_sum = jnp.zeros((ncp, qb), F32)
    o_c = []
    for hp in range(hg // 2):
        sh = s_c[:, hp * 2 * qb:(hp + 1) * 2 * qb] + bias_c
        p = jnp.exp2(sh - jnp.max(sh, axis=0, keepdims=True))
        p = p * (1.0 / jnp.sum(p, axis=0, keepdims=True))
        p_sum = p_sum + p[:, :qb] + p[:, qb:]
        o_c.append(jnp.dot(vcT, p.astype(BF16), preferred_element_type=F32))
    o_c = jnp.concatenate(o_c, axis=1)

    imp = jnp.zeros((LANES, qb), F32)
    rem = p_sum
    for _ in range(3):
        part = rem.astype(BF16)
        imp = imp + jnp.dot(c2sT_ref[...], part, preferred_element_type=F32)
        rem = rem - part.astype(F32)
    nb = dh
    imp = imp[:nb]
    sb = krow(nb)
    cur = jnp.right_shift(tq(nb), int(math.log2(SLC_BLOCK)))
    forced = (sb == 0) | (sb == cur) | (sb == cur - 1)
    future = sb > cur
    imp = jnp.where(forced, FORCED_SCORE, jnp.where(future, -FORCED_SCORE, imp))
    rank = jnp.zeros((nb, qb), jnp.int32)
    for sp in range(n_slc):
        row = imp[sp:sp + 1, :]
        beats = (row > imp) | ((row == imp) & (sb > sp))
        rank = rank + beats.astype(jnp.int32)
    selneg = jnp.where((rank < top_n) & jnp.logical_not(future), 0.0, NEG_INF).astype(BF16)
    for h in range(hg):
        qT_scr[LANES:LANES + nb, h * qb:(h + 1) * qb] = selneg

    qT_aug = qT_scr[...]

    last_tile = ks_ref.shape[1] // kt - 1

    def tile_start(j):
        return pl.multiple_of(jnp.minimum(j, last_tile) * kt, kt)

    def scores(j):
        k0 = tile_start(j)
        ka = jnp.concatenate([ks_ref[0, pl.ds(k0, kt), :], et_ref[pl.ds(k0, kt), :]], axis=1)
        return jnp.dot(ka, qT_aug, preferred_element_type=F32)

    def fold(s, j, causal):
        vT = v_tiles(vsT_ref, tile_start(j), kt)
        if causal:
            bias = jnp.where(krow(kt, j * kt) <= tq(kt), 0.0, NEG_INF)
            bias = jnp.concatenate([bias, bias], axis=1)
        for hp in range(hg // 2):
            sl = slice(hp * 2 * qb, (hp + 1) * 2 * qb)
            sh = s[:, sl] + bias if causal else s[:, sl]
            m_old = m_scr[:, sl]
            m_new = jnp.maximum(m_old, jnp.max(sh, axis=0, keepdims=True))
            pv = jnp.dot(vT, jnp.exp2(sh - m_new).astype(BF16), preferred_element_type=F32)
            acc_scr[:, sl] = jnp.exp2(m_old - m_new) * acc_scr[:, sl] + pv
            m_scr[:, sl] = m_new

    def tile_pair(i, causal):
        s0, s1 = scores(2 * i), scores(2 * i + 1)
        fold(s0, 2 * i, causal)
        fold(s1, 2 * i + 1, causal)

    n_main = q0 // kt
    m_scr[...] = jnp.full(m_scr.shape, NEG_INF, F32)
    acc_scr[...] = jnp.zeros(acc_scr.shape, F32)

    def main_pair(i, carry):
        tile_pair(i, False)
        return carry

    lax.fori_loop(0, n_main // 2, main_pair, 0)
    tile_pair(n_main // 2, True)
    o_s = acc_scr[:dh, :] * (1.0 / acc_scr[dh:dh + 1, :])

    span = WINDOW + qb
    w0 = pl.multiple_of(jnp.maximum(q0 - WINDOW, 0), qb)
    dist = tq(span) - krow(span, w0)
    bias_w = jnp.where((dist >= 0) & (dist < WINDOW), 0.0, NEG_INF)
    bias_w = jnp.concatenate([bias_w, bias_w], axis=1)
    s_w = jnp.dot(kw_ref[0, pl.ds(w0, span), :], qT, preferred_element_type=F32)
    vwT = v_tiles(vwT_ref, w0, span)
    o_w = []
    for hp in range(hg // 2):
        sh = s_w[:, hp * 2 * qb:(hp + 1) * 2 * qb] + bias_w
        p = jnp.exp2(sh - jnp.max(sh, axis=0, keepdims=True)).astype(BF16)
        acc_w = jnp.dot(vwT, p, preferred_element_type=F32)
        o_w.append(acc_w[:dh] * (1.0 / acc_w[dh:dh + 1]))
    o_w = jnp.concatenate(o_w, axis=1)

    has_cmp = (tq(1) >= CMP_BLOCK - 1).astype(F32)
    outs = []
    for h in range(hg):
        grow = g * (3 * hg) + 3 * h
        sl = slice(h * qb, (h + 1) * qb)
        outs.append(gT_scr[pl.ds(grow, 1), :] * has_cmp * o_c[:, sl] + gT_scr[pl.ds(grow + 1, 1), :] * o_s[:, sl]
                    + gT_scr[pl.ds(grow + 2, 1), :] * o_w[:, sl])
    for j in range(hg // 2):
        pair = jnp.concatenate([outs[2 * j], outs[2 * j + 1]], axis=0)
        o_ref[:, j * LANES:(j + 1) * LANES] = pair.T.astype(o_ref.dtype)


def _nsa(qk, gates, kc, vcT, vsT, vwT, c2sT, et, B, S, *, kt=256):
    G, Hg, dh = NSA_GROUPS, NSA_HG, NSA_DH
    T = B * S
    nq = S // Q_BLOCK
    n_slc = S // SLC_BLOCK
    assert n_slc <= dh and 2 * dh == LANES and kt == 2 * Q_BLOCK and S % kt == 0 and S >= WINDOW + Q_BLOCK
    qw = Hg * dh
    ncp = kc.shape[1]
    k_col = G * qw // LANES
    g_col = gates.shape[1] // MXU_N - 1
    seq_k = lambda col: pl.BlockSpec((1, S, LANES), lambda b, g, c: (b, 0, col))
    seq_vT = pl.BlockSpec((1, S // LANES, LANES, LANES), lambda b, g, c: (b, 0, 0, 0))
    qk3 = qk.reshape(B, S, qk.shape[1])
    return pl.pallas_call(
        functools.partial(_nsa_body, kt=kt, top_n=min(SLC_TOPN, n_slc), n_slc=n_slc, hg=Hg, dh=dh),
        grid=(B, G, nq),
        in_specs=[pl.BlockSpec((Q_BLOCK, qw), lambda b, g, c: (b * nq + c, g)),
                  pl.BlockSpec((Q_BLOCK, MXU_N), lambda b, g, c: (b * nq + c, g_col)),
                  pl.BlockSpec((1, ncp, LANES), lambda b, g, c: (b, 0, 0)),
                  pl.BlockSpec((1, LANES, ncp), lambda b, g, c: (b, 0, 0)),
                  seq_k(k_col), seq_vT, seq_k(k_col + 1), seq_vT,
                  _resident(c2sT.shape), _resident(et.shape)],
        out_specs=pl.BlockSpec((Q_BLOCK, qw), lambda b, g, c: (b * nq + c, g)),
        out_shape=jax.ShapeDtypeStruct((T, G * qw), BF16),
        scratch_shapes=[pltpu.VMEM((2 * LANES, Hg * Q_BLOCK), BF16), pltpu.VMEM((LANES, Q_BLOCK), F32),
                        pltpu.VMEM((1, Hg * Q_BLOCK), F32), pltpu.VMEM((dh + BF16_ROWS, Hg * Q_BLOCK), F32)],
        compiler_params=pltpu.CompilerParams(dimension_semantics=("parallel", "arbitrary", "arbitrary"),
                                             vmem_limit_bytes=VMEM_LIMIT),
        name="nsa",
    )(qk, gates, kc, vcT, qk3, vsT, qk3, vwT, c2sT, et)


def _nsa_constants(S):
    ncp = S // CMP_STRIDE
    n_slc = S // SLC_BLOCK
    cmp_start = np.arange(ncp) * CMP_STRIDE
    s_start = np.arange(n_slc) * SLC_BLOCK
    overlap = np.clip(np.minimum(cmp_start[:, None] + CMP_BLOCK, s_start[None, :] + SLC_BLOCK)
                      - np.maximum(cmp_start[:, None], s_start[None, :]), 0, None)
    c2sT = np.zeros((LANES, ncp), np.float32)
    c2sT[:n_slc, :] = overlap.T.astype(np.float32) / CMP_BLOCK
    c2sT[:, ncp - 1] = 0.0
    et = np.zeros((S, LANES), np.float32)
    et[np.arange(S), np.arange(S) // SLC_BLOCK] = 1.0
    return jnp.asarray(c2sT, BF16), jnp.asarray(et, BF16)


def _ret_body(q_ref, k_ref, v_ref, decay_ref, zeta_ref, xi_ref, gch_ref, gn_ref, o_ref, *, C, dk):
    S = q_ref.shape[1]
    dv = gn_ref.shape[2]
    heads = LANES // dk
    lane_head = lax.broadcasted_iota(jnp.int32, (C, LANES), 1) // dk
    R = [jnp.zeros((LANES, dv), F32) for _ in range(heads)]
    for n in range(S // C):
        sl = slice(n * C, (n + 1) * C)
        q = q_ref[0, sl, :].astype(F32)
        k = k_ref[0, sl, :]
        kT = k.astype(F32).T
        for a in range(heads):
            qa = jnp.where(lane_head == a, q, 0.0)
            v = v_ref[0, sl, a * dv:(a + 1) * dv]
            inner = lax.dot_general(qa.astype(BF16), k, NT_DIMS, preferred_element_type=F32) * decay_ref[a]
            o = jnp.dot(inner.astype(BF16), v, preferred_element_type=F32)
            o = o + jnp.dot((qa * xi_ref[a]).astype(BF16), R[a].astype(BF16),
                            preferred_element_type=F32)
            R[a] = gch_ref[a] * R[a] + jnp.dot((kT * zeta_ref[a]).astype(BF16), v, preferred_element_type=F32)
            mu = jnp.mean(o, axis=-1, keepdims=True)
            d = o - mu
            var = jnp.mean(d * d, axis=-1, keepdims=True)
            o_ref[0, sl, a * dv:(a + 1) * dv] = d * lax.rsqrt(var + GN_EPS) * gn_ref[a]


def _retention(qk, v, gn_gain, B, S, *, C=RET_CHUNK):
    H, dk, dv = RET_HEADS, RET_DK, RET_DV
    hp = LANES // dk
    log_g = jnp.log(1.0 - 2.0 ** (-5.0 - jnp.arange(H, dtype=F32)))
    i = jnp.arange(C, dtype=F32)
    diff = i[:, None] - i[None, :]
    decay = jnp.where(diff >= 0, jnp.exp(jnp.maximum(diff, 0.0) * log_g[:, None, None]), 0.0)
    zeta = jnp.exp((C - 1.0 - i)[None, :] * log_g[:, None]).reshape(H, 1, C)
    xi = jnp.broadcast_to(jnp.exp((i + 1.0)[None, :] * log_g[:, None])[:, :, None], (H, C, LANES))
    gch = jnp.broadcast_to(jnp.exp(C * log_g)[:, None, None], (H, 1, dv))
    per_pair = lambda shape: pl.BlockSpec((hp,) + shape, lambda b, p: (p, 0, 0))
    k_col = H * dk // LANES
    qk3 = qk.reshape(B, S, qk.shape[1])
    return pl.pallas_call(
        functools.partial(_ret_body, C=C, dk=dk),
        grid=(B, H // hp),
        in_specs=[pl.BlockSpec((1, S, LANES), lambda b, p: (b, 0, p)),
                  pl.BlockSpec((1, S, LANES), lambda b, p: (b, 0, k_col + p)),
                  pl.BlockSpec((1, S, hp * dv), lambda b, p: (b, 0, p)),
                  per_pair((C, C)), per_pair((1, C)), per_pair((C, LANES)), per_pair((1, dv)),
                  per_pair((1, dv))],
        out_specs=pl.BlockSpec((1, S, hp * dv), lambda b, p: (b, 0, p)),
        out_shape=jax.ShapeDtypeStruct((B, S, H * dv), F32),
        compiler_params=pltpu.CompilerParams(dimension_semantics=("parallel", "parallel"),
                                             vmem_limit_bytes=VMEM_LIMIT),
        name="retention",
    )(qk3, qk3, v.reshape(B, S, H * dv), decay, zeta, xi, gch, gn_gain.reshape(H, 1, dv))


def _merge_body(x_ref, a_ref, r_ref, gr_ref, ga_ref, gb_ref, wa_ref, wr_ref, wo_ref, o_ref):
    g = gr_ref[...]
    r = (g * jax.nn.sigmoid(g) * r_ref[...]).astype(BF16)
    ya = jnp.dot(a_ref[...], wa_ref[...], preferred_element_type=F32)
    yr = jnp.dot(r, wr_ref[...], preferred_element_type=F32)
    mixed = jax.nn.sigmoid(ga_ref[...]) * ya + jax.nn.sigmoid(gb_ref[...]) * yr
    o_ref[...] = x_ref[...] + jnp.dot(mixed.astype(BF16), wo_ref[...], preferred_element_type=F32)


def _merge(x2d, a, r, gates, wa, wr, wo, *, tm=512):
    T, D = x2d.shape
    col = lambda j: pl.BlockSpec((tm, D), lambda i: (i, j))
    return pl.pallas_call(
        _merge_body,
        grid=(T // tm,),
        in_specs=[col(0), col(0), col(0), col(0), col(1), col(2),
                  _resident(wa.shape), _resident(wr.shape), _resident(wo.shape)],
        out_specs=col(0),
        out_shape=jax.ShapeDtypeStruct((T, D), F32),
        compiler_params=pltpu.CompilerParams(dimension_semantics=("parallel",),
                                             vmem_limit_bytes=VMEM_LIMIT),
        name="merge",
    )(x2d, a, r, gates, gates, gates, wa, wr, wo)


def _layer(x, p, final_norm):
    B, S, D = x.shape
    T = B * S
    G, Hg, dh = NSA_GROUPS, NSA_HG, NSA_DH
    H, dk, dv = RET_HEADS, RET_DK, RET_DV
    bf = lambda w: w.astype(BF16)

    x1 = _ffn(x.reshape(T, D), p["ffn1_norm"], bf(p["ffn1_w_gate"]), bf(p["ffn1_w_up"]),
              bf(p["ffn1_w_down"]), final_norm, final=False)

    w = p["w_in"]
    widths = (G * Hg * dh,) + (G * dh,) * 6 + (3 * G * Hg, H * dk, H * dk, H * dv, H * dv, 2 * D)
    offs = np.concatenate([[0], np.cumsum(widths)])
    seg = lambda i: w[:, int(offs[i]):int(offs[i + 1])]
    (q_a, kc_a, vc_a, ks_a, vs_a, kw_a, vw_a, g_a, q_r, k_r, v_r, g_r, g_m) = [seg(i) for i in range(13)]
    gate_pad = MXU_N - g_a.shape[1]
    wa = bf(jnp.concatenate([q_a * (dh ** -0.5 * LOG2E), ks_a, kw_a], axis=1))
    wb = bf(jnp.concatenate([q_r * dk ** -0.5, k_r], axis=1))
    wc = bf(jnp.concatenate([v_r, kc_a, vc_a, vs_a, vw_a], axis=1))
    wd = bf(jnp.concatenate([g_r, g_m, g_a, jnp.zeros((D, gate_pad), F32)], axis=1))
    pos = jnp.arange(S)
    tab_a = _rope_tables(pos, ROPE_DIM, dh, ROPE_THETA)
    tab_b = _rope_tables(pos, dk, dk, RET_ROPE_THETA)
    qk_n, qk_r, v_ret, kc_tok, vc_tok, vsT, vwT, gates = _proj(
        x1, p["mix_norm"], wa, wb, wc, wd, tab_a, tab_b, B, S)

    chunk = lambda t: t.reshape(B, S // CMP_STRIDE, CMP_STRIDE * G * dh)
    kw1, kw2, pe = _compress_weights(p["cmp_k_w1"], p["cmp_k_w2"], p["cmp_pos_emb"])
    vw1, vw2, _ = _compress_weights(p["cmp_v_w1"], p["cmp_v_w2"], p["cmp_pos_emb"])
    cmp_end = jnp.arange(S // CMP_STRIDE) * CMP_STRIDE + (CMP_BLOCK - 1)
    kc, vcT = _compress(chunk(kc_tok), chunk(vc_tok), pe, kw1, kw2, vw1, vw2,
                        _rope_tables(cmp_end, ROPE_DIM, dh, ROPE_THETA))
    c2sT, et = _nsa_constants(S)
    a_out = _nsa(qk_n, gates, kc, vcT, vsT, vwT, c2sT, et, B, S)
    r_out = _retention(qk_r, v_ret, p["ret_gn_gain"], B, S).reshape(T, H * dv)

    x2 = _merge(x1, a_out, r_out, gates, bf(p["w_branch_nsa"]), bf(p["w_branch_ret"]), bf(p["w_out"]))
    x3 = _ffn(x2, p["ffn2_norm"], bf(p["ffn2_w_gate"]), bf(p["ffn2_w_up"]), bf(p["ffn2_w_down"]),
              final_norm, final=True)
    return x3.reshape(B, S, D)


def kernel(x, ffn1_norm, ffn1_w_gate, ffn1_w_up, ffn1_w_down, mix_norm, w_in, cmp_pos_emb, cmp_k_w1,
           cmp_k_w2, cmp_v_w1, cmp_v_w2, ret_gn_gain, w_branch_nsa, w_branch_ret, w_out, ffn2_norm,
           ffn2_w_gate, ffn2_w_up, ffn2_w_down, final_norm):
    assert ffn1_norm.shape[0] == 1, "single-layer stack"
    names = ("ffn1_norm", "ffn1_w_gate", "ffn1_w_up", "ffn1_w_down", "mix_norm", "w_in", "cmp_pos_emb",
             "cmp_k_w1", "cmp_k_w2", "cmp_v_w1", "cmp_v_w2", "ret_gn_gain", "w_branch_nsa",
             "w_branch_ret", "w_out", "ffn2_norm", "ffn2_w_gate", "ffn2_w_up", "ffn2_w_down")
    vals = (ffn1_norm, ffn1_w_gate, ffn1_w_up, ffn1_w_down, mix_norm, w_in, cmp_pos_emb, cmp_k_w1,
            cmp_k_w2, cmp_v_w1, cmp_v_w2, ret_gn_gain, w_branch_nsa, w_branch_ret, w_out, ffn2_norm,
            ffn2_w_gate, ffn2_w_up, ffn2_w_down)
    p = {n: v[0] for n, v in zip(names, vals)}
    return _layer(x, p, final_norm)
```

```python
import functools
import math

import numpy as np
import jax
import jax.numpy as jnp
from jax import lax
from jax.experimental import pallas as pl
from jax.experimental.pallas import tpu as pltpu

F32 = jnp.float32
BF16 = jnp.bfloat16

NSA_HEADS = 16
NSA_GROUPS = 2
NSA_HG = NSA_HEADS // NSA_GROUPS
NSA_DH = 64
CMP_BLOCK = 32
CMP_STRIDE = 16
SLC_BLOCK = 64
SLC_TOPN = 16
WINDOW = 512
Q_BLOCK = 128
ROPE_THETA = 500000.0
ROPE_DIM = NSA_DH // 4
FORCED_SCORE = 1.0e4
RET_HEADS = 8
RET_DK = 64
RET_DV = 128
RET_CHUNK = 128
RET_ROPE_THETA = 10000.0
EPS = 1e-6
GN_EPS = 1e-5
NEG_INF = -1e30
LOG2E = math.log2(math.e)

LANES = 128
BF16_ROWS = 16
MXU_N = 256
VMEM_LIMIT = 56 * 1024 * 1024

NT_DIMS = (((1,), (1,)), ((), ()))


def _rms(x, g, eps=EPS):
    return x * lax.rsqrt(jnp.mean(x * x, axis=-1, keepdims=True) + eps) * g


def _resident(shape):
    nd = len(shape)
    return pl.BlockSpec(shape, lambda *_: (0,) * nd, pipeline_mode=pl.Buffered(1))


def _ffn_body(x_ref, g_ref, wg_ref, wu_ref, wd_ref, fg_ref, o_ref, *, tf, final):
    x = x_ref[...]
    h = _rms(x, g_ref[...]).astype(BF16)
    acc = jnp.zeros(x.shape, F32)
    for c in range(wg_ref.shape[1] // tf):
        sl = slice(c * tf, (c + 1) * tf)
        g = jnp.dot(h, wg_ref[:, sl], preferred_element_type=F32)
        u = jnp.dot(h, wu_ref[:, sl], preferred_element_type=F32)
        a = (g * jax.nn.sigmoid(g) * u).astype(BF16)
        acc = acc + jnp.dot(a, wd_ref[sl, :], preferred_element_type=F32)
    y = x + 0.5 * acc
    if final:
        y = _rms(y, fg_ref[...])
    o_ref[...] = y


def _ffn(x2d, gain, wg, wu, wd, final_gain, *, final, tm=512):
    T, D = x2d.shape
    F = wg.shape[1]
    tf = F // 2 if (F // 2) % LANES == 0 else F
    tok = pl.BlockSpec((tm, D), lambda i: (i, 0))
    return pl.pallas_call(
        functools.partial(_ffn_body, tf=tf, final=final),
        grid=(T // tm,),
        in_specs=[tok, _resident((1, D)), _resident((D, F)), _resident((D, F)), _resident((F, D)),
                  _resident((1, D))],
        out_specs=tok,
        out_shape=jax.ShapeDtypeStruct((T, D), F32),
        compiler_params=pltpu.CompilerParams(dimension_semantics=("parallel",),
                                             vmem_limit_bytes=VMEM_LIMIT),
        name="ffn_final" if final else "ffn",
    )(x2d, gain.reshape(1, D), wg, wu, wd, final_gain.reshape(1, D))


def _rope_slab(y, tab_ref, shift):
    return (y * tab_ref[0] + pltpu.roll(y, LANES - shift, 1) * tab_ref[1]
            + pltpu.roll(y, shift, 1) * tab_ref[2])


def _proj_body(x_ref, g_ref, wa_ref, wb_ref, wc_ref, wd_ref, ta_ref, tb_ref,
               oa_ref, ob_ref, ov_ref, okc_ref, ovc_ref, ovsT_ref, ovwT_ref, od_ref):
    h = _rms(x_ref[...], g_ref[...]).astype(BF16)
    tm = h.shape[0]
    nv = ov_ref.shape[1] // LANES

    def slabs(w_ref):
        for c in range(w_ref.shape[1] // MXU_N):
            y = jnp.dot(h, w_ref[:, c * MXU_N:(c + 1) * MXU_N], preferred_element_type=F32)
            for s in range(MXU_N // LANES):
                yield c * (MXU_N // LANES) + s, y[:, s * LANES:(s + 1) * LANES]

    def put(o_ref, i, ys):
        o_ref[:, i * LANES:(i + 1) * LANES] = ys.astype(o_ref.dtype)

    def put_t(o_ref, ys):
        for r in range(tm // LANES):
            o_ref[0, r] = ys[r * LANES:(r + 1) * LANES, :].T.astype(o_ref.dtype)

    for i, ys in slabs(wa_ref):
        put(oa_ref, i, _rope_slab(ys, ta_ref, ROPE_DIM // 2))
    for i, ys in slabs(wb_ref):
        put(ob_ref, i, _rope_slab(ys, tb_ref, RET_DK // 2))
    for i, ys in slabs(wc_ref):
        if i < nv:
            put(ov_ref, i, ys)
        elif i == nv:
            put(okc_ref, 0, ys)
        elif i == nv + 1:
            put(ovc_ref, 0, ys)
        elif i == nv + 2:
            put_t(ovsT_ref, ys)
        else:
            put_t(ovwT_ref, ys)
    for i, ys in slabs(wd_ref):
        put(od_ref, i, ys)


def _proj(x2d, gain, wa, wb, wc, wd, tab_a, tab_b, B, S, *, tm=512):
    T, D = x2d.shape
    spt = S // tm
    nv = wc.shape[1] - 4 * LANES
    tok = lambda n: pl.BlockSpec((tm, n), lambda i: (i, 0))
    tab = pl.BlockSpec((3, tm, LANES), lambda i: (0, i % spt, 0))
    tr = pl.BlockSpec((1, tm // LANES, LANES, LANES), lambda i: (i // spt, i % spt, 0, 0))
    return pl.pallas_call(
        _proj_body,
        grid=(T // tm,),
        in_specs=[tok(D), _resident((1, D)), _resident(wa.shape), _resident(wb.shape),
                  _resident(wc.shape), _resident(wd.shape), tab, tab],
        out_specs=[tok(wa.shape[1]), tok(wb.shape[1]), tok(nv), tok(LANES), tok(LANES), tr, tr,
                   tok(wd.shape[1])],
        out_shape=[jax.ShapeDtypeStruct((T, wa.shape[1]), BF16),
                   jax.ShapeDtypeStruct((T, wb.shape[1]), BF16),
                   jax.ShapeDtypeStruct((T, nv), BF16),
                   jax.ShapeDtypeStruct((T, LANES), BF16),
                   jax.ShapeDtypeStruct((T, LANES), BF16),
                   jax.ShapeDtypeStruct((B, S // LANES, LANES, LANES), BF16),
                   jax.ShapeDtypeStruct((B, S // LANES, LANES, LANES), BF16),
                   jax.ShapeDtypeStruct((T, wd.shape[1]), F32)],
        compiler_params=pltpu.CompilerParams(dimension_semantics=("parallel",),
                                             vmem_limit_bytes=VMEM_LIMIT),
        name="proj",
    )(x2d, gain.reshape(1, D), wa, wb, wc, wd, tab_a, tab_b)


def _rope_tables(pos, rot_dim, head_dim, theta):
    half = rot_dim // 2
    freqs = theta ** (-(jnp.arange(half, dtype=F32) * 2.0 / rot_dim))
    ang = pos.astype(F32)[:, None] * freqs[None, :]
    cos, sin = jnp.cos(ang), jnp.sin(ang)
    n = pos.shape[0]
    rest = head_dim - rot_dim
    cos_h = jnp.concatenate([cos, cos, jnp.ones((n, rest), F32)], axis=-1)
    sl_h = jnp.concatenate([-sin, jnp.zeros((n, half + rest), F32)], axis=-1)
    sr_h = jnp.concatenate([jnp.zeros((n, half), F32), sin, jnp.zeros((n, rest), F32)], axis=-1)
    rep = LANES // head_dim
    return jnp.stack([jnp.tile(t, (1, rep)) for t in (cos_h, sl_h, sr_h)])


def _gelu_tanh(x):
    return 0.5 * x * (1.0 + jnp.tanh(math.sqrt(2.0 / math.pi) * (x + 0.044715 * (x * x * x))))


def _compress_body(kt_ref, vt_ref, pe_ref, kw1_ref, kw2_ref, vw1_ref, vw2_ref, tab_ref,
                   kc_ref, vcT_ref):
    pe = pe_ref[...]
    n = kt_ref.shape[1]

    def mlp(tok_ref, w1_ref, w2_ref):
        c = tok_ref[0].astype(F32)
        lo = (c + pe[0:1]).astype(BF16)
        hi = (c + pe[1:2]).astype(BF16)
        out = jnp.zeros((n, LANES), F32)
        for g in range(NSA_GROUPS):
            top = jnp.dot(lo, w1_ref[g, 0], preferred_element_type=F32)
            bot = jnp.dot(hi, w1_ref[g, 1], preferred_element_type=F32)
            hid = _gelu_tanh(top + pltpu.roll(bot, n - 1, 0))
            out = out + jnp.dot(hid.astype(BF16), w2_ref[g], preferred_element_type=F32)
        return out

    kc_ref[0] = _rope_slab(mlp(kt_ref, kw1_ref, kw2_ref), tab_ref, ROPE_DIM // 2).astype(kc_ref.dtype)
    vc = mlp(vt_ref, vw1_ref, vw2_ref)
    for r in range(n // LANES):
        vcT_ref[0, :, r * LANES:(r + 1) * LANES] = vc[r * LANES:(r + 1) * LANES, :].T.astype(vcT_ref.dtype)


def _compress(k_tok, v_tok, pe, kw1, kw2, vw1, vw2, tab):
    B, n, w = k_tok.shape
    tok = pl.BlockSpec((1, n, w), lambda b: (b, 0, 0))
    return pl.pallas_call(
        _compress_body,
        grid=(B,),
        in_specs=[tok, tok, _resident(pe.shape), _resident(kw1.shape), _resident(kw2.shape),
                  _resident(vw1.shape), _resident(vw2.shape), _resident(tab.shape)],
        out_specs=[pl.BlockSpec((1, n, LANES), lambda b: (b, 0, 0)),
                   pl.BlockSpec((1, LANES, n), lambda b: (b, 0, 0))],
        out_shape=[jax.ShapeDtypeStruct((B, n, LANES), BF16), jax.ShapeDtypeStruct((B, LANES, n), BF16)],
        compiler_params=pltpu.CompilerParams(dimension_semantics=("parallel",),
                                             vmem_limit_bytes=VMEM_LIMIT),
        name="compress",
    )(k_tok, v_tok, pe, kw1, kw2, vw1, vw2, tab)


def _compress_weights(w1, w2, pe):
    G, dh = NSA_GROUPS, NSA_DH
    hid = w1.shape[1]
    halves = w1.reshape(2, CMP_STRIDE, dh, hid)
    w1e = jnp.zeros((G, 2, CMP_STRIDE, G, dh, hid), F32)
    w2e = jnp.zeros((G, hid, G, dh), F32)
    for g in range(G):
        w1e = w1e.at[g, :, :, g].set(halves)
        w2e = w2e.at[g, :, g].set(w2)
    pe_e = jnp.broadcast_to(pe.reshape(2, CMP_STRIDE, 1, dh), (2, CMP_STRIDE, G, dh))
    return (w1e.reshape(G, 2, CMP_STRIDE * G * dh, hid).astype(BF16), w2e.reshape(G, hid, G * dh).astype(BF16),
            pe_e.reshape(2, CMP_STRIDE * G * dh))


def _nsa_body(q_ref, gate_ref, kc_ref, vcT_ref, ks_ref, vsT_ref, kw_ref, vwT_ref, c2sT_ref, et_ref,
              o_ref, qT_scr, gT_scr, m_scr, acc_scr, *, kt, top_n, n_slc, hg, dh):
    qb = q_ref.shape[0]
    cols = hg * qb
    ncp = kc_ref.shape[1]
    g = pl.program_id(1)
    c = pl.program_id(2)
    q0 = c * qb
    goff = pl.multiple_of(g * dh, dh)
    ones_rows = jnp.ones((BF16_ROWS, 1), BF16)

    def v_tiles(vT_ref, k0, n):
        t = vT_ref[0, pl.ds(k0 // LANES, n // LANES), pl.ds(goff, dh), :]
        vT = jnp.concatenate([t[i] for i in range(n // LANES)], axis=1)
        return jnp.concatenate([vT, jnp.broadcast_to(ones_rows, (BF16_ROWS, n))], axis=0)

    def tq(n):
        return q0 + lax.broadcasted_iota(jnp.int32, (n, qb), 1)

    def krow(n, k0=0):
        return k0 + lax.broadcasted_iota(jnp.int32, (n, qb), 0)

    qT_scr[...] = jnp.zeros(qT_scr.shape, BF16)
    for r in range(qb // LANES):
        rows = slice(r * LANES, (r + 1) * LANES)
        for j in range(hg // 2):
            t = q_ref[rows, j * LANES:(j + 1) * LANES].astype(F32).T.astype(BF16)
            qT_scr[pl.ds(goff, dh), (2 * j) * qb + r * LANES:(2 * j) * qb + (r + 1) * LANES] = t[:dh]
            qT_scr[pl.ds(goff, dh), (2 * j + 1) * qb + r * LANES:(2 * j + 1) * qb + (r + 1) * LANES] = t[dh:]
        gT_scr[:, rows] = jax.nn.sigmoid(gate_ref[rows, :LANES].T)

    qT = qT_scr[:LANES, :]
    s_c = jnp.dot(kc_ref[0], qT, preferred_element_type=F32)
    cend = krow(ncp) * CMP_STRIDE + (CMP_BLOCK - 1)
    bias_c = jnp.where(cend <= tq(ncp), 0.0, NEG_INF)
    bias_c = jnp.concatenate([bias_c, bias_c], axis=1)
    vcT = vcT_ref[0, pl.ds(goff, dh), :]
    p_sum = jnp.zeros((ncp, qb), F32)
    o_c = []
    for hp in range(hg // 2):
        sh = s_c[:, hp * 2 * qb:(hp + 1) * 2 * qb] + bias_c
        p = jnp.exp2(sh - jnp.max(sh, axis=0, keepdims=True))
        p = p * (1.0 / jnp.sum(p, axis=0, keepdims=True))
        p_sum = p_sum + p[:, :qb] + p[:, qb:]
        o_c.append(jnp.dot(vcT, p.astype(BF16), preferred_element_type=F32))
    o_c = jnp.concatenate(o_c, axis=1)

    imp = jnp.zeros((LANES, qb), F32)
    rem = p_sum
    for _ in range(3):
        part = rem.astype(BF16)
        imp = imp + jnp.dot(c2sT_ref[...], part, preferred_element_type=F32)
        rem = rem - part.astype(F32)
    nb = dh
    imp = imp[:nb]
    sb = krow(nb)
    cur = jnp.right_shift(tq(nb), int(math.log2(SLC_BLOCK)))
    forced = (sb == 0) | (sb == cur) | (sb == cur - 1)
    future = sb > cur
    imp = jnp.where(forced, FORCED_SCORE, jnp.where(future, -FORCED_SCORE, imp))
    sub = 8
    rank_cols = []
    for lc in range(qb // LANES):
        imp_l = imp[:, lc * LANES:(lc + 1) * LANES]
        sb_l = sb[:, :LANES]
        chunks = [imp_l[r * sub:(r + 1) * sub] for r in range(nb // sub)]
        ranks = [jnp.zeros((sub, LANES), jnp.int32) for _ in chunks]
        for sp in range(n_slc):
            row = imp_l[sp:sp + 1, :]
            for r, blk in enumerate(chunks):
                if r * sub > sp:
                    beats = row >= blk
                elif (r + 1) * sub - 1 <= sp:
                    beats = row > blk
                else:
                    beats = (row > blk) | ((row == blk) & (sb_l[r * sub:(r + 1) * sub] > sp))
                ranks[r] = ranks[r] + jnp.where(beats, 1, 0)
        rank_cols.append(jnp.concatenate(ranks, axis=0))
    rank = jnp.concatenate(rank_cols, axis=1)
    selneg = jnp.where((rank < top_n) & jnp.logical_not(future), 0.0, NEG_INF).astype(BF16)
    for h in range(hg):
        qT_scr[LANES:LANES + nb, h * qb:(h + 1) * qb] = selneg

    qT_aug = qT_scr[...]

    def scores(j):
        k0 = pl.multiple_of(j * kt, kt)
        ka = jnp.concatenate([ks_ref[0, pl.ds(k0, kt), :], et_ref[pl.ds(k0, kt), :]], axis=1)
        return jnp.dot(ka, qT_aug, preferred_element_type=F32)

    def fold(s, j, causal):
        vT = v_tiles(vsT_ref, pl.multiple_of(j * kt, kt), kt)
        if causal:
            bias = jnp.where(krow(kt, j * kt) <= tq(kt), 0.0, NEG_INF)
            bias = jnp.concatenate([bias, bias], axis=1)
        for hp in range(hg // 2):
            sl = slice(hp * 2 * qb, (hp + 1) * 2 * qb)
            sh = s[:, sl] + bias if causal else s[:, sl]
            m_old = m_scr[:, sl]
            m_new = jnp.maximum(m_old, jnp.max(sh, axis=0, keepdims=True))
            pv = jnp.dot(vT, jnp.exp2(sh - m_new).astype(BF16), preferred_element_type=F32)
            acc_scr[:, sl] = jnp.exp2(m_old - m_new) * acc_scr[:, sl] + pv
            m_scr[:, sl] = m_new

    m_scr[...] = jnp.full(m_scr.shape, NEG_INF, F32)
    acc_scr[...] = jnp.zeros(acc_scr.shape, F32)

    def main_pair(i, carry):
        s0, s1 = scores(2 * i), scores(2 * i + 1)
        fold(s0, 2 * i, False)
        fold(s1, 2 * i + 1, False)
        return carry

    lax.fori_loop(0, c // 2, main_pair, 0)

    @pl.when(c % 2 == 1)
    def _():
        fold(scores(c - 1), c - 1, False)

    fold(scores(c), c, True)
    o_s = acc_scr[:dh, :] * (1.0 / acc_scr[dh:dh + 1, :])

    span = WINDOW + qb
    w0 = pl.multiple_of(jnp.maximum(q0 - WINDOW, 0), qb)
    dist = tq(span) - krow(span, w0)
    bias_w = jnp.where((dist >= 0) & (dist < WINDOW), 0.0, NEG_INF)
    bias_w = jnp.concatenate([bias_w, bias_w], axis=1)
    s_w = jnp.dot(kw_ref[0, pl.ds(w0, span), :], qT, preferred_element_type=F32)
    vwT = v_tiles(vwT_ref, w0, span)
    o_w = []
    for hp in range(hg // 2):
        sh = s_w[:, hp * 2 * qb:(hp + 1) * 2 * qb] + bias_w
        p = jnp.exp2(sh - jnp.max(sh, axis=0, keepdims=True)).astype(BF16)
        acc_w = jnp.dot(vwT, p, preferred_element_type=F32)
        o_w.append(acc_w[:dh] * (1.0 / acc_w[dh:dh + 1]))
    o_w = jnp.concatenate(o_w, axis=1)

    has_cmp = (tq(1) >= CMP_BLOCK - 1).astype(F32)
    gts = gT_scr[pl.ds(pl.multiple_of(g * (3 * hg), 8), 3 * hg), :]
    outs = []
    for h in range(hg):
        sl = slice(h * qb, (h + 1) * qb)
        outs.append(gts[3 * h:3 * h + 1] * has_cmp * o_c[:, sl] + gts[3 * h + 1:3 * h + 2] * o_s[:, sl]
                    + gts[3 * h + 2:3 * h + 3] * o_w[:, sl])
    for j in range(hg // 2):
        pair = jnp.concatenate([outs[2 * j], outs[2 * j + 1]], axis=0)
        for r in range(qb // LANES):
            rows = slice(r * LANES, (r + 1) * LANES)
            o_ref[rows, j * LANES:(j + 1) * LANES] = pair[:, rows].T.astype(o_ref.dtype)


def _nsa(qk, gates, kc, vcT, vsT, vwT, c2sT, et, B, S, *, qb=256):
    G, Hg, dh = NSA_GROUPS, NSA_HG, NSA_DH
    T = B * S
    nq = S // qb
    n_slc = S // SLC_BLOCK
    kt = qb
    assert n_slc <= dh and 2 * dh == LANES and qb % LANES == 0 and S % qb == 0 and WINDOW % qb == 0
    assert S >= WINDOW + qb
    qw = Hg * dh
    ncp = kc.shape[1]
    k_col = G * qw // LANES
    g_col = gates.shape[1] // MXU_N - 1
    seq_k = lambda col: pl.BlockSpec((1, S, LANES), lambda b, g, c: (b, 0, col))
    seq_vT = pl.BlockSpec((1, S // LANES, LANES, LANES), lambda b, g, c: (b, 0, 0, 0))
    qk3 = qk.reshape(B, S, qk.shape[1])
    return pl.pallas_call(
        functools.partial(_nsa_body, kt=kt, top_n=min(SLC_TOPN, n_slc), n_slc=n_slc, hg=Hg, dh=dh),
        grid=(B, G, nq),
        in_specs=[pl.BlockSpec((qb, qw), lambda b, g, c: (b * nq + c, g)),
                  pl.BlockSpec((qb, MXU_N), lambda b, g, c: (b * nq + c, g_col)),
                  pl.BlockSpec((1, ncp, LANES), lambda b, g, c: (b, 0, 0)),
                  pl.BlockSpec((1, LANES, ncp), lambda b, g, c: (b, 0, 0)),
                  seq_k(k_col), seq_vT, seq_k(k_col + 1), seq_vT,
                  _resident(c2sT.shape), _resident(et.shape)],
        out_specs=pl.BlockSpec((qb, qw), lambda b, g, c: (b * nq + c, g)),
        out_shape=jax.ShapeDtypeStruct((T, G * qw), BF16),
        scratch_shapes=[pltpu.VMEM((2 * LANES, Hg * qb), BF16), pltpu.VMEM((LANES, qb), F32),
                        pltpu.VMEM((1, Hg * qb), F32), pltpu.VMEM((dh + BF16_ROWS, Hg * qb), F32)],
        compiler_params=pltpu.CompilerParams(dimension_semantics=("parallel", "arbitrary", "arbitrary"),
                                             vmem_limit_bytes=VMEM_LIMIT),
        name="nsa",
    )(qk, gates, kc, vcT, qk3, vsT, qk3, vwT, c2sT, et)


def _nsa_constants(S):
    ncp = S // CMP_STRIDE
    n_slc = S // SLC_BLOCK
    cmp_start = np.arange(ncp) * CMP_STRIDE
    s_start = np.arange(n_slc) * SLC_BLOCK
    overlap = np.clip(np.minimum(cmp_start[:, None] + CMP_BLOCK, s_start[None, :] + SLC_BLOCK)
                      - np.maximum(cmp_start[:, None], s_start[None, :]), 0, None)
    c2sT = np.zeros((LANES, ncp), np.float32)
    c2sT[:n_slc, :] = overlap.T.astype(np.float32) / CMP_BLOCK
    c2sT[:, ncp - 1] = 0.0
    et = np.zeros((S, LANES), np.float32)
    et[np.arange(S), np.arange(S) // SLC_BLOCK] = 1.0
    return jnp.asarray(c2sT, BF16), jnp.asarray(et, BF16)


def _ret_body(q_ref, k_ref, v_ref, decay_ref, zeta_ref, xi_ref, gch_ref, gn_ref, o_ref, *, C, dk):
    S = q_ref.shape[1]
    dv = gn_ref.shape[2]
    heads = LANES // dk
    lane_head = lax.broadcasted_iota(jnp.int32, (C, LANES), 1) // dk
    R = [jnp.zeros((LANES, dv), F32) for _ in range(heads)]
    for n in range(S // C):
        sl = slice(n * C, (n + 1) * C)
        q = q_ref[0, sl, :].astype(F32)
        k = k_ref[0, sl, :]
        kT = k.astype(F32).T
        for a in range(heads):
            qa = jnp.where(lane_head == a, q, 0.0)
            v = v_ref[0, sl, a * dv:(a + 1) * dv]
            inner = lax.dot_general(qa.astype(BF16), k, NT_DIMS, preferred_element_type=F32) * decay_ref[a]
            o = jnp.dot(inner.astype(BF16), v, preferred_element_type=F32)
            o = o + jnp.dot((qa * xi_ref[a]).astype(BF16), R[a].astype(BF16),
                            preferred_element_type=F32)
            R[a] = gch_ref[a] * R[a] + jnp.dot((kT * zeta_ref[a]).astype(BF16), v, preferred_element_type=F32)
            mu = jnp.mean(o, axis=-1, keepdims=True)
            d = o - mu
            var = jnp.mean(d * d, axis=-1, keepdims=True)
            o_ref[0, sl, a * dv:(a + 1) * dv] = d * lax.rsqrt(var + GN_EPS) * gn_ref[a]


def _retention(qk, v, gn_gain, B, S, *, C=RET_CHUNK):
    H, dk, dv = RET_HEADS, RET_DK, RET_DV
    hp = LANES // dk
    log_g = jnp.log(1.0 - 2.0 ** (-5.0 - jnp.arange(H, dtype=F32)))
    i = jnp.arange(C, dtype=F32)
    diff = i[:, None] - i[None, :]
    decay = jnp.where(diff >= 0, jnp.exp(jnp.maximum(diff, 0.0) * log_g[:, None, None]), 0.0)
    zeta = jnp.exp((C - 1.0 - i)[None, :] * log_g[:, None]).reshape(H, 1, C)
    xi = jnp.broadcast_to(jnp.exp((i + 1.0)[None, :] * log_g[:, None])[:, :, None], (H, C, LANES))
    gch = jnp.broadcast_to(jnp.exp(C * log_g)[:, None, None], (H, 1, dv))
    per_pair = lambda shape: pl.BlockSpec((hp,) + shape, lambda b, p: (p, 0, 0))
    k_col = H * dk // LANES
    qk3 = qk.reshape(B, S, qk.shape[1])
    return pl.pallas_call(
        functools.partial(_ret_body, C=C, dk=dk),
        grid=(B, H // hp),
        in_specs=[pl.BlockSpec((1, S, LANES), lambda b, p: (b, 0, p)),
                  pl.BlockSpec((1, S, LANES), lambda b, p: (b, 0, k_col + p)),
                  pl.BlockSpec((1, S, hp * dv), lambda b, p: (b, 0, p)),
                  per_pair((C, C)), per_pair((1, C)), per_pair((C, LANES)), per_pair((1, dv)),
                  per_pair((1, dv))],
        out_specs=pl.BlockSpec((1, S, hp * dv), lambda b, p: (b, 0, p)),
        out_shape=jax.ShapeDtypeStruct((B, S, H * dv), F32),
        compiler_params=pltpu.CompilerParams(dimension_semantics=("parallel", "parallel"),
                                             vmem_limit_bytes=VMEM_LIMIT),
        name="retention",
    )(qk3, qk3, v.reshape(B, S, H * dv), decay, zeta, xi, gch, gn_gain.reshape(H, 1, dv))


def _merge_body(x_ref, a_ref, r_ref, gr_ref, ga_ref, gb_ref, wa_ref, wr_ref, wo_ref, o_ref):
    g = gr_ref[...]
    r = (g * jax.nn.sigmoid(g) * r_ref[...]).astype(BF16)
    ya = jnp.dot(a_ref[...], wa_ref[...], preferred_element_type=F32)
    yr = jnp.dot(r, wr_ref[...], preferred_element_type=F32)
    mixed = jax.nn.sigmoid(ga_ref[...]) * ya + jax.nn.sigmoid(gb_ref[...]) * yr
    o_ref[...] = x_ref[...] + jnp.dot(mixed.astype(BF16), wo_ref[...], preferred_element_type=F32)


def _merge(x2d, a, r, gates, wa, wr, wo, *, tm=512):
    T, D = x2d.shape
    col = lambda j: pl.BlockSpec((tm, D), lambda i: (i, j))
    return pl.pallas_call(
        _merge_body,
        grid=(T // tm,),
        in_specs=[col(0), col(0), col(0), col(0), col(1), col(2),
                  _resident(wa.shape), _resident(wr.shape), _resident(wo.shape)],
        out_specs=col(0),
        out_shape=jax.ShapeDtypeStruct((T, D), F32),
        compiler_params=pltpu.CompilerParams(dimension_semantics=("parallel",),
                                             vmem_limit_bytes=VMEM_LIMIT),
        name="merge",
    )(x2d, a, r, gates, gates, gates, wa, wr, wo)


def _layer(x, p, final_norm):
    B, S, D = x.shape
    T = B * S
    G, Hg, dh = NSA_GROUPS, NSA_HG, NSA_DH
    H, dk, dv = RET_HEADS, RET_DK, RET_DV
    bf = lambda w: w.astype(BF16)

    x1 = _ffn(x.reshape(T, D), p["ffn1_norm"], bf(p["ffn1_w_gate"]), bf(p["ffn1_w_up"]),
              bf(p["ffn1_w_down"]), final_norm, final=False)

    w = p["w_in"]
    widths = (G * Hg * dh,) + (G * dh,) * 6 + (3 * G * Hg, H * dk, H * dk, H * dv, H * dv, 2 * D)
    offs = np.concatenate([[0], np.cumsum(widths)])
    seg = lambda i: w[:, int(offs[i]):int(offs[i + 1])]
    (q_a, kc_a, vc_a, ks_a, vs_a, kw_a, vw_a, g_a, q_r, k_r, v_r, g_r, g_m) = [seg(i) for i in range(13)]
    gate_pad = MXU_N - g_a.shape[1]
    wa = bf(jnp.concatenate([q_a * (dh ** -0.5 * LOG2E), ks_a, kw_a], axis=1))
    wb = bf(jnp.concatenate([q_r * dk ** -0.5, k_r], axis=1))
    wc = bf(jnp.concatenate([v_r, kc_a, vc_a, vs_a, vw_a], axis=1))
    wd = bf(jnp.concatenate([g_r, g_m, g_a, jnp.zeros((D, gate_pad), F32)], axis=1))
    pos = jnp.arange(S)
    tab_a = _rope_tables(pos, ROPE_DIM, dh, ROPE_THETA)
    tab_b = _rope_tables(pos, dk, dk, RET_ROPE_THETA)
    qk_n, qk_r, v_ret, kc_tok, vc_tok, vsT, vwT, gates = _proj(
        x1, p["mix_norm"], wa, wb, wc, wd, tab_a, tab_b, B, S)

    chunk = lambda t: t.reshape(B, S // CMP_STRIDE, CMP_STRIDE * G * dh)
    kw1, kw2, pe = _compress_weights(p["cmp_k_w1"], p["cmp_k_w2"], p["cmp_pos_emb"])
    vw1, vw2, _ = _compress_weights(p["cmp_v_w1"], p["cmp_v_w2"], p["cmp_pos_emb"])
    cmp_end = jnp.arange(S // CMP_STRIDE) * CMP_STRIDE + (CMP_BLOCK - 1)
    kc, vcT = _compress(chunk(kc_tok), chunk(vc_tok), pe, kw1, kw2, vw1, vw2,
                        _rope_tables(cmp_end, ROPE_DIM, dh, ROPE_THETA))
    c2sT, et = _nsa_constants(S)
    a_out = _nsa(qk_n, gates, kc, vcT, vsT, vwT, c2sT, et, B, S)
    r_out = _retention(qk_r, v_ret, p["ret_gn_gain"], B, S).reshape(T, H * dv)

    x2 = _merge(x1, a_out, r_out, gates, bf(p["w_branch_nsa"]), bf(p["w_branch_ret"]), bf(p["w_out"]))
    x3 = _ffn(x2, p["ffn2_norm"], bf(p["ffn2_w_gate"]), bf(p["ffn2_w_up"]), bf(p["ffn2_w_down"]),
              final_norm, final=True)
    return x3.reshape(B, S, D)


def kernel(x, ffn1_norm, ffn1_w_gate, ffn1_w_up, ffn1_w_down, mix_norm, w_in, cmp_pos_emb, cmp_k_w1,
           cmp_k_w2, cmp_v_w1, cmp_v_w2, ret_gn_gain, w_branch_nsa, w_branch_ret, w_out, ffn2_norm,
           ffn2_w_gate, ffn2_w_up, ffn2_w_down, final_norm):
    assert ffn1_norm.shape[0] == 1, "single-layer stack"
    names = ("ffn1_norm", "ffn1_w_gate", "ffn1_w_up", "ffn1_w_down", "mix_norm", "w_in", "cmp_pos_emb",
             "cmp_k_w1", "cmp_k_w2", "cmp_v_w1", "cmp_v_w2", "ret_gn_gain", "w_branch_nsa",
             "w_branch_ret", "w_out", "ffn2_norm", "ffn2_w_gate", "ffn2_w_up", "ffn2_w_down")
    vals = (ffn1_norm, ffn1_w_gate, ffn1_w_up, ffn1_w_down, mix_norm, w_in, cmp_pos_emb, cmp_k_w1,
            cmp_k_w2, cmp_v_w1, cmp_v_w2, ret_gn_gain, w_branch_nsa, w_branch_ret, w_out, ffn2_norm,
            ffn2_w_gate, ffn2_w_up, ffn2_w_down)
    p = {n: v[0] for n, v in zip(names, vals)}
    return _layer(x, p, final_norm)
```

```python
import functools
import math

import numpy as np
import jax
import jax.numpy as jnp
from jax import lax
from jax.experimental import pallas as pl
from jax.experimental.pallas import tpu as pltpu

F32 = jnp.float32
BF16 = jnp.bfloat16

NSA_HEADS = 16
NSA_GROUPS = 2
NSA_HG = NSA_HEADS // NSA_GROUPS
NSA_DH = 64
CMP_BLOCK = 32
CMP_STRIDE = 16
SLC_BLOCK = 64
SLC_TOPN = 16
WINDOW = 512
Q_BLOCK = 128
ROPE_THETA = 500000.0
ROPE_DIM = NSA_DH // 4
FORCED_SCORE = 1.0e4
RET_HEADS = 8
RET_DK = 64
RET_DV = 128
RET_CHUNK = 128
RET_ROPE_THETA = 10000.0
EPS = 1e-6
GN_EPS = 1e-5
NEG_INF = -1e30
LOG2E = math.log2(math.e)

LANES = 128
BF16_ROWS = 16
MXU_N = 256
VMEM_LIMIT = 56 * 1024 * 1024

NT_DIMS = (((1,), (1,)), ((), ()))


def _rms(x, g, eps=EPS):
    return x * lax.rsqrt(jnp.mean(x * x, axis=-1, keepdims=True) + eps) * g


def _resident(shape):
    nd = len(shape)
    return pl.BlockSpec(shape, lambda *_: (0,) * nd, pipeline_mode=pl.Buffered(1))


def _ffn_body(x_ref, g_ref, wg_ref, wu_ref, wd_ref, fg_ref, o_ref, *, tf, final):
    x = x_ref[...]
    h = _rms(x, g_ref[...]).astype(BF16)
    acc = jnp.zeros(x.shape, F32)
    for c in range(wg_ref.shape[1] // tf):
        sl = slice(c * tf, (c + 1) * tf)
        g = jnp.dot(h, wg_ref[:, sl], preferred_element_type=F32)
        u = jnp.dot(h, wu_ref[:, sl], preferred_element_type=F32)
        a = (g * jax.nn.sigmoid(g) * u).astype(BF16)
        acc = acc + jnp.dot(a, wd_ref[sl, :], preferred_element_type=F32)
    y = x + 0.5 * acc
    if final:
        y = _rms(y, fg_ref[...])
    o_ref[...] = y


def _ffn(x2d, gain, wg, wu, wd, final_gain, *, final, tm=512):
    T, D = x2d.shape
    F = wg.shape[1]
    tf = F // 2 if (F // 2) % LANES == 0 else F
    tok = pl.BlockSpec((tm, D), lambda i: (i, 0))
    return pl.pallas_call(
        functools.partial(_ffn_body, tf=tf, final=final),
        grid=(T // tm,),
        in_specs=[tok, _resident((1, D)), _resident((D, F)), _resident((D, F)), _resident((F, D)),
                  _resident((1, D))],
        out_specs=tok,
        out_shape=jax.ShapeDtypeStruct((T, D), F32),
        compiler_params=pltpu.CompilerParams(dimension_semantics=("parallel",),
                                             vmem_limit_bytes=VMEM_LIMIT),
        name="ffn_final" if final else "ffn",
    )(x2d, gain.reshape(1, D), wg, wu, wd, final_gain.reshape(1, D))


def _rope_slab(y, tab_ref, shift):
    return (y * tab_ref[0] + pltpu.roll(y, LANES - shift, 1) * tab_ref[1]
            + pltpu.roll(y, shift, 1) * tab_ref[2])


def _proj_body(x_ref, g_ref, wa_ref, wb_ref, wc_ref, wd_ref, ta_ref, tb_ref,
               oa_ref, ob_ref, ov_ref, okc_ref, ovc_ref, ovsT_ref, ovwT_ref, od_ref):
    h = _rms(x_ref[...], g_ref[...]).astype(BF16)
    tm = h.shape[0]
    nv = ov_ref.shape[1] // LANES

    def slabs(w_ref):
        for c in range(w_ref.shape[1] // MXU_N):
            y = jnp.dot(h, w_ref[:, c * MXU_N:(c + 1) * MXU_N], preferred_element_type=F32)
            for s in range(MXU_N // LANES):
                yield c * (MXU_N // LANES) + s, y[:, s * LANES:(s + 1) * LANES]

    def put(o_ref, i, ys):
        o_ref[:, i * LANES:(i + 1) * LANES] = ys.astype(o_ref.dtype)

    def put_t(o_ref, ys):
        for r in range(tm // LANES):
            o_ref[0, r] = ys[r * LANES:(r + 1) * LANES, :].T.astype(o_ref.dtype)

    for i, ys in slabs(wa_ref):
        put(oa_ref, i, _rope_slab(ys, ta_ref, ROPE_DIM // 2))
    for i, ys in slabs(wb_ref):
        put(ob_ref, i, _rope_slab(ys, tb_ref, RET_DK // 2))
    for i, ys in slabs(wc_ref):
        if i < nv:
            put(ov_ref, i, ys)
        elif i == nv:
            put(okc_ref, 0, ys)
        elif i == nv + 1:
            put(ovc_ref, 0, ys)
        elif i == nv + 2:
            put_t(ovsT_ref, ys)
        else:
            put_t(ovwT_ref, ys)
    for i, ys in slabs(wd_ref):
        put(od_ref, i, ys)


def _proj(x2d, gain, wa, wb, wc, wd, tab_a, tab_b, B, S, *, tm=512):
    T, D = x2d.shape
    spt = S // tm
    nv = wc.shape[1] - 4 * LANES
    tok = lambda n: pl.BlockSpec((tm, n), lambda i: (i, 0))
    tab = pl.BlockSpec((3, tm, LANES), lambda i: (0, i % spt, 0))
    tr = pl.BlockSpec((1, tm // LANES, LANES, LANES), lambda i: (i // spt, i % spt, 0, 0))
    return pl.pallas_call(
        _proj_body,
        grid=(T // tm,),
        in_specs=[tok(D), _resident((1, D)), _resident(wa.shape), _resident(wb.shape),
                  _resident(wc.shape), _resident(wd.shape), tab, tab],
        out_specs=[tok(wa.shape[1]), tok(wb.shape[1]), tok(nv), tok(LANES), tok(LANES), tr, tr,
                   tok(wd.shape[1])],
        out_shape=[jax.ShapeDtypeStruct((T, wa.shape[1]), BF16),
                   jax.ShapeDtypeStruct((T, wb.shape[1]), BF16),
                   jax.ShapeDtypeStruct((T, nv), BF16),
                   jax.ShapeDtypeStruct((T, LANES), BF16),
                   jax.ShapeDtypeStruct((T, LANES), BF16),
                   jax.ShapeDtypeStruct((B, S // LANES, LANES, LANES), BF16),
                   jax.ShapeDtypeStruct((B, S // LANES, LANES, LANES), BF16),
                   jax.ShapeDtypeStruct((T, wd.shape[1]), F32)],
        compiler_params=pltpu.CompilerParams(dimension_semantics=("parallel",),
                                             vmem_limit_bytes=VMEM_LIMIT),
        name="proj",
    )(x2d, gain.reshape(1, D), wa, wb, wc, wd, tab_a, tab_b)


def _rope_tables(pos, rot_dim, head_dim, theta):
    half = rot_dim // 2
    freqs = theta ** (-(jnp.arange(half, dtype=F32) * 2.0 / rot_dim))
    ang = pos.astype(F32)[:, None] * freqs[None, :]
    cos, sin = jnp.cos(ang), jnp.sin(ang)
    n = pos.shape[0]
    rest = head_dim - rot_dim
    cos_h = jnp.concatenate([cos, cos, jnp.ones((n, rest), F32)], axis=-1)
    sl_h = jnp.concatenate([-sin, jnp.zeros((n, half + rest), F32)], axis=-1)
    sr_h = jnp.concatenate([jnp.zeros((n, half), F32), sin, jnp.zeros((n, rest), F32)], axis=-1)
    rep = LANES // head_dim
    return jnp.stack([jnp.tile(t, (1, rep)) for t in (cos_h, sl_h, sr_h)])


def _gelu_tanh(x):
    return 0.5 * x * (1.0 + jnp.tanh(math.sqrt(2.0 / math.pi) * (x + 0.044715 * (x * x * x))))


def _compress_body(kt_ref, vt_ref, pe_ref, kw1_ref, kw2_ref, vw1_ref, vw2_ref, tab_ref,
                   kc_ref, vcT_ref):
    pe = pe_ref[...]
    n = kt_ref.shape[1]

    def mlp(tok_ref, w1_ref, w2_ref):
        c = tok_ref[0].astype(F32)
        lo = (c + pe[0:1]).astype(BF16)
        hi = (c + pe[1:2]).astype(BF16)
        out = jnp.zeros((n, LANES), F32)
        for g in range(NSA_GROUPS):
            top = jnp.dot(lo, w1_ref[g, 0], preferred_element_type=F32)
            bot = jnp.dot(hi, w1_ref[g, 1], preferred_element_type=F32)
            hid = _gelu_tanh(top + pltpu.roll(bot, n - 1, 0))
            out = out + jnp.dot(hid.astype(BF16), w2_ref[g], preferred_element_type=F32)
        return out

    kc_ref[0] = _rope_slab(mlp(kt_ref, kw1_ref, kw2_ref), tab_ref, ROPE_DIM // 2).astype(kc_ref.dtype)
    vc = mlp(vt_ref, vw1_ref, vw2_ref)
    for r in range(n // LANES):
        vcT_ref[0, :, r * LANES:(r + 1) * LANES] = vc[r * LANES:(r + 1) * LANES, :].T.astype(vcT_ref.dtype)


def _compress(k_tok, v_tok, pe, kw1, kw2, vw1, vw2, tab):
    B, n, w = k_tok.shape
    tok = pl.BlockSpec((1, n, w), lambda b: (b, 0, 0))
    return pl.pallas_call(
        _compress_body,
        grid=(B,),
        in_specs=[tok, tok, _resident(pe.shape), _resident(kw1.shape), _resident(kw2.shape),
                  _resident(vw1.shape), _resident(vw2.shape), _resident(tab.shape)],
        out_specs=[pl.BlockSpec((1, n, LANES), lambda b: (b, 0, 0)),
                   pl.BlockSpec((1, LANES, n), lambda b: (b, 0, 0))],
        out_shape=[jax.ShapeDtypeStruct((B, n, LANES), BF16), jax.ShapeDtypeStruct((B, LANES, n), BF16)],
        compiler_params=pltpu.CompilerParams(dimension_semantics=("parallel",),
                                             vmem_limit_bytes=VMEM_LIMIT),
        name="compress",
    )(k_tok, v_tok, pe, kw1, kw2, vw1, vw2, tab)


def _compress_weights(w1, w2, pe):
    G, dh = NSA_GROUPS, NSA_DH
    hid = w1.shape[1]
    halves = w1.reshape(2, CMP_STRIDE, dh, hid)
    w1e = jnp.zeros((G, 2, CMP_STRIDE, G, dh, hid), F32)
    w2e = jnp.zeros((G, hid, G, dh), F32)
    for g in range(G):
        w1e = w1e.at[g, :, :, g].set(halves)
        w2e = w2e.at[g, :, g].set(w2)
    pe_e = jnp.broadcast_to(pe.reshape(2, CMP_STRIDE, 1, dh), (2, CMP_STRIDE, G, dh))
    return (w1e.reshape(G, 2, CMP_STRIDE * G * dh, hid).astype(BF16), w2e.reshape(G, hid, G * dh).astype(BF16),
            pe_e.reshape(2, CMP_STRIDE * G * dh))


def _nsa_body(q_ref, gate_ref, kc_ref, vcT_ref, ks_ref, vsT_ref, kw_ref, vwT_ref, c2sT_ref, et_ref,
              o_ref, qT_scr, gT_scr, m_scr, acc_scr, rank_scr, *, kt, top_n, n_slc, hg, dh):
    qb = q_ref.shape[0]
    cols = hg * qb
    ncp = kc_ref.shape[1]
    g = pl.program_id(1)
    c = pl.program_id(2)
    q0 = c * qb
    goff = pl.multiple_of(g * dh, dh)
    ones_rows = jnp.ones((BF16_ROWS, 1), BF16)

    def v_tiles(vT_ref, k0, n):
        t = vT_ref[0, pl.ds(k0 // LANES, n // LANES), pl.ds(goff, dh), :]
        vT = jnp.concatenate([t[i] for i in range(n // LANES)], axis=1)
        return jnp.concatenate([vT, jnp.broadcast_to(ones_rows, (BF16_ROWS, n))], axis=0)

    def tq(n):
        return q0 + lax.broadcasted_iota(jnp.int32, (n, qb), 1)

    def krow(n, k0=0):
        return k0 + lax.broadcasted_iota(jnp.int32, (n, qb), 0)

    qT_scr[...] = jnp.zeros(qT_scr.shape, BF16)
    for r in range(qb // LANES):
        rows = slice(r * LANES, (r + 1) * LANES)
        for j in range(hg // 2):
            t = q_ref[rows, j * LANES:(j + 1) * LANES].astype(F32).T.astype(BF16)
            qT_scr[pl.ds(goff, dh), (2 * j) * qb + r * LANES:(2 * j) * qb + (r + 1) * LANES] = t[:dh]
            qT_scr[pl.ds(goff, dh), (2 * j + 1) * qb + r * LANES:(2 * j + 1) * qb + (r + 1) * LANES] = t[dh:]
        gT_scr[:, rows] = jax.nn.sigmoid(gate_ref[rows, :LANES].T)

    qT = qT_scr[:LANES, :]
    s_c = jnp.dot(kc_ref[0], qT, preferred_element_type=F32)
    cend = krow(ncp) * CMP_STRIDE + (CMP_BLOCK - 1)
    bias_c = jnp.where(cend <= tq(ncp), 0.0, NEG_INF)
    bias_c = jnp.concatenate([bias_c, bias_c], axis=1)
    nb = dh
    lhs_c = jnp.concatenate([vcT_ref[0, pl.ds(goff, dh), :], jnp.broadcast_to(ones_rows, (BF16_ROWS, ncp)),
                             c2sT_ref[:nb, :]], axis=0)
    imp = jnp.zeros((nb, qb), F32)
    o_c = []
    for hp in range(hg // 2):
        sh = s_c[:, hp * 2 * qb:(hp + 1) * 2 * qb] + bias_c
        e = jnp.exp2(sh - jnp.max(sh, axis=0, keepdims=True)).astype(BF16)
        r = jnp.dot(lhs_c, e, preferred_element_type=F32)
        inv_l = 1.0 / r[dh:dh + 1]
        o_c.append(r[:dh] * inv_l)
        w = r[dh + BF16_ROWS:] * inv_l
        imp = imp + w[:, :qb] + w[:, qb:]
    o_c = jnp.concatenate(o_c, axis=1)
    sb = krow(nb)
    cur = jnp.right_shift(tq(nb), int(math.log2(SLC_BLOCK)))
    forced = (sb == 0) | (sb == cur) | (sb == cur - 1)
    future = sb > cur
    imp = jnp.where(forced, FORCED_SCORE, jnp.where(future, -FORCED_SCORE, imp))
    sub = 8
    rank_scr[...] = jnp.zeros(rank_scr.shape, jnp.int32)
    last_block = (q0 + qb - 1) // SLC_BLOCK
    sb_l = sb[:, :LANES]
    for grp in range(n_slc // sub):
        @pl.when(grp * sub <= last_block)
        def _():
            for lc in range(qb // LANES):
                imp_l = imp[:, lc * LANES:(lc + 1) * LANES]
                chunks = [imp_l[r * sub:(r + 1) * sub] for r in range(nb // sub)]
                ranks = [jnp.zeros((sub, LANES), jnp.int32) for _ in chunks]
                for sp in range(grp * sub, (grp + 1) * sub):
                    row = imp_l[sp:sp + 1, :]
                    for r, blk in enumerate(chunks):
                        if r * sub > sp:
                            beats = row >= blk
                        elif (r + 1) * sub - 1 <= sp:
                            beats = row > blk
                        else:
                            beats = (row > blk) | ((row == blk) & (sb_l[r * sub:(r + 1) * sub] > sp))
                        ranks[r] = ranks[r] + jnp.where(beats, 1, 0)
                rank_scr[:, lc * LANES:(lc + 1) * LANES] += jnp.concatenate(ranks, axis=0)
    selneg = jnp.where((rank_scr[...] < top_n) & jnp.logical_not(future), 0.0, NEG_INF).astype(BF16)
    for h in range(hg):
        qT_scr[LANES:LANES + nb, h * qb:(h + 1) * qb] = selneg

    qT_aug = qT_scr[...]

    def scores(j):
        k0 = pl.multiple_of(j * kt, kt)
        ka = jnp.concatenate([ks_ref[0, pl.ds(k0, kt), :], et_ref[pl.ds(k0, kt), :]], axis=1)
        return jnp.dot(ka, qT_aug, preferred_element_type=F32)

    def fold(s, j, causal):
        vT = v_tiles(vsT_ref, pl.multiple_of(j * kt, kt), kt)
        if causal:
            bias = jnp.where(krow(kt, j * kt) <= tq(kt), 0.0, NEG_INF)
            bias = jnp.concatenate([bias, bias], axis=1)
        for hp in range(hg // 2):
            sl = slice(hp * 2 * qb, (hp + 1) * 2 * qb)
            sh = s[:, sl] + bias if causal else s[:, sl]
            m_old = m_scr[:, sl]
            m_new = jnp.maximum(m_old, jnp.max(sh, axis=0, keepdims=True))
            pv = jnp.dot(vT, jnp.exp2(sh - m_new).astype(BF16), preferred_element_type=F32)
            acc_scr[:, sl] = jnp.exp2(m_old - m_new) * acc_scr[:, sl] + pv
            m_scr[:, sl] = m_new

    m_scr[...] = jnp.full(m_scr.shape, NEG_INF, F32)
    acc_scr[...] = jnp.zeros(acc_scr.shape, F32)

    def main_pair(i, carry):
        s0, s1 = scores(2 * i), scores(2 * i + 1)
        fold(s0, 2 * i, False)
        fold(s1, 2 * i + 1, False)
        return carry

    lax.fori_loop(0, c // 2, main_pair, 0)

    @pl.when(c % 2 == 1)
    def _():
        fold(scores(c - 1), c - 1, False)

    fold(scores(c), c, True)
    o_s = acc_scr[:dh, :] * (1.0 / acc_scr[dh:dh + 1, :])

    span = WINDOW + qb
    w0 = pl.multiple_of(jnp.maximum(q0 - WINDOW, 0), qb)
    dist = tq(span) - krow(span, w0)
    bias_w = jnp.where((dist >= 0) & (dist < WINDOW), 0.0, NEG_INF)
    bias_w = jnp.concatenate([bias_w, bias_w], axis=1)
    s_w = jnp.dot(kw_ref[0, pl.ds(w0, span), :], qT, preferred_element_type=F32)
    vwT = v_tiles(vwT_ref, w0, span)
    o_w = []
    for hp in range(hg // 2):
        sh = s_w[:, hp * 2 * qb:(hp + 1) * 2 * qb] + bias_w
        p = jnp.exp2(sh - jnp.max(sh, axis=0, keepdims=True)).astype(BF16)
        acc_w = jnp.dot(vwT, p, preferred_element_type=F32)
        o_w.append(acc_w[:dh] * (1.0 / acc_w[dh:dh + 1]))
    o_w = jnp.concatenate(o_w, axis=1)

    has_cmp = (tq(1) >= CMP_BLOCK - 1).astype(F32)
    gts = gT_scr[pl.ds(pl.multiple_of(g * (3 * hg), 8), 3 * hg), :]
    outs = []
    for h in range(hg):
        sl = slice(h * qb, (h + 1) * qb)
        outs.append(gts[3 * h:3 * h + 1] * has_cmp * o_c[:, sl] + gts[3 * h + 1:3 * h + 2] * o_s[:, sl]
                    + gts[3 * h + 2:3 * h + 3] * o_w[:, sl])
    for j in range(hg // 2):
        pair = jnp.concatenate([outs[2 * j], outs[2 * j + 1]], axis=0)
        for r in range(qb // LANES):
            rows = slice(r * LANES, (r + 1) * LANES)
            o_ref[rows, j * LANES:(j + 1) * LANES] = pair[:, rows].T.astype(o_ref.dtype)


def _nsa(qk, gates, kc, vcT, vsT, vwT, c2sT, et, B, S, *, qb=256):
    G, Hg, dh = NSA_GROUPS, NSA_HG, NSA_DH
    T = B * S
    nq = S // qb
    n_slc = S // SLC_BLOCK
    kt = qb
    assert n_slc <= dh and 2 * dh == LANES and qb % LANES == 0 and S % qb == 0 and WINDOW % qb == 0
    assert S >= WINDOW + qb
    qw = Hg * dh
    ncp = kc.shape[1]
    k_col = G * qw // LANES
    g_col = gates.shape[1] // MXU_N - 1
    seq_k = lambda col: pl.BlockSpec((1, S, LANES), lambda b, g, c: (b, 0, col))
    seq_vT = pl.BlockSpec((1, S // LANES, LANES, LANES), lambda b, g, c: (b, 0, 0, 0))
    qk3 = qk.reshape(B, S, qk.shape[1])
    return pl.pallas_call(
        functools.partial(_nsa_body, kt=kt, top_n=min(SLC_TOPN, n_slc), n_slc=n_slc, hg=Hg, dh=dh),
        grid=(B, G, nq),
        in_specs=[pl.BlockSpec((qb, qw), lambda b, g, c: (b * nq + c, g)),
                  pl.BlockSpec((qb, MXU_N), lambda b, g, c: (b * nq + c, g_col)),
                  pl.BlockSpec((1, ncp, LANES), lambda b, g, c: (b, 0, 0)),
                  pl.BlockSpec((1, LANES, ncp), lambda b, g, c: (b, 0, 0)),
                  seq_k(k_col), seq_vT, seq_k(k_col + 1), seq_vT,
                  _resident(c2sT.shape), _resident(et.shape)],
        out_specs=pl.BlockSpec((qb, qw), lambda b, g, c: (b * nq + c, g)),
        out_shape=jax.ShapeDtypeStruct((T, G * qw), BF16),
        scratch_shapes=[pltpu.VMEM((2 * LANES, Hg * qb), BF16), pltpu.VMEM((LANES, qb), F32),
                        pltpu.VMEM((1, Hg * qb), F32), pltpu.VMEM((dh + BF16_ROWS, Hg * qb), F32),
                        pltpu.VMEM((dh, qb), jnp.int32)],
        compiler_params=pltpu.CompilerParams(dimension_semantics=("parallel", "arbitrary", "arbitrary"),
                                             vmem_limit_bytes=VMEM_LIMIT),
        name="nsa",
    )(qk, gates, kc, vcT, qk3, vsT, qk3, vwT, c2sT, et)


def _nsa_constants(S):
    ncp = S // CMP_STRIDE
    n_slc = S // SLC_BLOCK
    cmp_start = np.arange(ncp) * CMP_STRIDE
    s_start = np.arange(n_slc) * SLC_BLOCK
    overlap = np.clip(np.minimum(cmp_start[:, None] + CMP_BLOCK, s_start[None, :] + SLC_BLOCK)
                      - np.maximum(cmp_start[:, None], s_start[None, :]), 0, None)
    c2sT = np.zeros((LANES, ncp), np.float32)
    c2sT[:n_slc, :] = overlap.T.astype(np.float32) / CMP_BLOCK
    c2sT[:, ncp - 1] = 0.0
    et = np.zeros((S, LANES), np.float32)
    et[np.arange(S), np.arange(S) // SLC_BLOCK] = 1.0
    return jnp.asarray(c2sT, BF16), jnp.asarray(et, BF16)


def _ret_body(q_ref, k_ref, v_ref, decay_ref, zeta_ref, xi_ref, gch_ref, gn_ref, o_ref, *, C, dk):
    S = q_ref.shape[1]
    dv = gn_ref.shape[2]
    heads = LANES // dk
    lane_head = lax.broadcasted_iota(jnp.int32, (C, LANES), 1) // dk
    N = S // C
    units = [(n, a) for n in range(N) for a in range(heads)]
    rows = lambda n: slice(n * C, (n + 1) * C)
    val = lambda n, a: v_ref[0, rows(n), a * dv:(a + 1) * dv]

    qa = {(n, a): jnp.where(lane_head == a, q_ref[0, rows(n), :].astype(F32), 0.0) for n, a in units}
    inner = {u: lax.dot_general(qa[u].astype(BF16), k_ref[0, rows(u[0]), :], NT_DIMS,
                                preferred_element_type=F32) for u in units}
    inner = {u: (inner[u] * decay_ref[u[1]]).astype(BF16) for u in units}
    o = {u: jnp.dot(inner[u], val(*u), preferred_element_type=F32) for u in units}
    kT = [k_ref[0, rows(n), :].astype(F32).T for n in range(N)]
    kv = {(n, a): jnp.dot((kT[n] * zeta_ref[a]).astype(BF16), val(n, a), preferred_element_type=F32)
          for n, a in units}
    state = {}
    for a in range(heads):
        R = jnp.zeros((LANES, dv), F32)
        for n in range(N):
            state[(n, a)] = R.astype(BF16)
            R = gch_ref[a] * R + kv[(n, a)]
    o = {u: o[u] + jnp.dot((qa[u] * xi_ref[u[1]]).astype(BF16), state[u], preferred_element_type=F32)
         for u in units}
    mu = {u: jnp.mean(o[u], axis=-1, keepdims=True) for u in units}
    d = {u: o[u] - mu[u] for u in units}
    var = {u: jnp.mean(d[u] * d[u], axis=-1, keepdims=True) for u in units}
    for n, a in units:
        o_ref[0, rows(n), a * dv:(a + 1) * dv] = d[(n, a)] * lax.rsqrt(var[(n, a)] + GN_EPS) * gn_ref[a]


def _retention(qk, v, gn_gain, B, S, *, C=RET_CHUNK):
    H, dk, dv = RET_HEADS, RET_DK, RET_DV
    hp = LANES // dk
    log_g = jnp.log(1.0 - 2.0 ** (-5.0 - jnp.arange(H, dtype=F32)))
    i = jnp.arange(C, dtype=F32)
    diff = i[:, None] - i[None, :]
    decay = jnp.where(diff >= 0, jnp.exp(jnp.maximum(diff, 0.0) * log_g[:, None, None]), 0.0)
    zeta = jnp.exp((C - 1.0 - i)[None, :] * log_g[:, None]).reshape(H, 1, C)
    xi = jnp.broadcast_to(jnp.exp((i + 1.0)[None, :] * log_g[:, None])[:, :, None], (H, C, LANES))
    gch = jnp.broadcast_to(jnp.exp(C * log_g)[:, None, None], (H, 1, dv))
    per_pair = lambda shape: pl.BlockSpec((hp,) + shape, lambda b, p: (p, 0, 0))
    k_col = H * dk // LANES
    qk3 = qk.reshape(B, S, qk.shape[1])
    return pl.pallas_call(
        functools.partial(_ret_body, C=C, dk=dk),
        grid=(B, H // hp),
        in_specs=[pl.BlockSpec((1, S, LANES), lambda b, p: (b, 0, p)),
                  pl.BlockSpec((1, S, LANES), lambda b, p: (b, 0, k_col + p)),
                  pl.BlockSpec((1, S, hp * dv), lambda b, p: (b, 0, p)),
                  per_pair((C, C)), per_pair((1, C)), per_pair((C, LANES)), per_pair((1, dv)),
                  per_pair((1, dv))],
        out_specs=pl.BlockSpec((1, S, hp * dv), lambda b, p: (b, 0, p)),
        out_shape=jax.ShapeDtypeStruct((B, S, H * dv), F32),
        compiler_params=pltpu.CompilerParams(dimension_semantics=("parallel", "parallel"),
                                             vmem_limit_bytes=VMEM_LIMIT),
        name="retention",
    )(qk3, qk3, v.reshape(B, S, H * dv), decay, zeta, xi, gch, gn_gain.reshape(H, 1, dv))


def _merge_body(x_ref, a_ref, r_ref, gr_ref, ga_ref, gb_ref, wa_ref, wr_ref, wo_ref, o_ref):
    g = gr_ref[...]
    r = (g * jax.nn.sigmoid(g) * r_ref[...]).astype(BF16)
    ya = jnp.dot(a_ref[...], wa_ref[...], preferred_element_type=F32)
    yr = jnp.dot(r, wr_ref[...], preferred_element_type=F32)
    mixed = jax.nn.sigmoid(ga_ref[...]) * ya + jax.nn.sigmoid(gb_ref[...]) * yr
    o_ref[...] = x_ref[...] + jnp.dot(mixed.astype(BF16), wo_ref[...], preferred_element_type=F32)


def _merge(x2d, a, r, gates, wa, wr, wo, *, tm=512):
    T, D = x2d.shape
    col = lambda j: pl.BlockSpec((tm, D), lambda i: (i, j))
    return pl.pallas_call(
        _merge_body,
        grid=(T // tm,),
        in_specs=[col(0), col(0), col(0), col(0), col(1), col(2),
                  _resident(wa.shape), _resident(wr.shape), _resident(wo.shape)],
        out_specs=col(0),
        out_shape=jax.ShapeDtypeStruct((T, D), F32),
        compiler_params=pltpu.CompilerParams(dimension_semantics=("parallel",),
                                             vmem_limit_bytes=VMEM_LIMIT),
        name="merge",
    )(x2d, a, r, gates, gates, gates, wa, wr, wo)


def _layer(x, p, final_norm):
    B, S, D = x.shape
    T = B * S
    G, Hg, dh = NSA_GROUPS, NSA_HG, NSA_DH
    H, dk, dv = RET_HEADS, RET_DK, RET_DV
    bf = lambda w: w.astype(BF16)

    x1 = _ffn(x.reshape(T, D), p["ffn1_norm"], bf(p["ffn1_w_gate"]), bf(p["ffn1_w_up"]),
              bf(p["ffn1_w_down"]), final_norm, final=False)

    w = p["w_in"]
    widths = (G * Hg * dh,) + (G * dh,) * 6 + (3 * G * Hg, H * dk, H * dk, H * dv, H * dv, 2 * D)
    offs = np.concatenate([[0], np.cumsum(widths)])
    seg = lambda i: w[:, int(offs[i]):int(offs[i + 1])]
    (q_a, kc_a, vc_a, ks_a, vs_a, kw_a, vw_a, g_a, q_r, k_r, v_r, g_r, g_m) = [seg(i) for i in range(13)]
    gate_pad = MXU_N - g_a.shape[1]
    wa = bf(jnp.concatenate([q_a * (dh ** -0.5 * LOG2E), ks_a, kw_a], axis=1))
    wb = bf(jnp.concatenate([q_r * dk ** -0.5, k_r], axis=1))
    wc = bf(jnp.concatenate([v_r, kc_a, vc_a, vs_a, vw_a], axis=1))
    wd = bf(jnp.concatenate([g_r, g_m, g_a, jnp.zeros((D, gate_pad), F32)], axis=1))
    pos = jnp.arange(S)
    tab_a = _rope_tables(pos, ROPE_DIM, dh, ROPE_THETA)
    tab_b = _rope_tables(pos, dk, dk, RET_ROPE_THETA)
    qk_n, qk_r, v_ret, kc_tok, vc_tok, vsT, vwT, gates = _proj(
        x1, p["mix_norm"], wa, wb, wc, wd, tab_a, tab_b, B, S)

    chunk = lambda t: t.reshape(B, S // CMP_STRIDE, CMP_STRIDE * G * dh)
    kw1, kw2, pe = _compress_weights(p["cmp_k_w1"], p["cmp_k_w2"], p["cmp_pos_emb"])
    vw1, vw2, _ = _compress_weights(p["cmp_v_w1"], p["cmp_v_w2"], p["cmp_pos_emb"])
    cmp_end = jnp.arange(S // CMP_STRIDE) * CMP_STRIDE + (CMP_BLOCK - 1)
    kc, vcT = _compress(chunk(kc_tok), chunk(vc_tok), pe, kw1, kw2, vw1, vw2,
                        _rope_tables(cmp_end, ROPE_DIM, dh, ROPE_THETA))
    c2sT, et = _nsa_constants(S)
    a_out = _nsa(qk_n, gates, kc, vcT, vsT, vwT, c2sT, et, B, S)
    r_out = _retention(qk_r, v_ret, p["ret_gn_gain"], B, S).reshape(T, H * dv)

    x2 = _merge(x1, a_out, r_out, gates, bf(p["w_branch_nsa"]), bf(p["w_branch_ret"]), bf(p["w_out"]))
    x3 = _ffn(x2, p["ffn2_norm"], bf(p["ffn2_w_gate"]), bf(p["ffn2_w_up"]), bf(p["ffn2_w_down"]),
              final_norm, final=True)
    return x3.reshape(B, S, D)


def kernel(x, ffn1_norm, ffn1_w_gate, ffn1_w_up, ffn1_w_down, mix_norm, w_in, cmp_pos_emb, cmp_k_w1,
           cmp_k_w2, cmp_v_w1, cmp_v_w2, ret_gn_gain, w_branch_nsa, w_branch_ret, w_out, ffn2_norm,
           ffn2_w_gate, ffn2_w_up, ffn2_w_down, final_norm):
    assert ffn1_norm.shape[0] == 1, "single-layer stack"
    names = ("ffn1_norm", "ffn1_w_gate", "ffn1_w_up", "ffn1_w_down", "mix_norm", "w_in", "cmp_pos_emb",
             "cmp_k_w1", "cmp_k_w2", "cmp_v_w1", "cmp_v_w2", "ret_gn_gain", "w_branch_nsa",
             "w_branch_ret", "w_out", "ffn2_norm", "ffn2_w_gate", "ffn2_w_up", "ffn2_w_down")
    vals = (ffn1_norm, ffn1_w_gate, ffn1_w_up, ffn1_w_down, mix_norm, w_in, cmp_pos_emb, cmp_k_w1,
            cmp_k_w2, cmp_v_w1, cmp_v_w2, ret_gn_gain, w_branch_nsa, w_branch_ret, w_out, ffn2_norm,
            ffn2_w_gate, ffn2_w_up, ffn2_w_down)
    p = {n: v[0] for n, v in zip(names, vals)}
    return _layer(x, p, final_norm)
```

```python
import functools
import math

import numpy as np
import jax
import jax.numpy as jnp
from jax import lax
from jax.experimental import pallas as pl
from jax.experimental.pallas import tpu as pltpu

F32 = jnp.float32
BF16 = jnp.bfloat16

NSA_HEADS = 16
NSA_GROUPS = 2
NSA_HG = NSA_HEADS // NSA_GROUPS
NSA_DH = 64
CMP_BLOCK = 32
CMP_STRIDE = 16
SLC_BLOCK = 64
SLC_TOPN = 16
WINDOW = 512
Q_BLOCK = 128
ROPE_THETA = 500000.0
ROPE_DIM = NSA_DH // 4
FORCED_SCORE = 1.0e4
RET_HEADS = 8
RET_DK = 64
RET_DV = 128
RET_CHUNK = 128
RET_ROPE_THETA = 10000.0
EPS = 1e-6
GN_EPS = 1e-5
NEG_INF = -1e30
LOG2E = math.log2(math.e)

LANES = 128
BF16_ROWS = 16
MXU_N = 256
VMEM_LIMIT = 56 * 1024 * 1024

NT_DIMS = (((1,), (1,)), ((), ()))


def _rms(x, g, eps=EPS):
    return x * lax.rsqrt(jnp.mean(x * x, axis=-1, keepdims=True) + eps) * g


def _resident(shape):
    nd = len(shape)
    return pl.BlockSpec(shape, lambda *_: (0,) * nd, pipeline_mode=pl.Buffered(1))


def _ffn_body(x_ref, g_ref, wg_ref, wu_ref, wd_ref, fg_ref, o_ref, *, tf, final):
    x = x_ref[...]
    h = _rms(x, g_ref[...]).astype(BF16)
    acc = jnp.zeros(x.shape, F32)
    for c in range(wg_ref.shape[1] // tf):
        sl = slice(c * tf, (c + 1) * tf)
        g = jnp.dot(h, wg_ref[:, sl], preferred_element_type=F32)
        u = jnp.dot(h, wu_ref[:, sl], preferred_element_type=F32)
        a = (g * jax.nn.sigmoid(g) * u).astype(BF16)
        acc = acc + jnp.dot(a, wd_ref[sl, :], preferred_element_type=F32)
    y = x + 0.5 * acc
    if final:
        y = _rms(y, fg_ref[...])
    o_ref[...] = y


def _ffn(x2d, gain, wg, wu, wd, final_gain, *, final, tm=512):
    T, D = x2d.shape
    F = wg.shape[1]
    tf = F // 2 if (F // 2) % LANES == 0 else F
    tok = pl.BlockSpec((tm, D), lambda i: (i, 0))
    return pl.pallas_call(
        functools.partial(_ffn_body, tf=tf, final=final),
        grid=(T // tm,),
        in_specs=[tok, _resident((1, D)), _resident((D, F)), _resident((D, F)), _resident((F, D)),
                  _resident((1, D))],
        out_specs=tok,
        out_shape=jax.ShapeDtypeStruct((T, D), F32),
        compiler_params=pltpu.CompilerParams(dimension_semantics=("parallel",),
                                             vmem_limit_bytes=VMEM_LIMIT),
        name="ffn_final" if final else "ffn",
    )(x2d, gain.reshape(1, D), wg, wu, wd, final_gain.reshape(1, D))


def _rope_slab(y, tab_ref, shift):
    return (y * tab_ref[0] + pltpu.roll(y, LANES - shift, 1) * tab_ref[1]
            + pltpu.roll(y, shift, 1) * tab_ref[2])


def _proj_body(x_ref, g_ref, wa_ref, wb_ref, wc_ref, wd_ref, ta_ref, tb_ref,
               oa_ref, ob_ref, okrT_ref, ov_ref, okc_ref, ovc_ref, ovsT_ref, ovwT_ref, od_ref):
    h = _rms(x_ref[...], g_ref[...]).astype(BF16)
    tm = h.shape[0]
    nv = ov_ref.shape[1] // LANES
    nkr = okrT_ref.shape[1]

    def slabs(w_ref):
        for c in range(w_ref.shape[1] // MXU_N):
            y = jnp.dot(h, w_ref[:, c * MXU_N:(c + 1) * MXU_N], preferred_element_type=F32)
            for s in range(MXU_N // LANES):
                yield c * (MXU_N // LANES) + s, y[:, s * LANES:(s + 1) * LANES]

    def put(o_ref, i, ys):
        o_ref[:, i * LANES:(i + 1) * LANES] = ys.astype(o_ref.dtype)

    def put_t(o_ref, ys, *lead):
        for r in range(tm // LANES):
            o_ref[(0,) + lead + (r,)] = ys[r * LANES:(r + 1) * LANES, :].T.astype(o_ref.dtype)

    for i, ys in slabs(wa_ref):
        put(oa_ref, i, _rope_slab(ys, ta_ref, ROPE_DIM // 2))
    nb = wb_ref.shape[1] // LANES
    for i, ys in slabs(wb_ref):
        ys = _rope_slab(ys, tb_ref, RET_DK // 2)
        if i < nb - nkr:
            put(ob_ref, i, ys)
        else:
            put_t(okrT_ref, ys, i - (nb - nkr))
    for i, ys in slabs(wc_ref):
        if i < nv:
            put(ov_ref, i, ys)
        elif i == nv:
            put(okc_ref, 0, ys)
        elif i == nv + 1:
            put(ovc_ref, 0, ys)
        elif i == nv + 2:
            put_t(ovsT_ref, ys)
        else:
            put_t(ovwT_ref, ys)
    for i, ys in slabs(wd_ref):
        put(od_ref, i, ys)


def _proj(x2d, gain, wa, wb, wc, wd, tab_a, tab_b, B, S, *, tm=512):
    T, D = x2d.shape
    spt = S // tm
    nv = wc.shape[1] - 4 * LANES
    tok = lambda n: pl.BlockSpec((tm, n), lambda i: (i, 0))
    tab = pl.BlockSpec((3, tm, LANES), lambda i: (0, i % spt, 0))
    tr = pl.BlockSpec((1, tm // LANES, LANES, LANES), lambda i: (i // spt, i % spt, 0, 0))
    nkr = RET_HEADS * RET_DK // LANES
    tr_kr = pl.BlockSpec((1, nkr, tm // LANES, LANES, LANES), lambda i: (i // spt, 0, i % spt, 0, 0))
    return pl.pallas_call(
        _proj_body,
        grid=(T // tm,),
        in_specs=[tok(D), _resident((1, D)), _resident(wa.shape), _resident(wb.shape),
                  _resident(wc.shape), _resident(wd.shape), tab, tab],
        out_specs=[tok(wa.shape[1]), tok(wb.shape[1] - nkr * LANES), tr_kr, tok(nv), tok(LANES), tok(LANES),
                   tr, tr, tok(wd.shape[1])],
        out_shape=[jax.ShapeDtypeStruct((T, wa.shape[1]), BF16),
                   jax.ShapeDtypeStruct((T, wb.shape[1] - nkr * LANES), BF16),
                   jax.ShapeDtypeStruct((B, nkr, S // LANES, LANES, LANES), BF16),
                   jax.ShapeDtypeStruct((T, nv), BF16),
                   jax.ShapeDtypeStruct((T, LANES), BF16),
                   jax.ShapeDtypeStruct((T, LANES), BF16),
                   jax.ShapeDtypeStruct((B, S // LANES, LANES, LANES), BF16),
                   jax.ShapeDtypeStruct((B, S // LANES, LANES, LANES), BF16),
                   jax.ShapeDtypeStruct((T, wd.shape[1]), BF16)],
        compiler_params=pltpu.CompilerParams(dimension_semantics=("parallel",),
                                             vmem_limit_bytes=VMEM_LIMIT),
        name="proj",
    )(x2d, gain.reshape(1, D), wa, wb, wc, wd, tab_a, tab_b)


def _rope_tables(pos, rot_dim, head_dim, theta):
    half = rot_dim // 2
    freqs = theta ** (-(jnp.arange(half, dtype=F32) * 2.0 / rot_dim))
    ang = pos.astype(F32)[:, None] * freqs[None, :]
    cos, sin = jnp.cos(ang), jnp.sin(ang)
    n = pos.shape[0]
    rest = head_dim - rot_dim
    cos_h = jnp.concatenate([cos, cos, jnp.ones((n, rest), F32)], axis=-1)
    sl_h = jnp.concatenate([-sin, jnp.zeros((n, half + rest), F32)], axis=-1)
    sr_h = jnp.concatenate([jnp.zeros((n, half), F32), sin, jnp.zeros((n, rest), F32)], axis=-1)
    rep = LANES // head_dim
    return jnp.stack([jnp.tile(t, (1, rep)) for t in (cos_h, sl_h, sr_h)])


def _gelu_tanh(x):
    return 0.5 * x * (1.0 + jnp.tanh(math.sqrt(2.0 / math.pi) * (x + 0.044715 * (x * x * x))))


def _compress_body(kt_ref, vt_ref, pe_ref, kw1_ref, kw2_ref, vw1_ref, vw2_ref, tab_ref,
                   kc_ref, vcT_ref):
    pe = pe_ref[...]
    n = kt_ref.shape[1]

    def mlp(tok_ref, w1_ref, w2_ref):
        c = tok_ref[0].astype(F32)
        lo = (c + pe[0:1]).astype(BF16)
        hi = (c + pe[1:2]).astype(BF16)
        out = jnp.zeros((n, LANES), F32)
        for g in range(NSA_GROUPS):
            top = jnp.dot(lo, w1_ref[g, 0], preferred_element_type=F32)
            bot = jnp.dot(hi, w1_ref[g, 1], preferred_element_type=F32)
            hid = _gelu_tanh(top + pltpu.roll(bot, n - 1, 0))
            out = out + jnp.dot(hid.astype(BF16), w2_ref[g], preferred_element_type=F32)
        return out

    kc_ref[0] = _rope_slab(mlp(kt_ref, kw1_ref, kw2_ref), tab_ref, ROPE_DIM // 2).astype(kc_ref.dtype)
    vc = mlp(vt_ref, vw1_ref, vw2_ref)
    for r in range(n // LANES):
        vcT_ref[0, :, r * LANES:(r + 1) * LANES] = vc[r * LANES:(r + 1) * LANES, :].T.astype(vcT_ref.dtype)


def _compress(k_tok, v_tok, pe, kw1, kw2, vw1, vw2, tab):
    B, n, w = k_tok.shape
    tok = pl.BlockSpec((1, n, w), lambda b: (b, 0, 0))
    return pl.pallas_call(
        _compress_body,
        grid=(B,),
        in_specs=[tok, tok, _resident(pe.shape), _resident(kw1.shape), _resident(kw2.shape),
                  _resident(vw1.shape), _resident(vw2.shape), _resident(tab.shape)],
        out_specs=[pl.BlockSpec((1, n, LANES), lambda b: (b, 0, 0)),
                   pl.BlockSpec((1, LANES, n), lambda b: (b, 0, 0))],
        out_shape=[jax.ShapeDtypeStruct((B, n, LANES), BF16), jax.ShapeDtypeStruct((B, LANES, n), BF16)],
        compiler_params=pltpu.CompilerParams(dimension_semantics=("parallel",),
                                             vmem_limit_bytes=VMEM_LIMIT),
        name="compress",
    )(k_tok, v_tok, pe, kw1, kw2, vw1, vw2, tab)


def _compress_weights(w1, w2, pe):
    G, dh = NSA_GROUPS, NSA_DH
    hid = w1.shape[1]
    halves = w1.reshape(2, CMP_STRIDE, dh, hid)
    w1e = jnp.zeros((G, 2, CMP_STRIDE, G, dh, hid), F32)
    w2e = jnp.zeros((G, hid, G, dh), F32)
    for g in range(G):
        w1e = w1e.at[g, :, :, g].set(halves)
        w2e = w2e.at[g, :, g].set(w2)
    pe_e = jnp.broadcast_to(pe.reshape(2, CMP_STRIDE, 1, dh), (2, CMP_STRIDE, G, dh))
    return (w1e.reshape(G, 2, CMP_STRIDE * G * dh, hid).astype(BF16), w2e.reshape(G, hid, G * dh).astype(BF16),
            pe_e.reshape(2, CMP_STRIDE * G * dh))


def _nsa_body(q_ref, gate_ref, kc_ref, vcT_ref, ks_ref, vsT_ref, kw_ref, vwT_ref, c2sT_ref, et_ref,
              o_ref, qT_scr, gT_scr, m_scr, acc_scr, rank_scr, *, kt, top_n, n_slc, hg, dh):
    qb = q_ref.shape[0]
    cols = hg * qb
    ncp = kc_ref.shape[1]
    g = pl.program_id(1)
    c = pl.program_id(2)
    q0 = c * qb
    goff = pl.multiple_of(g * dh, dh)
    ones_rows = jnp.ones((BF16_ROWS, 1), BF16)

    def v_tiles(vT_ref, k0, n):
        t = vT_ref[0, pl.ds(k0 // LANES, n // LANES), pl.ds(goff, dh), :]
        vT = jnp.concatenate([t[i] for i in range(n // LANES)], axis=1)
        return jnp.concatenate([vT, jnp.broadcast_to(ones_rows, (BF16_ROWS, n))], axis=0)

    def tq(n):
        return q0 + lax.broadcasted_iota(jnp.int32, (n, qb), 1)

    def krow(n, k0=0):
        return k0 + lax.broadcasted_iota(jnp.int32, (n, qb), 0)

    qT_scr[...] = jnp.zeros(qT_scr.shape, BF16)
    for r in range(qb // LANES):
        rows = slice(r * LANES, (r + 1) * LANES)
        for j in range(hg // 2):
            t = q_ref[rows, j * LANES:(j + 1) * LANES].astype(F32).T.astype(BF16)
            qT_scr[pl.ds(goff, dh), (2 * j) * qb + r * LANES:(2 * j) * qb + (r + 1) * LANES] = t[:dh]
            qT_scr[pl.ds(goff, dh), (2 * j + 1) * qb + r * LANES:(2 * j + 1) * qb + (r + 1) * LANES] = t[dh:]
        gT_scr[:, rows] = jax.nn.sigmoid(gate_ref[rows, :LANES].astype(F32).T)

    qT = qT_scr[:LANES, :]
    s_c = jnp.dot(kc_ref[0], qT, preferred_element_type=F32)
    cend = krow(ncp) * CMP_STRIDE + (CMP_BLOCK - 1)
    bias_c = jnp.where(cend <= tq(ncp), 0.0, NEG_INF)
    bias_c = jnp.concatenate([bias_c, bias_c], axis=1)
    nb = dh
    lhs_c = jnp.concatenate([vcT_ref[0, pl.ds(goff, dh), :], jnp.broadcast_to(ones_rows, (BF16_ROWS, ncp)),
                             c2sT_ref[:nb, :]], axis=0)
    imp = jnp.zeros((nb, qb), F32)
    o_c = []
    for hp in range(hg // 2):
        sh = s_c[:, hp * 2 * qb:(hp + 1) * 2 * qb] + bias_c
        e = jnp.exp2(sh - jnp.max(sh, axis=0, keepdims=True)).astype(BF16)
        r = jnp.dot(lhs_c, e, preferred_element_type=F32)
        inv_l = 1.0 / r[dh:dh + 1]
        o_c.append(r[:dh] * inv_l)
        w = r[dh + BF16_ROWS:] * inv_l
        imp = imp + w[:, :qb] + w[:, qb:]
    o_c = jnp.concatenate(o_c, axis=1)
    sb = krow(nb)
    cur = jnp.right_shift(tq(nb), int(math.log2(SLC_BLOCK)))
    forced = (sb == 0) | (sb == cur) | (sb == cur - 1)
    future = sb > cur
    imp = jnp.where(forced, FORCED_SCORE, jnp.where(future, -FORCED_SCORE, imp))
    sub = 8
    rank_scr[...] = jnp.zeros(rank_scr.shape, jnp.int32)
    last_block = (q0 + qb - 1) // SLC_BLOCK
    sb_l = sb[:, :LANES]
    for grp in range(n_slc // sub):
        @pl.when(grp * sub <= last_block)
        def _():
            for lc in range(qb // LANES):
                imp_l = imp[:, lc * LANES:(lc + 1) * LANES]
                chunks = [imp_l[r * sub:(r + 1) * sub] for r in range(nb // sub)]
                ranks = [jnp.zeros((sub, LANES), jnp.int32) for _ in chunks]
                for sp in range(grp * sub, (grp + 1) * sub):
                    row = imp_l[sp:sp + 1, :]
                    for r, blk in enumerate(chunks):
                        if r * sub > sp:
                            beats = row >= blk
                        elif (r + 1) * sub - 1 <= sp:
                            beats = row > blk
                        else:
                            beats = (row > blk) | ((row == blk) & (sb_l[r * sub:(r + 1) * sub] > sp))
                        ranks[r] = ranks[r] + jnp.where(beats, 1, 0)
                rank_scr[:, lc * LANES:(lc + 1) * LANES] += jnp.concatenate(ranks, axis=0)
    selneg = jnp.where((rank_scr[...] < top_n) & jnp.logical_not(future), 0.0, NEG_INF).astype(BF16)
    for h in range(hg):
        qT_scr[LANES:LANES + nb, h * qb:(h + 1) * qb] = selneg

    qT_aug = qT_scr[...]

    def scores(j):
        k0 = pl.multiple_of(j * kt, kt)
        ka = jnp.concatenate([ks_ref[0, pl.ds(k0, kt), :], et_ref[pl.ds(k0, kt), :]], axis=1)
        return jnp.dot(ka, qT_aug, preferred_element_type=F32)

    def fold(s, j, causal):
        vT = v_tiles(vsT_ref, pl.multiple_of(j * kt, kt), kt)
        if causal:
            bias = jnp.where(krow(kt, j * kt) <= tq(kt), 0.0, NEG_INF)
            bias = jnp.concatenate([bias, bias], axis=1)
        for hp in range(hg // 2):
            sl = slice(hp * 2 * qb, (hp + 1) * 2 * qb)
            sh = s[:, sl] + bias if causal else s[:, sl]
            m_old = m_scr[:, sl]
            m_new = jnp.maximum(m_old, jnp.max(sh, axis=0, keepdims=True))
            pv = jnp.dot(vT, jnp.exp2(sh - m_new).astype(BF16), preferred_element_type=F32)
            acc_scr[:, sl] = jnp.exp2(m_old - m_new) * acc_scr[:, sl] + pv
            m_scr[:, sl] = m_new

    m_scr[...] = jnp.full(m_scr.shape, NEG_INF, F32)
    acc_scr[...] = jnp.zeros(acc_scr.shape, F32)

    def main_pair(i, carry):
        s0, s1 = scores(2 * i), scores(2 * i + 1)
        fold(s0, 2 * i, False)
        fold(s1, 2 * i + 1, False)
        return carry

    lax.fori_loop(0, c // 2, main_pair, 0)

    @pl.when(c % 2 == 1)
    def _():
        fold(scores(c - 1), c - 1, False)

    fold(scores(c), c, True)
    o_s = acc_scr[:dh, :] * (1.0 / acc_scr[dh:dh + 1, :])

    span = WINDOW + LANES
    o_w = [[None] * (qb // LANES) for _ in range(hg)]
    for r in range(qb // LANES):
        t0 = q0 + r * LANES
        w0 = pl.multiple_of(jnp.maximum(t0 - WINDOW, 0), LANES)
        dist = (t0 + lax.broadcasted_iota(jnp.int32, (span, LANES), 1)
                - (w0 + lax.broadcasted_iota(jnp.int32, (span, LANES), 0)))
        bias_w = jnp.where((dist >= 0) & (dist < WINDOW), 0.0, NEG_INF)
        bias_w = jnp.concatenate([bias_w, bias_w], axis=1)
        qT_r = jnp.concatenate([qT[:, h * qb + r * LANES:h * qb + (r + 1) * LANES] for h in range(hg)], axis=1)
        s_w = jnp.dot(kw_ref[0, pl.ds(w0, span), :], qT_r, preferred_element_type=F32)
        vwT = v_tiles(vwT_ref, w0, span)
        for hp in range(hg // 2):
            sh = s_w[:, hp * 2 * LANES:(hp + 1) * 2 * LANES] + bias_w
            p = jnp.exp2(sh - jnp.max(sh, axis=0, keepdims=True)).astype(BF16)
            acc_w = jnp.dot(vwT, p, preferred_element_type=F32)
            ow = acc_w[:dh] * (1.0 / acc_w[dh:dh + 1])
            o_w[2 * hp][r], o_w[2 * hp + 1][r] = ow[:, :LANES], ow[:, LANES:]
    o_w = jnp.concatenate([t for per_head in o_w for t in per_head], axis=1)

    has_cmp = (tq(1) >= CMP_BLOCK - 1).astype(F32)
    gts = gT_scr[pl.ds(pl.multiple_of(g * (3 * hg), 8), 3 * hg), :]
    outs = []
    for h in range(hg):
        sl = slice(h * qb, (h + 1) * qb)
        outs.append(gts[3 * h:3 * h + 1] * has_cmp * o_c[:, sl] + gts[3 * h + 1:3 * h + 2] * o_s[:, sl]
                    + gts[3 * h + 2:3 * h + 3] * o_w[:, sl])
    for j in range(hg // 2):
        pair = jnp.concatenate([outs[2 * j], outs[2 * j + 1]], axis=0)
        for r in range(qb // LANES):
            rows = slice(r * LANES, (r + 1) * LANES)
            o_ref[rows, j * LANES:(j + 1) * LANES] = pair[:, rows].T.astype(o_ref.dtype)


def _nsa(qk, gates, kc, vcT, vsT, vwT, c2sT, et, B, S, *, qb=256):
    G, Hg, dh = NSA_GROUPS, NSA_HG, NSA_DH
    T = B * S
    nq = S // qb
    n_slc = S // SLC_BLOCK
    kt = qb
    assert n_slc <= dh and 2 * dh == LANES and qb % LANES == 0 and S % qb == 0 and WINDOW % qb == 0
    assert S >= WINDOW + qb
    qw = Hg * dh
    ncp = kc.shape[1]
    k_col = G * qw // LANES
    g_col = gates.shape[1] // MXU_N - 1
    seq_k = lambda col: pl.BlockSpec((1, S, LANES), lambda b, g, c: (b, 0, col))
    seq_vT = pl.BlockSpec((1, S // LANES, LANES, LANES), lambda b, g, c: (b, 0, 0, 0))
    qk3 = qk.reshape(B, S, qk.shape[1])
    return pl.pallas_call(
        functools.partial(_nsa_body, kt=kt, top_n=min(SLC_TOPN, n_slc), n_slc=n_slc, hg=Hg, dh=dh),
        grid=(B, G, nq),
        in_specs=[pl.BlockSpec((qb, qw), lambda b, g, c: (b * nq + c, g)),
                  pl.BlockSpec((qb, MXU_N), lambda b, g, c: (b * nq + c, g_col)),
                  pl.BlockSpec((1, ncp, LANES), lambda b, g, c: (b, 0, 0)),
                  pl.BlockSpec((1, LANES, ncp), lambda b, g, c: (b, 0, 0)),
                  seq_k(k_col), seq_vT, seq_k(k_col + 1), seq_vT,
                  _resident(c2sT.shape), _resident(et.shape)],
        out_specs=pl.BlockSpec((qb, qw), lambda b, g, c: (b * nq + c, g)),
        out_shape=jax.ShapeDtypeStruct((T, G * qw), BF16),
        scratch_shapes=[pltpu.VMEM((2 * LANES, Hg * qb), BF16), pltpu.VMEM((LANES, qb), F32),
                        pltpu.VMEM((1, Hg * qb), F32), pltpu.VMEM((dh + BF16_ROWS, Hg * qb), F32),
                        pltpu.VMEM((dh, qb), jnp.int32)],
        compiler_params=pltpu.CompilerParams(dimension_semantics=("parallel", "arbitrary", "arbitrary"),
                                             vmem_limit_bytes=VMEM_LIMIT),
        name="nsa",
    )(qk, gates, kc, vcT, qk3, vsT, qk3, vwT, c2sT, et)


def _nsa_constants(S):
    ncp = S // CMP_STRIDE
    n_slc = S // SLC_BLOCK
    cmp_start = np.arange(ncp) * CMP_STRIDE
    s_start = np.arange(n_slc) * SLC_BLOCK
    overlap = np.clip(np.minimum(cmp_start[:, None] + CMP_BLOCK, s_start[None, :] + SLC_BLOCK)
                      - np.maximum(cmp_start[:, None], s_start[None, :]), 0, None)
    c2sT = np.zeros((LANES, ncp), np.float32)
    c2sT[:n_slc, :] = overlap.T.astype(np.float32) / CMP_BLOCK
    c2sT[:, ncp - 1] = 0.0
    et = np.zeros((S, LANES), np.float32)
    et[np.arange(S), np.arange(S) // SLC_BLOCK] = 1.0
    return jnp.asarray(c2sT, BF16), jnp.asarray(et, BF16)


def _ret_body(q_ref, kT_ref, v_ref, decay_ref, zeta_ref, xi_ref, gch_ref, gn_ref, o_ref, *, C, dk):
    S = q_ref.shape[1]
    dv = gn_ref.shape[2]
    heads = LANES // dk
    N = S // C
    units = [(n, a) for n in range(N) for a in range(heads)]
    rows = lambda n: slice(n * C, (n + 1) * C)
    val = lambda n, a: v_ref[0, rows(n), a * dv:(a + 1) * dv]
    qp = lambda n: q_ref[0, rows(n), :]

    def own_rows(x, a):
        return jnp.concatenate([x if b == a else jnp.zeros_like(x) for b in range(heads)], axis=0)

    kT = {(n, a): kT_ref[0, 0, n, a * dk:(a + 1) * dk, :] for n, a in units}
    inner = {u: jnp.dot(qp(u[0]), own_rows(kT[u], u[1]), preferred_element_type=F32) for u in units}
    inner = {u: (inner[u] * decay_ref[u[1]]).astype(BF16) for u in units}
    o = {u: jnp.dot(inner[u], val(*u), preferred_element_type=F32) for u in units}
    kv = {(n, a): jnp.dot((kT[(n, a)].astype(F32) * zeta_ref[a]).astype(BF16), val(n, a),
                          preferred_element_type=F32) for n, a in units}
    state = {}
    for a in range(heads):
        R = jnp.zeros((dk, dv), F32)
        for n in range(N):
            state[(n, a)] = own_rows(R.astype(BF16), a)
            R = gch_ref[a] * R + kv[(n, a)]
    o = {u: o[u] + xi_ref[u[1]] * jnp.dot(qp(u[0]), state[u], preferred_element_type=F32) for u in units}
    mu = {u: jnp.mean(o[u], axis=-1, keepdims=True) for u in units}
    d = {u: o[u] - mu[u] for u in units}
    var = {u: jnp.mean(d[u] * d[u], axis=-1, keepdims=True) for u in units}
    for n, a in units:
        o_ref[0, rows(n), a * dv:(a + 1) * dv] = (d[(n, a)] * lax.rsqrt(var[(n, a)] + GN_EPS)
                                                  * gn_ref[a]).astype(o_ref.dtype)


def _retention(qk, kT, v, gn_gain, B, S, *, C=RET_CHUNK):
    H, dk, dv = RET_HEADS, RET_DK, RET_DV
    hp = LANES // dk
    assert C == LANES and dv == LANES
    log_g = jnp.log(1.0 - 2.0 ** (-5.0 - jnp.arange(H, dtype=F32)))
    i = jnp.arange(C, dtype=F32)
    diff = i[:, None] - i[None, :]
    decay = jnp.where(diff >= 0, jnp.exp(jnp.maximum(diff, 0.0) * log_g[:, None, None]), 0.0)
    zeta = jnp.exp((C - 1.0 - i)[None, :] * log_g[:, None]).reshape(H, 1, C)
    xi = jnp.broadcast_to(jnp.exp((i + 1.0)[None, :] * log_g[:, None])[:, :, None], (H, C, LANES))
    gch = jnp.broadcast_to(jnp.exp(C * log_g)[:, None, None], (H, 1, dv))
    per_pair = lambda shape: pl.BlockSpec((hp,) + shape, lambda b, p: (p, 0, 0))
    qk3 = qk.reshape(B, S, qk.shape[1])
    return pl.pallas_call(
        functools.partial(_ret_body, C=C, dk=dk),
        grid=(B, H // hp),
        in_specs=[pl.BlockSpec((1, S, LANES), lambda b, p: (b, 0, p)),
                  pl.BlockSpec((1, 1, S // LANES, LANES, LANES), lambda b, p: (b, p, 0, 0, 0)),
                  pl.BlockSpec((1, S, hp * dv), lambda b, p: (b, 0, p)),
                  per_pair((C, C)), per_pair((1, C)), per_pair((C, LANES)), per_pair((1, dv)),
                  per_pair((1, dv))],
        out_specs=pl.BlockSpec((1, S, hp * dv), lambda b, p: (b, 0, p)),
        out_shape=jax.ShapeDtypeStruct((B, S, H * dv), BF16),
        compiler_params=pltpu.CompilerParams(dimension_semantics=("parallel", "parallel"),
                                             vmem_limit_bytes=VMEM_LIMIT),
        name="retention",
    )(qk3, kT, v.reshape(B, S, H * dv), decay, zeta, xi, gch, gn_gain.reshape(H, 1, dv))


def _merge_body(x_ref, a_ref, r_ref, gr_ref, ga_ref, gb_ref, wa_ref, wr_ref, wo_ref, o_ref):
    g = gr_ref[...].astype(F32)
    r = (g * jax.nn.sigmoid(g) * r_ref[...].astype(F32)).astype(BF16)
    ya = jnp.dot(a_ref[...], wa_ref[...], preferred_element_type=F32)
    yr = jnp.dot(r, wr_ref[...], preferred_element_type=F32)
    mixed = jax.nn.sigmoid(ga_ref[...].astype(F32)) * ya + jax.nn.sigmoid(gb_ref[...].astype(F32)) * yr
    o_ref[...] = x_ref[...] + jnp.dot(mixed.astype(BF16), wo_ref[...], preferred_element_type=F32)


def _merge(x2d, a, r, gates, wa, wr, wo, *, tm=512):
    T, D = x2d.shape
    col = lambda j: pl.BlockSpec((tm, D), lambda i: (i, j))
    return pl.pallas_call(
        _merge_body,
        grid=(T // tm,),
        in_specs=[col(0), col(0), col(0), col(0), col(1), col(2),
                  _resident(wa.shape), _resident(wr.shape), _resident(wo.shape)],
        out_specs=col(0),
        out_shape=jax.ShapeDtypeStruct((T, D), F32),
        compiler_params=pltpu.CompilerParams(dimension_semantics=("parallel",),
                                             vmem_limit_bytes=VMEM_LIMIT),
        name="merge",
    )(x2d, a, r, gates, gates, gates, wa, wr, wo)


def _layer(x, p, final_norm):
    B, S, D = x.shape
    T = B * S
    G, Hg, dh = NSA_GROUPS, NSA_HG, NSA_DH
    H, dk, dv = RET_HEADS, RET_DK, RET_DV
    bf = lambda w: w.astype(BF16)

    x1 = _ffn(x.reshape(T, D), p["ffn1_norm"], bf(p["ffn1_w_gate"]), bf(p["ffn1_w_up"]),
              bf(p["ffn1_w_down"]), final_norm, final=False)

    w = p["w_in"]
    widths = (G * Hg * dh,) + (G * dh,) * 6 + (3 * G * Hg, H * dk, H * dk, H * dv, H * dv, 2 * D)
    offs = np.concatenate([[0], np.cumsum(widths)])
    seg = lambda i: w[:, int(offs[i]):int(offs[i + 1])]
    (q_a, kc_a, vc_a, ks_a, vs_a, kw_a, vw_a, g_a, q_r, k_r, v_r, g_r, g_m) = [seg(i) for i in range(13)]
    gate_pad = MXU_N - g_a.shape[1]
    wa = bf(jnp.concatenate([q_a * (dh ** -0.5 * LOG2E), ks_a, kw_a], axis=1))
    wb = bf(jnp.concatenate([q_r * dk ** -0.5, k_r], axis=1))
    wc = bf(jnp.concatenate([v_r, kc_a, vc_a, vs_a, vw_a], axis=1))
    wd = bf(jnp.concatenate([g_r, g_m, g_a, jnp.zeros((D, gate_pad), F32)], axis=1))
    pos = jnp.arange(S)
    tab_a = _rope_tables(pos, ROPE_DIM, dh, ROPE_THETA)
    tab_b = _rope_tables(pos, dk, dk, RET_ROPE_THETA)
    qk_n, qk_r, krT, v_ret, kc_tok, vc_tok, vsT, vwT, gates = _proj(
        x1, p["mix_norm"], wa, wb, wc, wd, tab_a, tab_b, B, S)

    chunk = lambda t: t.reshape(B, S // CMP_STRIDE, CMP_STRIDE * G * dh)
    kw1, kw2, pe = _compress_weights(p["cmp_k_w1"], p["cmp_k_w2"], p["cmp_pos_emb"])
    vw1, vw2, _ = _compress_weights(p["cmp_v_w1"], p["cmp_v_w2"], p["cmp_pos_emb"])
    cmp_end = jnp.arange(S // CMP_STRIDE) * CMP_STRIDE + (CMP_BLOCK - 1)
    kc, vcT = _compress(chunk(kc_tok), chunk(vc_tok), pe, kw1, kw2, vw1, vw2,
                        _rope_tables(cmp_end, ROPE_DIM, dh, ROPE_THETA))
    c2sT, et = _nsa_constants(S)
    a_out = _nsa(qk_n, gates, kc, vcT, vsT, vwT, c2sT, et, B, S)
    r_out = _retention(qk_r, krT, v_ret, p["ret_gn_gain"], B, S).reshape(T, H * dv)

    x2 = _merge(x1, a_out, r_out, gates, bf(p["w_branch_nsa"]), bf(p["w_branch_ret"]), bf(p["w_out"]))
    x3 = _ffn(x2, p["ffn2_norm"], bf(p["ffn2_w_gate"]), bf(p["ffn2_w_up"]), bf(p["ffn2_w_down"]),
              final_norm, final=True)
    return x3.reshape(B, S, D)


def kernel(x, ffn1_norm, ffn1_w_gate, ffn1_w_up, ffn1_w_down, mix_norm, w_in, cmp_pos_emb, cmp_k_w1,
           cmp_k_w2, cmp_v_w1, cmp_v_w2, ret_gn_gain, w_branch_nsa, w_branch_ret, w_out, ffn2_norm,
           ffn2_w_gate, ffn2_w_up, ffn2_w_down, final_norm):
    assert ffn1_norm.shape[0] == 1, "single-layer stack"
    names = ("ffn1_norm", "ffn1_w_gate", "ffn1_w_up", "ffn1_w_down", "mix_norm", "w_in", "cmp_pos_emb",
             "cmp_k_w1", "cmp_k_w2", "cmp_v_w1", "cmp_v_w2", "ret_gn_gain", "w_branch_nsa",
             "w_branch_ret", "w_out", "ffn2_norm", "ffn2_w_gate", "ffn2_w_up", "ffn2_w_down")
    vals = (ffn1_norm, ffn1_w_gate, ffn1_w_up, ffn1_w_down, mix_norm, w_in, cmp_pos_emb, cmp_k_w1,
            cmp_k_w2, cmp_v_w1, cmp_v_w2, ret_gn_gain, w_branch_nsa, w_branch_ret, w_out, ffn2_norm,
            ffn2_w_gate, ffn2_w_up, ffn2_w_down)
    p = {n: v[0] for n, v in zip(names, vals)}
    return _layer(x, p, final_norm)
```

```python
import functools
import math

import numpy as np
import jax
import jax.numpy as jnp
from jax import lax
from jax.experimental import pallas as pl
from jax.experimental.pallas import tpu as pltpu

F32 = jnp.float32
BF16 = jnp.bfloat16

NSA_HEADS = 16
NSA_GROUPS = 2
NSA_HG = NSA_HEADS // NSA_GROUPS
NSA_DH = 64
CMP_BLOCK = 32
CMP_STRIDE = 16
SLC_BLOCK = 64
SLC_TOPN = 16
WINDOW = 512
Q_BLOCK = 128
ROPE_THETA = 500000.0
ROPE_DIM = NSA_DH // 4
FORCED_SCORE = 1.0e4
RET_HEADS = 8
RET_DK = 64
RET_DV = 128
RET_CHUNK = 128
RET_ROPE_THETA = 10000.0
EPS = 1e-6
GN_EPS = 1e-5
NEG_INF = -1e30
LOG2E = math.log2(math.e)

LANES = 128
BF16_ROWS = 16
MXU_N = 256
VMEM_LIMIT = 56 * 1024 * 1024

NT_DIMS = (((1,), (1,)), ((), ()))


def _rms(x, g, eps=EPS):
    return x * lax.rsqrt(jnp.mean(x * x, axis=-1, keepdims=True) + eps) * g


def _resident(shape):
    nd = len(shape)
    return pl.BlockSpec(shape, lambda *_: (0,) * nd, pipeline_mode=pl.Buffered(1))


def _ffn_body(x_ref, g_ref, wg_ref, wu_ref, wd_ref, fg_ref, o_ref, *, tf, final):
    x = x_ref[...]
    h = _rms(x, g_ref[...]).astype(BF16)
    acc = jnp.zeros(x.shape, F32)
    for c in range(wg_ref.shape[1] // tf):
        sl = slice(c * tf, (c + 1) * tf)
        g = jnp.dot(h, wg_ref[:, sl], preferred_element_type=F32)
        u = jnp.dot(h, wu_ref[:, sl], preferred_element_type=F32)
        a = (g * jax.nn.sigmoid(g) * u).astype(BF16)
        acc = acc + jnp.dot(a, wd_ref[sl, :], preferred_element_type=F32)
    y = x + 0.5 * acc
    if final:
        y = _rms(y, fg_ref[...])
    o_ref[...] = y


def _ffn(x2d, gain, wg, wu, wd, final_gain, *, final, tm=512):
    T, D = x2d.shape
    F = wg.shape[1]
    tf = F // 2 if (F // 2) % LANES == 0 else F
    tok = pl.BlockSpec((tm, D), lambda i: (i, 0))
    return pl.pallas_call(
        functools.partial(_ffn_body, tf=tf, final=final),
        grid=(T // tm,),
        in_specs=[tok, _resident((1, D)), _resident((D, F)), _resident((D, F)), _resident((F, D)),
                  _resident((1, D))],
        out_specs=tok,
        out_shape=jax.ShapeDtypeStruct((T, D), F32),
        compiler_params=pltpu.CompilerParams(dimension_semantics=("parallel",),
                                             vmem_limit_bytes=VMEM_LIMIT),
        name="ffn_final" if final else "ffn",
    )(x2d, gain.reshape(1, D), wg, wu, wd, final_gain.reshape(1, D))


def _rope_slab(y, tab_ref, shift):
    return (y * tab_ref[0] + pltpu.roll(y, LANES - shift, 1) * tab_ref[1]
            + pltpu.roll(y, shift, 1) * tab_ref[2])


def _proj_body(x_ref, g_ref, wa_ref, wb_ref, wc_ref, wd_ref, ta_ref, tb_ref,
               oa_ref, ob_ref, okrT_ref, ov_ref, okc_ref, ovc_ref, ovsT_ref, ovwT_ref, od_ref):
    h = _rms(x_ref[...], g_ref[...]).astype(BF16)
    tm = h.shape[0]
    nv = ov_ref.shape[1] // LANES
    nkr = okrT_ref.shape[1]

    def slabs(w_ref):
        for c in range(w_ref.shape[1] // MXU_N):
            y = jnp.dot(h, w_ref[:, c * MXU_N:(c + 1) * MXU_N], preferred_element_type=F32)
            for s in range(MXU_N // LANES):
                yield c * (MXU_N // LANES) + s, y[:, s * LANES:(s + 1) * LANES]

    def put(o_ref, i, ys):
        o_ref[:, i * LANES:(i + 1) * LANES] = ys.astype(o_ref.dtype)

    def put_t(o_ref, ys, *lead):
        for r in range(tm // LANES):
            o_ref[(0,) + lead + (r,)] = ys[r * LANES:(r + 1) * LANES, :].T.astype(o_ref.dtype)

    for i, ys in slabs(wa_ref):
        put(oa_ref, i, _rope_slab(ys, ta_ref, ROPE_DIM // 2))
    nb = wb_ref.shape[1] // LANES
    for i, ys in slabs(wb_ref):
        ys = _rope_slab(ys, tb_ref, RET_DK // 2)
        if i < nb - nkr:
            put(ob_ref, i, ys)
        else:
            put_t(okrT_ref, ys, i - (nb - nkr))
    for i, ys in slabs(wc_ref):
        if i < nv:
            put(ov_ref, i, ys)
        elif i == nv:
            put(okc_ref, 0, ys)
        elif i == nv + 1:
            put(ovc_ref, 0, ys)
        elif i == nv + 2:
            put_t(ovsT_ref, ys)
        else:
            put_t(ovwT_ref, ys)
    for i, ys in slabs(wd_ref):
        put(od_ref, i, ys)


def _proj(x2d, gain, wa, wb, wc, wd, tab_a, tab_b, B, S, *, tm=512):
    T, D = x2d.shape
    spt = S // tm
    nv = wc.shape[1] - 4 * LANES
    tok = lambda n: pl.BlockSpec((tm, n), lambda i: (i, 0))
    tab = pl.BlockSpec((3, tm, LANES), lambda i: (0, i % spt, 0))
    tr = pl.BlockSpec((1, tm // LANES, LANES, LANES), lambda i: (i // spt, i % spt, 0, 0))
    nkr = RET_HEADS * RET_DK // LANES
    tr_kr = pl.BlockSpec((1, nkr, tm // LANES, LANES, LANES), lambda i: (i // spt, 0, i % spt, 0, 0))
    return pl.pallas_call(
        _proj_body,
        grid=(T // tm,),
        in_specs=[tok(D), _resident((1, D)), _resident(wa.shape), _resident(wb.shape),
                  _resident(wc.shape), _resident(wd.shape), tab, tab],
        out_specs=[tok(wa.shape[1]), tok(wb.shape[1] - nkr * LANES), tr_kr, tok(nv), tok(LANES), tok(LANES),
                   tr, tr, tok(wd.shape[1])],
        out_shape=[jax.ShapeDtypeStruct((T, wa.shape[1]), BF16),
                   jax.ShapeDtypeStruct((T, wb.shape[1] - nkr * LANES), BF16),
                   jax.ShapeDtypeStruct((B, nkr, S // LANES, LANES, LANES), BF16),
                   jax.ShapeDtypeStruct((T, nv), BF16),
                   jax.ShapeDtypeStruct((T, LANES), BF16),
                   jax.ShapeDtypeStruct((T, LANES), BF16),
                   jax.ShapeDtypeStruct((B, S // LANES, LANES, LANES), BF16),
                   jax.ShapeDtypeStruct((B, S // LANES, LANES, LANES), BF16),
                   jax.ShapeDtypeStruct((T, wd.shape[1]), BF16)],
        compiler_params=pltpu.CompilerParams(dimension_semantics=("parallel",),
                                             vmem_limit_bytes=VMEM_LIMIT),
        name="proj",
    )(x2d, gain.reshape(1, D), wa, wb, wc, wd, tab_a, tab_b)


def _rope_tables(pos, rot_dim, head_dim, theta):
    half = rot_dim // 2
    freqs = theta ** (-(jnp.arange(half, dtype=F32) * 2.0 / rot_dim))
    ang = pos.astype(F32)[:, None] * freqs[None, :]
    cos, sin = jnp.cos(ang), jnp.sin(ang)
    n = pos.shape[0]
    rest = head_dim - rot_dim
    cos_h = jnp.concatenate([cos, cos, jnp.ones((n, rest), F32)], axis=-1)
    sl_h = jnp.concatenate([-sin, jnp.zeros((n, half + rest), F32)], axis=-1)
    sr_h = jnp.concatenate([jnp.zeros((n, half), F32), sin, jnp.zeros((n, rest), F32)], axis=-1)
    rep = LANES // head_dim
    return jnp.stack([jnp.tile(t, (1, rep)) for t in (cos_h, sl_h, sr_h)])


def _gelu_tanh(x):
    return 0.5 * x * (1.0 + jnp.tanh(math.sqrt(2.0 / math.pi) * (x + 0.044715 * (x * x * x))))


def _compress_body(kt_ref, vt_ref, pe_ref, kw1_ref, kw2_ref, vw1_ref, vw2_ref, tab_ref,
                   kc_ref, vcT_ref):
    pe = pe_ref[...]
    n = kt_ref.shape[1]

    def mlp(tok_ref, w1_ref, w2_ref):
        c = tok_ref[0].astype(F32)
        lo = (c + pe[0:1]).astype(BF16)
        hi = (c + pe[1:2]).astype(BF16)
        out = jnp.zeros((n, LANES), F32)
        for g in range(NSA_GROUPS):
            top = jnp.dot(lo, w1_ref[g, 0], preferred_element_type=F32)
            bot = jnp.dot(hi, w1_ref[g, 1], preferred_element_type=F32)
            hid = _gelu_tanh(top + pltpu.roll(bot, n - 1, 0))
            out = out + jnp.dot(hid.astype(BF16), w2_ref[g], preferred_element_type=F32)
        return out

    kc_ref[0] = _rope_slab(mlp(kt_ref, kw1_ref, kw2_ref), tab_ref, ROPE_DIM // 2).astype(kc_ref.dtype)
    vc = mlp(vt_ref, vw1_ref, vw2_ref)
    for r in range(n // LANES):
        vcT_ref[0, :, r * LANES:(r + 1) * LANES] = vc[r * LANES:(r + 1) * LANES, :].T.astype(vcT_ref.dtype)


def _compress(k_tok, v_tok, pe, kw1, kw2, vw1, vw2, tab):
    B, n, w = k_tok.shape
    tok = pl.BlockSpec((1, n, w), lambda b: (b, 0, 0))
    return pl.pallas_call(
        _compress_body,
        grid=(B,),
        in_specs=[tok, tok, _resident(pe.shape), _resident(kw1.shape), _resident(kw2.shape),
                  _resident(vw1.shape), _resident(vw2.shape), _resident(tab.shape)],
        out_specs=[pl.BlockSpec((1, n, LANES), lambda b: (b, 0, 0)),
                   pl.BlockSpec((1, LANES, n), lambda b: (b, 0, 0))],
        out_shape=[jax.ShapeDtypeStruct((B, n, LANES), BF16), jax.ShapeDtypeStruct((B, LANES, n), BF16)],
        compiler_params=pltpu.CompilerParams(dimension_semantics=("parallel",),
                                             vmem_limit_bytes=VMEM_LIMIT),
        name="compress",
    )(k_tok, v_tok, pe, kw1, kw2, vw1, vw2, tab)


def _compress_weights(w1, w2, pe):
    G, dh = NSA_GROUPS, NSA_DH
    hid = w1.shape[1]
    halves = w1.reshape(2, CMP_STRIDE, dh, hid)
    w1e = jnp.zeros((G, 2, CMP_STRIDE, G, dh, hid), F32)
    w2e = jnp.zeros((G, hid, G, dh), F32)
    for g in range(G):
        w1e = w1e.at[g, :, :, g].set(halves)
        w2e = w2e.at[g, :, g].set(w2)
    pe_e = jnp.broadcast_to(pe.reshape(2, CMP_STRIDE, 1, dh), (2, CMP_STRIDE, G, dh))
    return (w1e.reshape(G, 2, CMP_STRIDE * G * dh, hid).astype(BF16), w2e.reshape(G, hid, G * dh).astype(BF16),
            pe_e.reshape(2, CMP_STRIDE * G * dh))


def _nsa_body(q_ref, gate_ref, kc_ref, vcT_ref, ks_ref, vsT_ref, kw_ref, vwT_ref, c2sT_ref, et_ref,
              o_ref, qT_scr, gT_scr, m_scr, acc_scr, rank_scr, *, kt, top_n, n_slc, hg, dh):
    qb = q_ref.shape[0]
    nh = q_ref.shape[1] // dh
    groups = nh // hg
    ncp = kc_ref.shape[1]
    c = pl.program_id(1)
    q0 = c * qb
    ones_rows = jnp.ones((BF16_ROWS, 1), BF16)
    grp_rows = lambda g: slice(g * dh, (g + 1) * dh)

    def v_tiles(vT_ref, k0, n):
        t = vT_ref[0, pl.ds(k0 // LANES, n // LANES)]
        vT = jnp.concatenate([t[i] for i in range(n // LANES)], axis=1)
        ones = jnp.broadcast_to(ones_rows, (BF16_ROWS, n))
        return [jnp.concatenate([vT[grp_rows(g)], ones], axis=0) for g in range(groups)]

    def tq(n):
        return q0 + lax.broadcasted_iota(jnp.int32, (n, qb), 1)

    def krow(n, k0=0):
        return k0 + lax.broadcasted_iota(jnp.int32, (n, qb), 0)

    qT_scr[...] = jnp.zeros(qT_scr.shape, BF16)
    for r in range(qb // LANES):
        rows = slice(r * LANES, (r + 1) * LANES)
        for j in range(nh // 2):
            t = q_ref[rows, j * LANES:(j + 1) * LANES].astype(F32).T.astype(BF16)
            for half in range(2):
                h = 2 * j + half
                qT_scr[grp_rows(h // hg), h * qb + r * LANES:h * qb + (r + 1) * LANES] = (
                    t[half * dh:(half + 1) * dh])
        gT_scr[:, rows] = jax.nn.sigmoid(gate_ref[rows, :LANES].astype(F32).T)

    qT = qT_scr[:LANES, :]
    s_c = jnp.dot(kc_ref[0], qT, preferred_element_type=F32)
    cend = krow(ncp) * CMP_STRIDE + (CMP_BLOCK - 1)
    bias_c = jnp.where(cend <= tq(ncp), 0.0, NEG_INF)
    bias_c = jnp.concatenate([bias_c, bias_c], axis=1)
    nb = dh
    vcT = vcT_ref[0]
    lhs_c = [jnp.concatenate([vcT[grp_rows(g)], jnp.broadcast_to(ones_rows, (BF16_ROWS, ncp)),
                              c2sT_ref[:nb, :]], axis=0) for g in range(groups)]
    imps = [jnp.zeros((nb, qb), F32) for _ in range(groups)]
    o_c = []
    for hp in range(nh // 2):
        g = 2 * hp // hg
        sh = s_c[:, hp * 2 * qb:(hp + 1) * 2 * qb] + bias_c
        e = jnp.exp2(sh - jnp.max(sh, axis=0, keepdims=True)).astype(BF16)
        r = jnp.dot(lhs_c[g], e, preferred_element_type=F32)
        inv_l = 1.0 / r[dh:dh + 1]
        o_c.append(r[:dh] * inv_l)
        w = r[dh + BF16_ROWS:] * inv_l
        imps[g] = imps[g] + w[:, :qb] + w[:, qb:]
    o_c = jnp.concatenate(o_c, axis=1)
    sb = krow(nb)
    cur = jnp.right_shift(tq(nb), int(math.log2(SLC_BLOCK)))
    forced = (sb == 0) | (sb == cur) | (sb == cur - 1)
    future = sb > cur
    imps = [jnp.where(forced, FORCED_SCORE, jnp.where(future, -FORCED_SCORE, imp)) for imp in imps]
    sub = 8
    rank_scr[...] = jnp.zeros(rank_scr.shape, jnp.int32)
    last_block = (q0 + qb - 1) // SLC_BLOCK
    sb_l = sb[:, :LANES]
    for grp in range(n_slc // sub):
        @pl.when(grp * sub <= last_block)
        def _():
            for g in range(groups):
                for lc in range(qb // LANES):
                    imp_l = imps[g][:, lc * LANES:(lc + 1) * LANES]
                    chunks = [imp_l[r * sub:(r + 1) * sub] for r in range(nb // sub)]
                    ranks = [jnp.zeros((sub, LANES), jnp.int32) for _ in chunks]
                    for sp in range(grp * sub, (grp + 1) * sub):
                        row = imp_l[sp:sp + 1, :]
                        for r, blk in enumerate(chunks):
                            if r * sub > sp:
                                beats = row >= blk
                            elif (r + 1) * sub - 1 <= sp:
                                beats = row > blk
                            else:
                                beats = (row > blk) | ((row == blk) & (sb_l[r * sub:(r + 1) * sub] > sp))
                            ranks[r] = ranks[r] + jnp.where(beats, 1, 0)
                    rank_scr[g, :, lc * LANES:(lc + 1) * LANES] += jnp.concatenate(ranks, axis=0)
    for g in range(groups):
        selneg = jnp.where((rank_scr[g] < top_n) & jnp.logical_not(future), 0.0, NEG_INF).astype(BF16)
        for h in range(g * hg, (g + 1) * hg):
            qT_scr[LANES:LANES + nb, h * qb:(h + 1) * qb] = selneg

    qT_aug = qT_scr[...]

    def scores(j):
        k0 = pl.multiple_of(j * kt, kt)
        ka = jnp.concatenate([ks_ref[0, pl.ds(k0, kt), :], et_ref[pl.ds(k0, kt), :]], axis=1)
        return jnp.dot(ka, qT_aug, preferred_element_type=F32)

    def fold(s, j, causal):
        vT = v_tiles(vsT_ref, pl.multiple_of(j * kt, kt), kt)
        if causal:
            bias = jnp.where(krow(kt, j * kt) <= tq(kt), 0.0, NEG_INF)
            bias = jnp.concatenate([bias, bias], axis=1)
        for hp in range(nh // 2):
            sl = slice(hp * 2 * qb, (hp + 1) * 2 * qb)
            sh = s[:, sl] + bias if causal else s[:, sl]
            m_old = m_scr[:, sl]
            m_new = jnp.maximum(m_old, jnp.max(sh, axis=0, keepdims=True))
            pv = jnp.dot(vT[2 * hp // hg], jnp.exp2(sh - m_new).astype(BF16), preferred_element_type=F32)
            acc_scr[:, sl] = jnp.exp2(m_old - m_new) * acc_scr[:, sl] + pv
            m_scr[:, sl] = m_new

    m_scr[...] = jnp.full(m_scr.shape, NEG_INF, F32)
    acc_scr[...] = jnp.zeros(acc_scr.shape, F32)

    def run_tiles(j0, n):
        ss = [scores(j0 + t) for t in range(n)]
        for t in range(n):
            fold(ss[t], j0 + t, False)

    def main_pair(i, carry):
        run_tiles(2 * i, 2)
        return carry

    lax.fori_loop(0, c // 2, main_pair, 0)

    @pl.when(c % 2 == 1)
    def _():
        fold(scores(c - 1), c - 1, False)

    fold(scores(c), c, True)
    o_s = acc_scr[:dh, :] * (1.0 / acc_scr[dh:dh + 1, :])

    span = WINDOW + LANES
    o_w = [[None] * (qb // LANES) for _ in range(nh)]
    for r in range(qb // LANES):
        t0 = q0 + r * LANES
        w0 = pl.multiple_of(jnp.maximum(t0 - WINDOW, 0), LANES)
        dist = (t0 + lax.broadcasted_iota(jnp.int32, (span, LANES), 1)
                - (w0 + lax.broadcasted_iota(jnp.int32, (span, LANES), 0)))
        bias_w = jnp.where((dist >= 0) & (dist < WINDOW), 0.0, NEG_INF)
        bias_w = jnp.concatenate([bias_w, bias_w], axis=1)
        qT_r = jnp.concatenate([qT[:, h * qb + r * LANES:h * qb + (r + 1) * LANES] for h in range(nh)], axis=1)
        s_w = jnp.dot(kw_ref[0, pl.ds(w0, span), :], qT_r, preferred_element_type=F32)
        vwT = v_tiles(vwT_ref, w0, span)
        for hp in range(nh // 2):
            sh = s_w[:, hp * 2 * LANES:(hp + 1) * 2 * LANES] + bias_w
            p = jnp.exp2(sh - jnp.max(sh, axis=0, keepdims=True)).astype(BF16)
            acc_w = jnp.dot(vwT[2 * hp // hg], p, preferred_element_type=F32)
            ow = acc_w[:dh] * (1.0 / acc_w[dh:dh + 1])
            o_w[2 * hp][r], o_w[2 * hp + 1][r] = ow[:, :LANES], ow[:, LANES:]
    o_w = jnp.concatenate([t for per_head in o_w for t in per_head], axis=1)

    has_cmp = (tq(1) >= CMP_BLOCK - 1).astype(F32)
    gts = gT_scr[:3 * nh, :]
    outs = []
    for h in range(nh):
        sl = slice(h * qb, (h + 1) * qb)
        outs.append(gts[3 * h:3 * h + 1] * has_cmp * o_c[:, sl] + gts[3 * h + 1:3 * h + 2] * o_s[:, sl]
                    + gts[3 * h + 2:3 * h + 3] * o_w[:, sl])
    for j in range(nh // 2):
        pair = jnp.concatenate([outs[2 * j], outs[2 * j + 1]], axis=0)
        for r in range(qb // LANES):
            rows = slice(r * LANES, (r + 1) * LANES)
            o_ref[rows, j * LANES:(j + 1) * LANES] = pair[:, rows].T.astype(o_ref.dtype)


def _nsa(qk, gates, kc, vcT, vsT, vwT, c2sT, et, B, S, *, qb=256):
    G, Hg, dh = NSA_GROUPS, NSA_HG, NSA_DH
    T = B * S
    nq = S // qb
    n_slc = S // SLC_BLOCK
    kt = qb
    assert n_slc <= dh and 2 * dh == LANES and qb % LANES == 0 and S % qb == 0 and WINDOW % qb == 0
    assert S >= WINDOW + qb
    qw = G * Hg * dh
    assert 3 * G * Hg <= LANES and G * dh == LANES
    ncp = kc.shape[1]
    k_col = qw // LANES
    g_col = gates.shape[1] // MXU_N - 1
    seq_k = lambda col: pl.BlockSpec((1, S, LANES), lambda b, c: (b, 0, col))
    seq_vT = pl.BlockSpec((1, S // LANES, LANES, LANES), lambda b, c: (b, 0, 0, 0))
    qk3 = qk.reshape(B, S, qk.shape[1])
    cols = G * Hg * qb
    return pl.pallas_call(
        functools.partial(_nsa_body, kt=kt, top_n=min(SLC_TOPN, n_slc), n_slc=n_slc, hg=Hg, dh=dh),
        grid=(B, nq),
        in_specs=[pl.BlockSpec((qb, qw), lambda b, c: (b * nq + c, 0)),
                  pl.BlockSpec((qb, MXU_N), lambda b, c: (b * nq + c, g_col)),
                  pl.BlockSpec((1, ncp, LANES), lambda b, c: (b, 0, 0)),
                  pl.BlockSpec((1, LANES, ncp), lambda b, c: (b, 0, 0)),
                  seq_k(k_col), seq_vT, seq_k(k_col + 1), seq_vT,
                  _resident(c2sT.shape), _resident(et.shape)],
        out_specs=pl.BlockSpec((qb, qw), lambda b, c: (b * nq + c, 0)),
        out_shape=jax.ShapeDtypeStruct((T, qw), BF16),
        scratch_shapes=[pltpu.VMEM((2 * LANES, cols), BF16), pltpu.VMEM((LANES, qb), F32),
                        pltpu.VMEM((1, cols), F32), pltpu.VMEM((dh + BF16_ROWS, cols), F32),
                        pltpu.VMEM((G, dh, qb), jnp.int32)],
        compiler_params=pltpu.CompilerParams(dimension_semantics=("parallel", "arbitrary"),
                                             vmem_limit_bytes=VMEM_LIMIT),
        name="nsa",
    )(qk, gates, kc, vcT, qk3, vsT, qk3, vwT, c2sT, et)


def _nsa_constants(S):
    ncp = S // CMP_STRIDE
    n_slc = S // SLC_BLOCK
    cmp_start = np.arange(ncp) * CMP_STRIDE
    s_start = np.arange(n_slc) * SLC_BLOCK
    overlap = np.clip(np.minimum(cmp_start[:, None] + CMP_BLOCK, s_start[None, :] + SLC_BLOCK)
                      - np.maximum(cmp_start[:, None], s_start[None, :]), 0, None)
    c2sT = np.zeros((LANES, ncp), np.float32)
    c2sT[:n_slc, :] = overlap.T.astype(np.float32) / CMP_BLOCK
    c2sT[:, ncp - 1] = 0.0
    et = np.zeros((S, LANES), np.float32)
    et[np.arange(S), np.arange(S) // SLC_BLOCK] = 1.0
    return jnp.asarray(c2sT, BF16), jnp.asarray(et, BF16)


def _ret_body(q_ref, kT_ref, v_ref, decay_ref, zeta_ref, xi_ref, gch_ref, gn_ref, o_ref, *, C, dk):
    S = q_ref.shape[1]
    dv = gn_ref.shape[2]
    heads = LANES // dk
    N = S // C
    units = [(n, a) for n in range(N) for a in range(heads)]
    rows = lambda n: slice(n * C, (n + 1) * C)
    val = lambda n, a: v_ref[0, rows(n), a * dv:(a + 1) * dv]
    qp = lambda n: q_ref[0, rows(n), :]

    def own_rows(x, a):
        return jnp.concatenate([x if b == a else jnp.zeros_like(x) for b in range(heads)], axis=0)

    kT = {(n, a): kT_ref[0, 0, n, a * dk:(a + 1) * dk, :] for n, a in units}
    inner = {u: jnp.dot(qp(u[0]), own_rows(kT[u], u[1]), preferred_element_type=F32) for u in units}
    inner = {u: (inner[u] * decay_ref[u[1]]).astype(BF16) for u in units}
    o = {u: jnp.dot(inner[u], val(*u), preferred_element_type=F32) for u in units}
    kv = {(n, a): jnp.dot((kT[(n, a)].astype(F32) * zeta_ref[a]).astype(BF16), val(n, a),
                          preferred_element_type=F32) for n, a in units}
    state = {}
    for a in range(heads):
        R = jnp.zeros((dk, dv), F32)
        for n in range(N):
            state[(n, a)] = own_rows(R.astype(BF16), a)
            R = gch_ref[a] * R + kv[(n, a)]
    o = {u: o[u] + xi_ref[u[1]] * jnp.dot(qp(u[0]), state[u], preferred_element_type=F32) for u in units}
    mu = {u: jnp.mean(o[u], axis=-1, keepdims=True) for u in units}
    d = {u: o[u] - mu[u] for u in units}
    var = {u: jnp.mean(d[u] * d[u], axis=-1, keepdims=True) for u in units}
    for n, a in units:
        o_ref[0, rows(n), a * dv:(a + 1) * dv] = (d[(n, a)] * lax.rsqrt(var[(n, a)] + GN_EPS)
                                                  * gn_ref[a]).astype(o_ref.dtype)


def _retention(qk, kT, v, gn_gain, B, S, *, C=RET_CHUNK):
    H, dk, dv = RET_HEADS, RET_DK, RET_DV
    hp = LANES // dk
    assert C == LANES and dv == LANES
    log_g = jnp.log(1.0 - 2.0 ** (-5.0 - jnp.arange(H, dtype=F32)))
    i = jnp.arange(C, dtype=F32)
    diff = i[:, None] - i[None, :]
    decay = jnp.where(diff >= 0, jnp.exp(jnp.maximum(diff, 0.0) * log_g[:, None, None]), 0.0)
    zeta = jnp.exp((C - 1.0 - i)[None, :] * log_g[:, None]).reshape(H, 1, C)
    xi = jnp.broadcast_to(jnp.exp((i + 1.0)[None, :] * log_g[:, None])[:, :, None], (H, C, LANES))
    gch = jnp.broadcast_to(jnp.exp(C * log_g)[:, None, None], (H, 1, dv))
    per_pair = lambda shape: pl.BlockSpec((hp,) + shape, lambda b, p: (p, 0, 0))
    qk3 = qk.reshape(B, S, qk.shape[1])
    return pl.pallas_call(
        functools.partial(_ret_body, C=C, dk=dk),
        grid=(B, H // hp),
        in_specs=[pl.BlockSpec((1, S, LANES), lambda b, p: (b, 0, p)),
                  pl.BlockSpec((1, 1, S // LANES, LANES, LANES), lambda b, p: (b, p, 0, 0, 0)),
                  pl.BlockSpec((1, S, hp * dv), lambda b, p: (b, 0, p)),
                  per_pair((C, C)), per_pair((1, C)), per_pair((C, LANES)), per_pair((1, dv)),
                  per_pair((1, dv))],
        out_specs=pl.BlockSpec((1, S, hp * dv), lambda b, p: (b, 0, p)),
        out_shape=jax.ShapeDtypeStruct((B, S, H * dv), BF16),
        compiler_params=pltpu.CompilerParams(dimension_semantics=("parallel", "parallel"),
                                             vmem_limit_bytes=VMEM_LIMIT),
        name="retention",
    )(qk3, kT, v.reshape(B, S, H * dv), decay, zeta, xi, gch, gn_gain.reshape(H, 1, dv))


def _merge_body(x_ref, a_ref, r_ref, gr_ref, ga_ref, gb_ref, wa_ref, wr_ref, wo_ref, o_ref):
    g = gr_ref[...].astype(F32)
    r = (g * jax.nn.sigmoid(g) * r_ref[...].astype(F32)).astype(BF16)
    ya = jnp.dot(a_ref[...], wa_ref[...], preferred_element_type=F32)
    yr = jnp.dot(r, wr_ref[...], preferred_element_type=F32)
    mixed = jax.nn.sigmoid(ga_ref[...].astype(F32)) * ya + jax.nn.sigmoid(gb_ref[...].astype(F32)) * yr
    o_ref[...] = x_ref[...] + jnp.dot(mixed.astype(BF16), wo_ref[...], preferred_element_type=F32)


def _merge(x2d, a, r, gates, wa, wr, wo, *, tm=512):
    T, D = x2d.shape
    col = lambda j: pl.BlockSpec((tm, D), lambda i: (i, j))
    return pl.pallas_call(
        _merge_body,
        grid=(T // tm,),
        in_specs=[col(0), col(0), col(0), col(0), col(1), col(2),
                  _resident(wa.shape), _resident(wr.shape), _resident(wo.shape)],
        out_specs=col(0),
        out_shape=jax.ShapeDtypeStruct((T, D), F32),
        compiler_params=pltpu.CompilerParams(dimension_semantics=("parallel",),
                                             vmem_limit_bytes=VMEM_LIMIT),
        name="merge",
    )(x2d, a, r, gates, gates, gates, wa, wr, wo)


def _layer(x, p, final_norm):
    B, S, D = x.shape
    T = B * S
    G, Hg, dh = NSA_GROUPS, NSA_HG, NSA_DH
    H, dk, dv = RET_HEADS, RET_DK, RET_DV
    bf = lambda w: w.astype(BF16)

    x1 = _ffn(x.reshape(T, D), p["ffn1_norm"], bf(p["ffn1_w_gate"]), bf(p["ffn1_w_up"]),
              bf(p["ffn1_w_down"]), final_norm, final=False)

    w = p["w_in"]
    widths = (G * Hg * dh,) + (G * dh,) * 6 + (3 * G * Hg, H * dk, H * dk, H * dv, H * dv, 2 * D)
    offs = np.concatenate([[0], np.cumsum(widths)])
    seg = lambda i: w[:, int(offs[i]):int(offs[i + 1])]
    (q_a, kc_a, vc_a, ks_a, vs_a, kw_a, vw_a, g_a, q_r, k_r, v_r, g_r, g_m) = [seg(i) for i in range(13)]
    gate_pad = MXU_N - g_a.shape[1]
    wa = bf(jnp.concatenate([q_a * (dh ** -0.5 * LOG2E), ks_a, kw_a], axis=1))
    wb = bf(jnp.concatenate([q_r * dk ** -0.5, k_r], axis=1))
    wc = bf(jnp.concatenate([v_r, kc_a, vc_a, vs_a, vw_a], axis=1))
    wd = bf(jnp.concatenate([g_r, g_m, g_a, jnp.zeros((D, gate_pad), F32)], axis=1))
    pos = jnp.arange(S)
    tab_a = _rope_tables(pos, ROPE_DIM, dh, ROPE_THETA)
    tab_b = _rope_tables(pos, dk, dk, RET_ROPE_THETA)
    qk_n, qk_r, krT, v_ret, kc_tok, vc_tok, vsT, vwT, gates = _proj(
        x1, p["mix_norm"], wa, wb, wc, wd, tab_a, tab_b, B, S)

    chunk = lambda t: t.reshape(B, S // CMP_STRIDE, CMP_STRIDE * G * dh)
    kw1, kw2, pe = _compress_weights(p["cmp_k_w1"], p["cmp_k_w2"], p["cmp_pos_emb"])
    vw1, vw2, _ = _compress_weights(p["cmp_v_w1"], p["cmp_v_w2"], p["cmp_pos_emb"])
    cmp_end = jnp.arange(S // CMP_STRIDE) * CMP_STRIDE + (CMP_BLOCK - 1)
    kc, vcT = _compress(chunk(kc_tok), chunk(vc_tok), pe, kw1, kw2, vw1, vw2,
                        _rope_tables(cmp_end, ROPE_DIM, dh, ROPE_THETA))
    c2sT, et = _nsa_constants(S)
    a_out = _nsa(qk_n, gates, kc, vcT, vsT, vwT, c2sT, et, B, S)
    r_out = _retention(qk_r, krT, v_ret, p["ret_gn_gain"], B, S).reshape(T, H * dv)

    x2 = _merge(x1, a_out, r_out, gates, bf(p["w_branch_nsa"]), bf(p["w_branch_ret"]), bf(p["w_out"]))
    x3 = _ffn(x2, p["ffn2_norm"], bf(p["ffn2_w_gate"]), bf(p["ffn2_w_up"]), bf(p["ffn2_w_down"]),
              final_norm, final=True)
    return x3.reshape(B, S, D)


def kernel(x, ffn1_norm, ffn1_w_gate, ffn1_w_up, ffn1_w_down, mix_norm, w_in, cmp_pos_emb, cmp_k_w1,
           cmp_k_w2, cmp_v_w1, cmp_v_w2, ret_gn_gain, w_branch_nsa, w_branch_ret, w_out, ffn2_norm,
           ffn2_w_gate, ffn2_w_up, ffn2_w_down, final_norm):
    assert ffn1_norm.shape[0] == 1, "single-layer stack"
    names = ("ffn1_norm", "ffn1_w_gate", "ffn1_w_up", "ffn1_w_down", "mix_norm", "w_in", "cmp_pos_emb",
             "cmp_k_w1", "cmp_k_w2", "cmp_v_w1", "cmp_v_w2", "ret_gn_gain", "w_branch_nsa",
             "w_branch_ret", "w_out", "ffn2_norm", "ffn2_w_gate", "ffn2_w_up", "ffn2_w_down")
    vals = (ffn1_norm, ffn1_w_gate, ffn1_w_up, ffn1_w_down, mix_norm, w_in, cmp_pos_emb, cmp_k_w1,
            cmp_k_w2, cmp_v_w1, cmp_v_w2, ret_gn_gain, w_branch_nsa, w_branch_ret, w_out, ffn2_norm,
            ffn2_w_gate, ffn2_w_up, ffn2_w_down)
    p = {n: v[0] for n, v in zip(names, vals)}
    return _layer(x, p, final_norm)
```

```python
import functools
import math

import numpy as np
import jax
import jax.numpy as jnp
from jax import lax
from jax.experimental import pallas as pl
from jax.experimental.pallas import tpu as pltpu

F32 = jnp.float32
BF16 = jnp.bfloat16

NSA_HEADS = 16
NSA_GROUPS = 2
NSA_HG = NSA_HEADS // NSA_GROUPS
NSA_DH = 64
CMP_BLOCK = 32
CMP_STRIDE = 16
SLC_BLOCK = 64
SLC_TOPN = 16
WINDOW = 512
Q_BLOCK = 128
ROPE_THETA = 500000.0
ROPE_DIM = NSA_DH // 4
FORCED_SCORE = 1.0e4
RET_HEADS = 8
RET_DK = 64
RET_DV = 128
RET_CHUNK = 128
RET_ROPE_THETA = 10000.0
EPS = 1e-6
GN_EPS = 1e-5
NEG_INF = -1e30
LOG2E = math.log2(math.e)

LANES = 128
BF16_ROWS = 16
MXU_N = 256
VMEM_LIMIT = 56 * 1024 * 1024

NT_DIMS = (((1,), (1,)), ((), ()))


def _rms(x, g, eps=EPS):
    return x * lax.rsqrt(jnp.mean(x * x, axis=-1, keepdims=True) + eps) * g


def _resident(shape):
    nd = len(shape)
    return pl.BlockSpec(shape, lambda *_: (0,) * nd, pipeline_mode=pl.Buffered(1))


def _ffn_body(x_ref, g_ref, wg_ref, wu_ref, wd_ref, fg_ref, o_ref, *, cuts, final):
    x = x_ref[...]
    h = _rms(x, g_ref[...]).astype(BF16)
    acc = jnp.zeros(x.shape, F32)
    for lo, hi in zip(cuts[:-1], cuts[1:]):
        sl = slice(lo, hi)
        g = jnp.dot(h, wg_ref[:, sl], preferred_element_type=F32)
        u = jnp.dot(h, wu_ref[:, sl], preferred_element_type=F32)
        a = (g * jax.nn.sigmoid(g) * u).astype(BF16)
        acc = acc + jnp.dot(a, wd_ref[sl, :], preferred_element_type=F32)
    y = x + 0.5 * acc
    if final:
        y = _rms(y, fg_ref[...])
    o_ref[...] = y


def _ffn(x2d, gain, wg, wu, wd, final_gain, *, final, tm=512):
    T, D = x2d.shape
    F = wg.shape[1]
    assert F % MXU_N == 0
    n_tiles = F // MXU_N
    cuts = (0, (n_tiles + 1) // 2 * MXU_N, F)
    tok = pl.BlockSpec((tm, D), lambda i: (i, 0))
    return pl.pallas_call(
        functools.partial(_ffn_body, cuts=cuts, final=final),
        grid=(T // tm,),
        in_specs=[tok, _resident((1, D)), _resident((D, F)), _resident((D, F)), _resident((F, D)),
                  _resident((1, D))],
        out_specs=tok,
        out_shape=jax.ShapeDtypeStruct((T, D), F32),
        compiler_params=pltpu.CompilerParams(dimension_semantics=("parallel",),
                                             vmem_limit_bytes=VMEM_LIMIT),
        name="ffn_final" if final else "ffn",
    )(x2d, gain.reshape(1, D), wg, wu, wd, final_gain.reshape(1, D))


def _rope_slab(y, tab_ref, shift):
    return (y * tab_ref[0] + pltpu.roll(y, LANES - shift, 1) * tab_ref[1]
            + pltpu.roll(y, shift, 1) * tab_ref[2])


def _proj_body(x_ref, g_ref, wa_ref, wb_ref, wc_ref, wd_ref, ta_ref, tb_ref,
               oa_ref, ob_ref, okrT_ref, ov_ref, okc_ref, ovc_ref, ovsT_ref, ovwT_ref, od_ref):
    h = _rms(x_ref[...], g_ref[...]).astype(BF16)
    tm = h.shape[0]
    nv = ov_ref.shape[1] // LANES
    nkr = okrT_ref.shape[1]

    def slabs(w_ref):
        for c in range(w_ref.shape[1] // MXU_N):
            y = jnp.dot(h, w_ref[:, c * MXU_N:(c + 1) * MXU_N], preferred_element_type=F32)
            for s in range(MXU_N // LANES):
                yield c * (MXU_N // LANES) + s, y[:, s * LANES:(s + 1) * LANES]

    def put(o_ref, i, ys):
        o_ref[:, i * LANES:(i + 1) * LANES] = ys.astype(o_ref.dtype)

    def put_t(o_ref, ys, *lead):
        for r in range(tm // LANES):
            o_ref[(0,) + lead + (r,)] = ys[r * LANES:(r + 1) * LANES, :].T.astype(o_ref.dtype)

    for i, ys in slabs(wa_ref):
        put(oa_ref, i, _rope_slab(ys, ta_ref, ROPE_DIM // 2))
    nb = wb_ref.shape[1] // LANES
    for i, ys in slabs(wb_ref):
        ys = _rope_slab(ys, tb_ref, RET_DK // 2)
        if i < nb - nkr:
            put(ob_ref, i, ys)
        else:
            put_t(okrT_ref, ys, i - (nb - nkr))
    for i, ys in slabs(wc_ref):
        if i < nv:
            put(ov_ref, i, ys)
        elif i == nv:
            put(okc_ref, 0, ys)
        elif i == nv + 1:
            put(ovc_ref, 0, ys)
        elif i == nv + 2:
            put_t(ovsT_ref, ys)
        else:
            put_t(ovwT_ref, ys)
    for i, ys in slabs(wd_ref):
        put(od_ref, i, ys)


def _proj(x2d, gain, wa, wb, wc, wd, tab_a, tab_b, B, S, *, tm=512):
    T, D = x2d.shape
    spt = S // tm
    nv = wc.shape[1] - 4 * LANES
    tok = lambda n: pl.BlockSpec((tm, n), lambda i: (i, 0))
    tab = pl.BlockSpec((3, tm, LANES), lambda i: (0, i % spt, 0))
    tr = pl.BlockSpec((1, tm // LANES, LANES, LANES), lambda i: (i // spt, i % spt, 0, 0))
    nkr = RET_HEADS * RET_DK // LANES
    tr_kr = pl.BlockSpec((1, nkr, tm // LANES, LANES, LANES), lambda i: (i // spt, 0, i % spt, 0, 0))
    return pl.pallas_call(
        _proj_body,
        grid=(T // tm,),
        in_specs=[tok(D), _resident((1, D)), _resident(wa.shape), _resident(wb.shape),
                  _resident(wc.shape), _resident(wd.shape), tab, tab],
        out_specs=[tok(wa.shape[1]), tok(wb.shape[1] - nkr * LANES), tr_kr, tok(nv), tok(LANES), tok(LANES),
                   tr, tr, tok(wd.shape[1])],
        out_shape=[jax.ShapeDtypeStruct((T, wa.shape[1]), BF16),
                   jax.ShapeDtypeStruct((T, wb.shape[1] - nkr * LANES), BF16),
                   jax.ShapeDtypeStruct((B, nkr, S // LANES, LANES, LANES), BF16),
                   jax.ShapeDtypeStruct((T, nv), BF16),
                   jax.ShapeDtypeStruct((T, LANES), BF16),
                   jax.ShapeDtypeStruct((T, LANES), BF16),
                   jax.ShapeDtypeStruct((B, S // LANES, LANES, LANES), BF16),
                   jax.ShapeDtypeStruct((B, S // LANES, LANES, LANES), BF16),
                   jax.ShapeDtypeStruct((T, wd.shape[1]), BF16)],
        compiler_params=pltpu.CompilerParams(dimension_semantics=("parallel",),
                                             vmem_limit_bytes=VMEM_LIMIT),
        name="proj",
    )(x2d, gain.reshape(1, D), wa, wb, wc, wd, tab_a, tab_b)


def _rope_tables(pos, rot_dim, head_dim, theta):
    half = rot_dim // 2
    freqs = theta ** (-(jnp.arange(half, dtype=F32) * 2.0 / rot_dim))
    ang = pos.astype(F32)[:, None] * freqs[None, :]
    cos, sin = jnp.cos(ang), jnp.sin(ang)
    n = pos.shape[0]
    rest = head_dim - rot_dim
    cos_h = jnp.concatenate([cos, cos, jnp.ones((n, rest), F32)], axis=-1)
    sl_h = jnp.concatenate([-sin, jnp.zeros((n, half + rest), F32)], axis=-1)
    sr_h = jnp.concatenate([jnp.zeros((n, half), F32), sin, jnp.zeros((n, rest), F32)], axis=-1)
    rep = LANES // head_dim
    return jnp.stack([jnp.tile(t, (1, rep)) for t in (cos_h, sl_h, sr_h)])


def _gelu_tanh(x):
    return 0.5 * x * (1.0 + jnp.tanh(math.sqrt(2.0 / math.pi) * (x + 0.044715 * (x * x * x))))


def _compress_body(kt_ref, vt_ref, pe_ref, kw1_ref, kw2_ref, vw1_ref, vw2_ref, tab_ref,
                   kc_ref, vcT_ref):
    pe = pe_ref[...]
    n = kt_ref.shape[1]

    def mlp(tok_ref, w1_ref, w2_ref):
        c = tok_ref[0].astype(F32)
        lo = (c + pe[0:1]).astype(BF16)
        hi = (c + pe[1:2]).astype(BF16)
        out = jnp.zeros((n, LANES), F32)
        for g in range(NSA_GROUPS):
            top = jnp.dot(lo, w1_ref[g, 0], preferred_element_type=F32)
            bot = jnp.dot(hi, w1_ref[g, 1], preferred_element_type=F32)
            hid = _gelu_tanh(top + pltpu.roll(bot, n - 1, 0))
            out = out + jnp.dot(hid.astype(BF16), w2_ref[g], preferred_element_type=F32)
        return out

    kc_ref[0] = _rope_slab(mlp(kt_ref, kw1_ref, kw2_ref), tab_ref, ROPE_DIM // 2).astype(kc_ref.dtype)
    vc = mlp(vt_ref, vw1_ref, vw2_ref)
    for r in range(n // LANES):
        vcT_ref[0, :, r * LANES:(r + 1) * LANES] = vc[r * LANES:(r + 1) * LANES, :].T.astype(vcT_ref.dtype)


def _compress(k_tok, v_tok, pe, kw1, kw2, vw1, vw2, tab):
    B, n, w = k_tok.shape
    tok = pl.BlockSpec((1, n, w), lambda b: (b, 0, 0))
    return pl.pallas_call(
        _compress_body,
        grid=(B,),
        in_specs=[tok, tok, _resident(pe.shape), _resident(kw1.shape), _resident(kw2.shape),
                  _resident(vw1.shape), _resident(vw2.shape), _resident(tab.shape)],
        out_specs=[pl.BlockSpec((1, n, LANES), lambda b: (b, 0, 0)),
                   pl.BlockSpec((1, LANES, n), lambda b: (b, 0, 0))],
        out_shape=[jax.ShapeDtypeStruct((B, n, LANES), BF16), jax.ShapeDtypeStruct((B, LANES, n), BF16)],
        compiler_params=pltpu.CompilerParams(dimension_semantics=("parallel",),
                                             vmem_limit_bytes=VMEM_LIMIT),
        name="compress",
    )(k_tok, v_tok, pe, kw1, kw2, vw1, vw2, tab)


def _compress_weights(w1, w2, pe):
    G, dh = NSA_GROUPS, NSA_DH
    hid = w1.shape[1]
    halves = w1.reshape(2, CMP_STRIDE, dh, hid)
    w1e = jnp.zeros((G, 2, CMP_STRIDE, G, dh, hid), F32)
    w2e = jnp.zeros((G, hid, G, dh), F32)
    for g in range(G):
        w1e = w1e.at[g, :, :, g].set(halves)
        w2e = w2e.at[g, :, g].set(w2)
    pe_e = jnp.broadcast_to(pe.reshape(2, CMP_STRIDE, 1, dh), (2, CMP_STRIDE, G, dh))
    return (w1e.reshape(G, 2, CMP_STRIDE * G * dh, hid).astype(BF16), w2e.reshape(G, hid, G * dh).astype(BF16),
            pe_e.reshape(2, CMP_STRIDE * G * dh))


def _nsa_body(q_ref, gate_ref, kc_ref, vcT_ref, ks_ref, vsT_ref, kw_ref, vwT_ref, c2sT_ref, et_ref,
              o_ref, qT_scr, gT_scr, m_scr, acc_scr, rank_scr, *, kt, top_n, n_slc, hg, dh):
    qb = q_ref.shape[0]
    nh = q_ref.shape[1] // dh
    groups = nh // hg
    ncp = kc_ref.shape[1]
    c = pl.program_id(1)
    q0 = c * qb
    ones_rows = jnp.ones((BF16_ROWS, 1), BF16)
    grp_rows = lambda g: slice(g * dh, (g + 1) * dh)

    def v_tiles(vT_ref, k0, n):
        t = vT_ref[0, pl.ds(k0 // LANES, n // LANES)]
        vT = jnp.concatenate([t[i] for i in range(n // LANES)], axis=1)
        ones = jnp.broadcast_to(ones_rows, (BF16_ROWS, n))
        return [jnp.concatenate([vT[grp_rows(g)], ones], axis=0) for g in range(groups)]

    def tq(n):
        return q0 + lax.broadcasted_iota(jnp.int32, (n, qb), 1)

    def krow(n, k0=0):
        return k0 + lax.broadcasted_iota(jnp.int32, (n, qb), 0)

    qT_scr[...] = jnp.zeros(qT_scr.shape, BF16)
    for r in range(qb // LANES):
        rows = slice(r * LANES, (r + 1) * LANES)
        for j in range(nh // 2):
            t = q_ref[rows, j * LANES:(j + 1) * LANES].astype(F32).T.astype(BF16)
            for half in range(2):
                h = 2 * j + half
                qT_scr[grp_rows(h // hg), h * qb + r * LANES:h * qb + (r + 1) * LANES] = (
                    t[half * dh:(half + 1) * dh])
        gT_scr[:, rows] = jax.nn.sigmoid(gate_ref[rows, :LANES].astype(F32).T)

    qT = qT_scr[:LANES, :]

    span = WINDOW + LANES

    def window_tile(r):
        t0 = q0 + r * LANES
        w0 = pl.multiple_of(jnp.maximum(t0 - WINDOW, 0), LANES)
        dist = (t0 + lax.broadcasted_iota(jnp.int32, (span, LANES), 1)
                - (w0 + lax.broadcasted_iota(jnp.int32, (span, LANES), 0)))
        bias_w = jnp.where((dist >= 0) & (dist < WINDOW), 0.0, NEG_INF)
        bias_w = jnp.concatenate([bias_w, bias_w], axis=1)
        qT_r = jnp.concatenate([qT[:, h * qb + r * LANES:h * qb + (r + 1) * LANES] for h in range(nh)], axis=1)
        s_w = jnp.dot(kw_ref[0, pl.ds(w0, span), :], qT_r, preferred_element_type=F32)
        vwT = v_tiles(vwT_ref, w0, span)
        outs_r = []
        for hp in range(nh // 2):
            sh = s_w[:, hp * 2 * LANES:(hp + 1) * 2 * LANES] + bias_w
            p = jnp.exp2(sh - jnp.max(sh, axis=0, keepdims=True)).astype(BF16)
            acc_w = jnp.dot(vwT[2 * hp // hg], p, preferred_element_type=F32)
            ow = acc_w[:dh] * (1.0 / acc_w[dh:dh + 1])
            outs_r += [ow[:, :LANES], ow[:, LANES:]]
        return outs_r

    n_lt = qb // LANES
    o_w = [window_tile(r) for r in range(n_lt // 2)]

    s_c = jnp.dot(kc_ref[0], qT, preferred_element_type=F32)
    cend = krow(ncp) * CMP_STRIDE + (CMP_BLOCK - 1)
    bias_c = jnp.where(cend <= tq(ncp), 0.0, NEG_INF)
    bias_c = jnp.concatenate([bias_c, bias_c], axis=1)
    nb = dh
    vcT = vcT_ref[0]
    lhs_c = [jnp.concatenate([vcT[grp_rows(g)], jnp.broadcast_to(ones_rows, (BF16_ROWS, ncp)),
                              c2sT_ref[:nb, :]], axis=0) for g in range(groups)]
    imps = [jnp.zeros((nb, qb), F32) for _ in range(groups)]
    o_c = []
    for hp in range(nh // 2):
        g = 2 * hp // hg
        sh = s_c[:, hp * 2 * qb:(hp + 1) * 2 * qb] + bias_c
        e = jnp.exp2(sh - jnp.max(sh, axis=0, keepdims=True)).astype(BF16)
        r = jnp.dot(lhs_c[g], e, preferred_element_type=F32)
        inv_l = 1.0 / r[dh:dh + 1]
        o_c.append(r[:dh] * inv_l)
        w = r[dh + BF16_ROWS:] * inv_l
        imps[g] = imps[g] + w[:, :qb] + w[:, qb:]
    o_c = jnp.concatenate(o_c, axis=1)
    sb = krow(nb)
    cur = jnp.right_shift(tq(nb), int(math.log2(SLC_BLOCK)))
    forced = (sb == 0) | (sb == cur) | (sb == cur - 1)
    future = sb > cur
    imps = [jnp.where(forced, FORCED_SCORE, jnp.where(future, -FORCED_SCORE, imp)) for imp in imps]
    sub = 8
    rank_scr[...] = jnp.zeros(rank_scr.shape, jnp.int32)
    last_block = (q0 + qb - 1) // SLC_BLOCK
    sb_l = sb[:, :LANES]
    for grp in range(n_slc // sub):
        @pl.when(grp * sub <= last_block)
        def _():
            for g in range(groups):
                for lc in range(qb // LANES):
                    imp_l = imps[g][:, lc * LANES:(lc + 1) * LANES]
                    chunks = [imp_l[r * sub:(r + 1) * sub] for r in range(nb // sub)]
                    ranks = [jnp.zeros((sub, LANES), jnp.int32) for _ in chunks]
                    for sp in range(grp * sub, (grp + 1) * sub):
                        row = imp_l[sp:sp + 1, :]
                        for r, blk in enumerate(chunks):
                            if r * sub > sp:
                                beats = row >= blk
                            elif (r + 1) * sub - 1 <= sp:
                                beats = row > blk
                            else:
                                beats = (row > blk) | ((row == blk) & (sb_l[r * sub:(r + 1) * sub] > sp))
                            ranks[r] = ranks[r] + jnp.where(beats, 1, 0)
                    rank_scr[g, :, lc * LANES:(lc + 1) * LANES] += jnp.concatenate(ranks, axis=0)
    for g in range(groups):
        selneg = jnp.where((rank_scr[g] < top_n) & jnp.logical_not(future), 0.0, NEG_INF).astype(BF16)
        for h in range(g * hg, (g + 1) * hg):
            qT_scr[LANES:LANES + nb, h * qb:(h + 1) * qb] = selneg

    qT_aug = qT_scr[...]

    def scores(j):
        k0 = pl.multiple_of(j * kt, kt)
        ka = jnp.concatenate([ks_ref[0, pl.ds(k0, kt), :], et_ref[pl.ds(k0, kt), :]], axis=1)
        return jnp.dot(ka, qT_aug, preferred_element_type=F32)

    def fold(s, j, causal):
        vT = v_tiles(vsT_ref, pl.multiple_of(j * kt, kt), kt)
        if causal:
            bias = jnp.where(krow(kt, j * kt) <= tq(kt), 0.0, NEG_INF)
            bias = jnp.concatenate([bias, bias], axis=1)
        for hp in range(nh // 2):
            sl = slice(hp * 2 * qb, (hp + 1) * 2 * qb)
            sh = s[:, sl] + bias if causal else s[:, sl]
            m_old = m_scr[:, sl]
            m_new = jnp.maximum(m_old, jnp.max(sh, axis=0, keepdims=True))
            pv = jnp.dot(vT[2 * hp // hg], jnp.exp2(sh - m_new).astype(BF16), preferred_element_type=F32)
            acc_scr[:, sl] = jnp.exp2(m_old - m_new) * acc_scr[:, sl] + pv
            m_scr[:, sl] = m_new

    m_scr[...] = jnp.full(m_scr.shape, NEG_INF, F32)
    acc_scr[...] = jnp.zeros(acc_scr.shape, F32)

    def run_tiles(j0, n):
        ss = [scores(j0 + t) for t in range(n)]
        for t in range(n):
            fold(ss[t], j0 + t, False)

    def main_pair(i, carry):
        run_tiles(2 * i, 2)
        return carry

    lax.fori_loop(0, c // 2, main_pair, 0)

    @pl.when(c % 2 == 1)
    def _():
        fold(scores(c - 1), c - 1, False)

    fold(scores(c), c, True)
    o_s = acc_scr[:dh, :] * (1.0 / acc_scr[dh:dh + 1, :])

    o_w += [window_tile(r) for r in range(n_lt // 2, n_lt)]
    o_w = jnp.concatenate([o_w[r][h] for h in range(nh) for r in range(n_lt)], axis=1)

    has_cmp = (tq(1) >= CMP_BLOCK - 1).astype(F32)
    gts = gT_scr[:3 * nh, :]
    outs = []
    for h in range(nh):
        sl = slice(h * qb, (h + 1) * qb)
        outs.append(gts[3 * h:3 * h + 1] * has_cmp * o_c[:, sl] + gts[3 * h + 1:3 * h + 2] * o_s[:, sl]
                    + gts[3 * h + 2:3 * h + 3] * o_w[:, sl])
    for j in range(nh // 2):
        pair = jnp.concatenate([outs[2 * j], outs[2 * j + 1]], axis=0)
        for r in range(qb // LANES):
            rows = slice(r * LANES, (r + 1) * LANES)
            o_ref[rows, j * LANES:(j + 1) * LANES] = pair[:, rows].T.astype(o_ref.dtype)


def _nsa(qk, gates, kc, vcT, vsT, vwT, c2sT, et, B, S, *, qb=256):
    G, Hg, dh = NSA_GROUPS, NSA_HG, NSA_DH
    T = B * S
    nq = S // qb
    n_slc = S // SLC_BLOCK
    kt = qb
    assert n_slc <= dh and 2 * dh == LANES and qb % LANES == 0 and S % qb == 0 and WINDOW % qb == 0
    assert S >= WINDOW + qb
    qw = G * Hg * dh
    assert 3 * G * Hg <= LANES and G * dh == LANES
    ncp = kc.shape[1]
    k_col = qw // LANES
    g_col = gates.shape[1] // MXU_N - 1
    seq_k = lambda col: pl.BlockSpec((1, S, LANES), lambda b, c: (b, 0, col))
    seq_vT = pl.BlockSpec((1, S // LANES, LANES, LANES), lambda b, c: (b, 0, 0, 0))
    qk3 = qk.reshape(B, S, qk.shape[1])
    cols = G * Hg * qb
    return pl.pallas_call(
        functools.partial(_nsa_body, kt=kt, top_n=min(SLC_TOPN, n_slc), n_slc=n_slc, hg=Hg, dh=dh),
        grid=(B, nq),
        in_specs=[pl.BlockSpec((qb, qw), lambda b, c: (b * nq + c, 0)),
                  pl.BlockSpec((qb, MXU_N), lambda b, c: (b * nq + c, g_col)),
                  pl.BlockSpec((1, ncp, LANES), lambda b, c: (b, 0, 0)),
                  pl.BlockSpec((1, LANES, ncp), lambda b, c: (b, 0, 0)),
                  seq_k(k_col), seq_vT, seq_k(k_col + 1), seq_vT,
                  _resident(c2sT.shape), _resident(et.shape)],
        out_specs=pl.BlockSpec((qb, qw), lambda b, c: (b * nq + c, 0)),
        out_shape=jax.ShapeDtypeStruct((T, qw), BF16),
        scratch_shapes=[pltpu.VMEM((2 * LANES, cols), BF16), pltpu.VMEM((LANES, qb), F32),
                        pltpu.VMEM((1, cols), F32), pltpu.VMEM((dh + BF16_ROWS, cols), F32),
                        pltpu.VMEM((G, dh, qb), jnp.int32)],
        compiler_params=pltpu.CompilerParams(dimension_semantics=("parallel", "arbitrary"),
                                             vmem_limit_bytes=VMEM_LIMIT),
        name="nsa",
    )(qk, gates, kc, vcT, qk3, vsT, qk3, vwT, c2sT, et)


def _nsa_constants(S):
    ncp = S // CMP_STRIDE
    n_slc = S // SLC_BLOCK
    cmp_start = np.arange(ncp) * CMP_STRIDE
    s_start = np.arange(n_slc) * SLC_BLOCK
    overlap = np.clip(np.minimum(cmp_start[:, None] + CMP_BLOCK, s_start[None, :] + SLC_BLOCK)
                      - np.maximum(cmp_start[:, None], s_start[None, :]), 0, None)
    c2sT = np.zeros((LANES, ncp), np.float32)
    c2sT[:n_slc, :] = overlap.T.astype(np.float32) / CMP_BLOCK
    c2sT[:, ncp - 1] = 0.0
    et = np.zeros((S, LANES), np.float32)
    et[np.arange(S), np.arange(S) // SLC_BLOCK] = 1.0
    return jnp.asarray(c2sT, BF16), jnp.asarray(et, BF16)


def _ret_body(q_ref, kT_ref, v_ref, decay_ref, zeta_ref, xi_ref, gch_ref, gn_ref, o_ref, *, C, dk):
    S = q_ref.shape[1]
    dv = gn_ref.shape[2]
    heads = LANES // dk
    N = S // C
    units = [(n, a) for n in range(N) for a in range(heads)]
    rows = lambda n: slice(n * C, (n + 1) * C)
    val = lambda n, a: v_ref[0, rows(n), a * dv:(a + 1) * dv]
    qp = lambda n: q_ref[0, rows(n), :]

    def own_rows(x, a):
        return jnp.concatenate([x if b == a else jnp.zeros_like(x) for b in range(heads)], axis=0)

    kT = {(n, a): kT_ref[0, 0, n, a * dk:(a + 1) * dk, :] for n, a in units}
    inner = {u: jnp.dot(qp(u[0]), own_rows(kT[u], u[1]), preferred_element_type=F32) for u in units}
    inner = {u: (inner[u] * decay_ref[u[1]]).astype(BF16) for u in units}
    o = {u: jnp.dot(inner[u], val(*u), preferred_element_type=F32) for u in units}
    kv = {(n, a): jnp.dot((kT[(n, a)].astype(F32) * zeta_ref[a]).astype(BF16), val(n, a),
                          preferred_element_type=F32) for n, a in units}
    state = {}
    for a in range(heads):
        R = jnp.zeros((dk, dv), F32)
        for n in range(N):
            state[(n, a)] = own_rows(R.astype(BF16), a)
            R = gch_ref[a] * R + kv[(n, a)]
    o = {u: o[u] + xi_ref[u[1]] * jnp.dot(qp(u[0]), state[u], preferred_element_type=F32) for u in units}
    mu = {u: jnp.mean(o[u], axis=-1, keepdims=True) for u in units}
    d = {u: o[u] - mu[u] for u in units}
    var = {u: jnp.mean(d[u] * d[u], axis=-1, keepdims=True) for u in units}
    for n, a in units:
        o_ref[0, rows(n), a * dv:(a + 1) * dv] = (d[(n, a)] * lax.rsqrt(var[(n, a)] + GN_EPS)
                                                  * gn_ref[a]).astype(o_ref.dtype)


def _retention(qk, kT, v, gn_gain, B, S, *, C=RET_CHUNK):
    H, dk, dv = RET_HEADS, RET_DK, RET_DV
    hp = LANES // dk
    assert C == LANES and dv == LANES
    log_g = jnp.log(1.0 - 2.0 ** (-5.0 - jnp.arange(H, dtype=F32)))
    i = jnp.arange(C, dtype=F32)
    diff = i[:, None] - i[None, :]
    decay = jnp.where(diff >= 0, jnp.exp(jnp.maximum(diff, 0.0) * log_g[:, None, None]), 0.0)
    zeta = jnp.exp((C - 1.0 - i)[None, :] * log_g[:, None]).reshape(H, 1, C)
    xi = jnp.broadcast_to(jnp.exp((i + 1.0)[None, :] * log_g[:, None])[:, :, None], (H, C, LANES))
    gch = jnp.broadcast_to(jnp.exp(C * log_g)[:, None, None], (H, 1, dv))
    per_pair = lambda shape: pl.BlockSpec((hp,) + shape, lambda b, p: (p, 0, 0))
    qk3 = qk.reshape(B, S, qk.shape[1])
    return pl.pallas_call(
        functools.partial(_ret_body, C=C, dk=dk),
        grid=(B, H // hp),
        in_specs=[pl.BlockSpec((1, S, LANES), lambda b, p: (b, 0, p)),
                  pl.BlockSpec((1, 1, S // LANES, LANES, LANES), lambda b, p: (b, p, 0, 0, 0)),
                  pl.BlockSpec((1, S, hp * dv), lambda b, p: (b, 0, p)),
                  per_pair((C, C)), per_pair((1, C)), per_pair((C, LANES)), per_pair((1, dv)),
                  per_pair((1, dv))],
        out_specs=pl.BlockSpec((1, S, hp * dv), lambda b, p: (b, 0, p)),
        out_shape=jax.ShapeDtypeStruct((B, S, H * dv), BF16),
        compiler_params=pltpu.CompilerParams(dimension_semantics=("parallel", "parallel"),
                                             vmem_limit_bytes=VMEM_LIMIT),
        name="retention",
    )(qk3, kT, v.reshape(B, S, H * dv), decay, zeta, xi, gch, gn_gain.reshape(H, 1, dv))


def _merge_body(x_ref, a_ref, r_ref, gr_ref, ga_ref, gb_ref, wa_ref, wr_ref, wo_ref, o_ref):
    g = gr_ref[...].astype(F32)
    r = (g * jax.nn.sigmoid(g) * r_ref[...].astype(F32)).astype(BF16)
    ya = jnp.dot(a_ref[...], wa_ref[...], preferred_element_type=F32)
    yr = jnp.dot(r, wr_ref[...], preferred_element_type=F32)
    mixed = jax.nn.sigmoid(ga_ref[...].astype(F32)) * ya + jax.nn.sigmoid(gb_ref[...].astype(F32)) * yr
    o_ref[...] = x_ref[...] + jnp.dot(mixed.astype(BF16), wo_ref[...], preferred_element_type=F32)


def _merge(x2d, a, r, gates, wa, wr, wo, *, tm=512):
    T, D = x2d.shape
    col = lambda j: pl.BlockSpec((tm, D), lambda i: (i, j))
    return pl.pallas_call(
        _merge_body,
        grid=(T // tm,),
        in_specs=[col(0), col(0), col(0), col(0), col(1), col(2),
                  _resident(wa.shape), _resident(wr.shape), _resident(wo.shape)],
        out_specs=col(0),
        out_shape=jax.ShapeDtypeStruct((T, D), F32),
        compiler_params=pltpu.CompilerParams(dimension_semantics=("parallel",),
                                             vmem_limit_bytes=VMEM_LIMIT),
        name="merge",
    )(x2d, a, r, gates, gates, gates, wa, wr, wo)


def _layer(x, p, final_norm):
    B, S, D = x.shape
    T = B * S
    G, Hg, dh = NSA_GROUPS, NSA_HG, NSA_DH
    H, dk, dv = RET_HEADS, RET_DK, RET_DV
    bf = lambda w: w.astype(BF16)

    x1 = _ffn(x.reshape(T, D), p["ffn1_norm"], bf(p["ffn1_w_gate"]), bf(p["ffn1_w_up"]),
              bf(p["ffn1_w_down"]), final_norm, final=False)

    w = p["w_in"]
    widths = (G * Hg * dh,) + (G * dh,) * 6 + (3 * G * Hg, H * dk, H * dk, H * dv, H * dv, 2 * D)
    offs = np.concatenate([[0], np.cumsum(widths)])
    seg = lambda i: w[:, int(offs[i]):int(offs[i + 1])]
    (q_a, kc_a, vc_a, ks_a, vs_a, kw_a, vw_a, g_a, q_r, k_r, v_r, g_r, g_m) = [seg(i) for i in range(13)]
    gate_pad = MXU_N - g_a.shape[1]
    wa = bf(jnp.concatenate([q_a * (dh ** -0.5 * LOG2E), ks_a, kw_a], axis=1))
    wb = bf(jnp.concatenate([q_r * dk ** -0.5, k_r], axis=1))
    wc = bf(jnp.concatenate([v_r, kc_a, vc_a, vs_a, vw_a], axis=1))
    wd = bf(jnp.concatenate([g_r, g_m, g_a, jnp.zeros((D, gate_pad), F32)], axis=1))
    pos = jnp.arange(S)
    tab_a = _rope_tables(pos, ROPE_DIM, dh, ROPE_THETA)
    tab_b = _rope_tables(pos, dk, dk, RET_ROPE_THETA)
    qk_n, qk_r, krT, v_ret, kc_tok, vc_tok, vsT, vwT, gates = _proj(
        x1, p["mix_norm"], wa, wb, wc, wd, tab_a, tab_b, B, S)

    chunk = lambda t: t.reshape(B, S // CMP_STRIDE, CMP_STRIDE * G * dh)
    kw1, kw2, pe = _compress_weights(p["cmp_k_w1"], p["cmp_k_w2"], p["cmp_pos_emb"])
    vw1, vw2, _ = _compress_weights(p["cmp_v_w1"], p["cmp_v_w2"], p["cmp_pos_emb"])
    cmp_end = jnp.arange(S // CMP_STRIDE) * CMP_STRIDE + (CMP_BLOCK - 1)
    kc, vcT = _compress(chunk(kc_tok), chunk(vc_tok), pe, kw1, kw2, vw1, vw2,
                        _rope_tables(cmp_end, ROPE_DIM, dh, ROPE_THETA))
    c2sT, et = _nsa_constants(S)
    a_out = _nsa(qk_n, gates, kc, vcT, vsT, vwT, c2sT, et, B, S)
    r_out = _retention(qk_r, krT, v_ret, p["ret_gn_gain"], B, S).reshape(T, H * dv)

    x2 = _merge(x1, a_out, r_out, gates, bf(p["w_branch_nsa"]), bf(p["w_branch_ret"]), bf(p["w_out"]))
    x3 = _ffn(x2, p["ffn2_norm"], bf(p["ffn2_w_gate"]), bf(p["ffn2_w_up"]), bf(p["ffn2_w_down"]),
              final_norm, final=True)
    return x3.reshape(B, S, D)


def kernel(x, ffn1_norm, ffn1_w_gate, ffn1_w_up, ffn1_w_down, mix_norm, w_in, cmp_pos_emb, cmp_k_w1,
           cmp_k_w2, cmp_v_w1, cmp_v_w2, ret_gn_gain, w_branch_nsa, w_branch_ret, w_out, ffn2_norm,
           ffn2_w_gate, ffn2_w_up, ffn2_w_down, final_norm):
    assert ffn1_norm.shape[0] == 1, "single-layer stack"
    names = ("ffn1_norm", "ffn1_w_gate", "ffn1_w_up", "ffn1_w_down", "mix_norm", "w_in", "cmp_pos_emb",
             "cmp_k_w1", "cmp_k_w2", "cmp_v_w1", "cmp_v_w2", "ret_gn_gain", "w_branch_nsa",
             "w_branch_ret", "w_out", "ffn2_norm", "ffn2_w_gate", "ffn2_w_up", "ffn2_w_down")
    vals = (ffn1_norm, ffn1_w_gate, ffn1_w_up, ffn1_w_down, mix_norm, w_in, cmp_pos_emb, cmp_k_w1,
            cmp_k_w2, cmp_v_w1, cmp_v_w2, ret_gn_gain, w_branch_nsa, w_branch_ret, w_out, ffn2_norm,
            ffn2_w_gate, ffn2_w_up, ffn2_w_down)
    p = {n: v[0] for n, v in zip(names, vals)}
    return _layer(x, p, final_norm)
```

```python
import functools
import math

import numpy as np
import jax
import jax.numpy as jnp
from jax import lax
from jax.experimental import pallas as pl
from jax.experimental.pallas import tpu as pltpu

F32 = jnp.float32
BF16 = jnp.bfloat16

NSA_HEADS = 16
NSA_GROUPS = 2
NSA_HG = NSA_HEADS // NSA_GROUPS
NSA_DH = 64
CMP_BLOCK = 32
CMP_STRIDE = 16
SLC_BLOCK = 64
SLC_TOPN = 16
WINDOW = 512
Q_BLOCK = 128
ROPE_THETA = 500000.0
ROPE_DIM = NSA_DH // 4
FORCED_SCORE = 1.0e4
RET_HEADS = 8
RET_DK = 64
RET_DV = 128
RET_CHUNK = 128
RET_ROPE_THETA = 10000.0
EPS = 1e-6
GN_EPS = 1e-5
NEG_INF = -1e30
LOG2E = math.log2(math.e)

LANES = 128
BF16_ROWS = 16
MXU_N = 256
VMEM_LIMIT = 56 * 1024 * 1024

NT_DIMS = (((1,), (1,)), ((), ()))


def _rms(x, g, eps=EPS):
    return x * lax.rsqrt(jnp.mean(x * x, axis=-1, keepdims=True) + eps) * g


def _resident(shape):
    nd = len(shape)
    return pl.BlockSpec(shape, lambda *_: (0,) * nd, pipeline_mode=pl.Buffered(1))


def _ffn_body(x_ref, g_ref, wg_ref, wu_ref, wd_ref, fg_ref, o_ref, *, cuts, final):
    x = x_ref[...]
    h = _rms(x, g_ref[...]).astype(BF16)
    acc = jnp.zeros(x.shape, F32)
    for lo, hi in zip(cuts[:-1], cuts[1:]):
        sl = slice(lo, hi)
        g = jnp.dot(h, wg_ref[:, sl], preferred_element_type=F32)
        u = jnp.dot(h, wu_ref[:, sl], preferred_element_type=F32)
        a = (g * jax.nn.sigmoid(g) * u).astype(BF16)
        acc = acc + jnp.dot(a, wd_ref[sl, :], preferred_element_type=F32)
    y = x + 0.5 * acc
    if final:
        y = _rms(y, fg_ref[...])
    o_ref[...] = y


def _ffn(x2d, gain, wg, wu, wd, final_gain, *, final, tm=512):
    T, D = x2d.shape
    F = wg.shape[1]
    assert F % MXU_N == 0
    n_tiles = F // MXU_N
    cuts = (0, (n_tiles + 1) // 2 * MXU_N, F)
    tok = pl.BlockSpec((tm, D), lambda i: (i, 0))
    return pl.pallas_call(
        functools.partial(_ffn_body, cuts=cuts, final=final),
        grid=(T // tm,),
        in_specs=[tok, _resident((1, D)), _resident((D, F)), _resident((D, F)), _resident((F, D)),
                  _resident((1, D))],
        out_specs=tok,
        out_shape=jax.ShapeDtypeStruct((T, D), F32),
        compiler_params=pltpu.CompilerParams(dimension_semantics=("parallel",),
                                             vmem_limit_bytes=VMEM_LIMIT),
        name="ffn_final" if final else "ffn",
    )(x2d, gain.reshape(1, D), wg, wu, wd, final_gain.reshape(1, D))


def _rope_slab(y, tab_ref, shift):
    return (y * tab_ref[0] + pltpu.roll(y, LANES - shift, 1) * tab_ref[1]
            + pltpu.roll(y, shift, 1) * tab_ref[2])


def _proj_body(x_ref, g_ref, wa_ref, wb_ref, wc_ref, wd_ref, ta_ref, tb_ref,
               oa_ref, ob_ref, okrT_ref, ov_ref, okc_ref, ovc_ref, ovsT_ref, ovwT_ref, od_ref):
    h = _rms(x_ref[...], g_ref[...]).astype(BF16)
    tm = h.shape[0]
    nv = ov_ref.shape[1] // LANES
    nkr = okrT_ref.shape[1]

    def slabs(w_ref):
        for c in range(w_ref.shape[0] // MXU_N):
            y = lax.dot_general(h, w_ref[c * MXU_N:(c + 1) * MXU_N, :], NT_DIMS, preferred_element_type=F32)
            for s in range(MXU_N // LANES):
                yield c * (MXU_N // LANES) + s, y[:, s * LANES:(s + 1) * LANES]

    def put(o_ref, i, ys):
        o_ref[:, i * LANES:(i + 1) * LANES] = ys.astype(o_ref.dtype)

    def put_t(o_ref, ys, *lead):
        for r in range(tm // LANES):
            o_ref[(0,) + lead + (r,)] = ys[r * LANES:(r + 1) * LANES, :].T.astype(o_ref.dtype)

    for i, ys in slabs(wa_ref):
        put(oa_ref, i, _rope_slab(ys, ta_ref, ROPE_DIM // 2))
    nb = wb_ref.shape[0] // LANES
    for i, ys in slabs(wb_ref):
        ys = _rope_slab(ys, tb_ref, RET_DK // 2)
        if i < nb - nkr:
            put(ob_ref, i, ys)
        else:
            put_t(okrT_ref, ys, i - (nb - nkr))
    for i, ys in slabs(wc_ref):
        if i < nv:
            put(ov_ref, i, ys)
        elif i == nv:
            put(okc_ref, 0, ys)
        elif i == nv + 1:
            put(ovc_ref, 0, ys)
        elif i == nv + 2:
            put_t(ovsT_ref, ys)
        else:
            put_t(ovwT_ref, ys)
    for i, ys in slabs(wd_ref):
        put(od_ref, i, ys)


def _proj(x2d, gain, wa, wb, wc, wd, tab_a, tab_b, B, S, *, tm=512):
    T, D = x2d.shape
    spt = S // tm
    na, nbw, nd = wa.shape[0], wb.shape[0], wd.shape[0]
    nv = wc.shape[0] - 4 * LANES
    tok = lambda n: pl.BlockSpec((tm, n), lambda i: (i, 0))
    tab = pl.BlockSpec((3, tm, LANES), lambda i: (0, i % spt, 0))
    tr = pl.BlockSpec((1, tm // LANES, LANES, LANES), lambda i: (i // spt, i % spt, 0, 0))
    nkr = RET_HEADS * RET_DK // LANES
    tr_kr = pl.BlockSpec((1, nkr, tm // LANES, LANES, LANES), lambda i: (i // spt, 0, i % spt, 0, 0))
    return pl.pallas_call(
        _proj_body,
        grid=(T // tm,),
        in_specs=[tok(D), _resident((1, D)), _resident(wa.shape), _resident(wb.shape),
                  _resident(wc.shape), _resident(wd.shape), tab, tab],
        out_specs=[tok(na), tok(nbw - nkr * LANES), tr_kr, tok(nv), tok(LANES), tok(LANES),
                   tr, tr, tok(nd)],
        out_shape=[jax.ShapeDtypeStruct((T, na), BF16),
                   jax.ShapeDtypeStruct((T, nbw - nkr * LANES), BF16),
                   jax.ShapeDtypeStruct((B, nkr, S // LANES, LANES, LANES), BF16),
                   jax.ShapeDtypeStruct((T, nv), BF16),
                   jax.ShapeDtypeStruct((T, LANES), BF16),
                   jax.ShapeDtypeStruct((T, LANES), BF16),
                   jax.ShapeDtypeStruct((B, S // LANES, LANES, LANES), BF16),
                   jax.ShapeDtypeStruct((B, S // LANES, LANES, LANES), BF16),
                   jax.ShapeDtypeStruct((T, nd), BF16)],
        compiler_params=pltpu.CompilerParams(dimension_semantics=("parallel",),
                                             vmem_limit_bytes=VMEM_LIMIT),
        name="proj",
    )(x2d, gain.reshape(1, D), wa, wb, wc, wd, tab_a, tab_b)


def _rope_tables(pos, rot_dim, head_dim, theta):
    half = rot_dim // 2
    freqs = theta ** (-(jnp.arange(half, dtype=F32) * 2.0 / rot_dim))
    ang = pos.astype(F32)[:, None] * freqs[None, :]
    cos, sin = jnp.cos(ang), jnp.sin(ang)
    n = pos.shape[0]
    rest = head_dim - rot_dim
    cos_h = jnp.concatenate([cos, cos, jnp.ones((n, rest), F32)], axis=-1)
    sl_h = jnp.concatenate([-sin, jnp.zeros((n, half + rest), F32)], axis=-1)
    sr_h = jnp.concatenate([jnp.zeros((n, half), F32), sin, jnp.zeros((n, rest), F32)], axis=-1)
    rep = LANES // head_dim
    return jnp.stack([jnp.tile(t, (1, rep)) for t in (cos_h, sl_h, sr_h)])


def _gelu_tanh(x):
    return 0.5 * x * (1.0 + jnp.tanh(math.sqrt(2.0 / math.pi) * (x + 0.044715 * (x * x * x))))


def _compress_body(kt_ref, vt_ref, pe_ref, kw1_ref, kw2_ref, vw1_ref, vw2_ref, tab_ref,
                   kc_ref, vcT_ref):
    pe = pe_ref[...]
    n = kt_ref.shape[1]

    def mlp(tok_ref, w1_ref, w2_ref):
        c = tok_ref[0].astype(F32)
        lo = (c + pe[0:1]).astype(BF16)
        hi = (c + pe[1:2]).astype(BF16)
        out = jnp.zeros((n, LANES), F32)
        for g in range(NSA_GROUPS):
            top = jnp.dot(lo, w1_ref[g, 0], preferred_element_type=F32)
            bot = jnp.dot(hi, w1_ref[g, 1], preferred_element_type=F32)
            hid = _gelu_tanh(top + pltpu.roll(bot, n - 1, 0))
            out = out + jnp.dot(hid.astype(BF16), w2_ref[g], preferred_element_type=F32)
        return out

    kc_ref[0] = _rope_slab(mlp(kt_ref, kw1_ref, kw2_ref), tab_ref, ROPE_DIM // 2).astype(kc_ref.dtype)
    vc = mlp(vt_ref, vw1_ref, vw2_ref)
    for r in range(n // LANES):
        vcT_ref[0, :, r * LANES:(r + 1) * LANES] = vc[r * LANES:(r + 1) * LANES, :].T.astype(vcT_ref.dtype)


def _compress(k_tok, v_tok, pe, kw1, kw2, vw1, vw2, tab):
    B, n, w = k_tok.shape
    tok = pl.BlockSpec((1, n, w), lambda b: (b, 0, 0))
    return pl.pallas_call(
        _compress_body,
        grid=(B,),
        in_specs=[tok, tok, _resident(pe.shape), _resident(kw1.shape), _resident(kw2.shape),
                  _resident(vw1.shape), _resident(vw2.shape), _resident(tab.shape)],
        out_specs=[pl.BlockSpec((1, n, LANES), lambda b: (b, 0, 0)),
                   pl.BlockSpec((1, LANES, n), lambda b: (b, 0, 0))],
        out_shape=[jax.ShapeDtypeStruct((B, n, LANES), BF16), jax.ShapeDtypeStruct((B, LANES, n), BF16)],
        compiler_params=pltpu.CompilerParams(dimension_semantics=("parallel",),
                                             vmem_limit_bytes=VMEM_LIMIT),
        name="compress",
    )(k_tok, v_tok, pe, kw1, kw2, vw1, vw2, tab)


def _compress_weights(w1, w2, pe):
    G, dh = NSA_GROUPS, NSA_DH
    hid = w1.shape[1]
    halves = w1.reshape(2, CMP_STRIDE, dh, hid)
    w1e = jnp.zeros((G, 2, CMP_STRIDE, G, dh, hid), F32)
    w2e = jnp.zeros((G, hid, G, dh), F32)
    for g in range(G):
        w1e = w1e.at[g, :, :, g].set(halves)
        w2e = w2e.at[g, :, g].set(w2)
    pe_e = jnp.broadcast_to(pe.reshape(2, CMP_STRIDE, 1, dh), (2, CMP_STRIDE, G, dh))
    return (w1e.reshape(G, 2, CMP_STRIDE * G * dh, hid).astype(BF16), w2e.reshape(G, hid, G * dh).astype(BF16),
            pe_e.reshape(2, CMP_STRIDE * G * dh))


def _nsa_body(q_ref, gate_ref, kc_ref, vcT_ref, ks_ref, vsT_ref, kw_ref, vwT_ref, c2sT_ref, et_ref,
              o_ref, qT_scr, gT_scr, m_scr, acc_scr, rank_scr, *, kt, top_n, n_slc, hg, dh):
    qb = q_ref.shape[0]
    nh = q_ref.shape[1] // dh
    groups = nh // hg
    ncp = kc_ref.shape[1]
    c = pl.program_id(1)
    q0 = c * qb
    ones_rows = jnp.ones((BF16_ROWS, 1), BF16)
    grp_rows = lambda g: slice(g * dh, (g + 1) * dh)

    def v_tiles(vT_ref, k0, n):
        t = vT_ref[0, pl.ds(k0 // LANES, n // LANES)]
        vT = jnp.concatenate([t[i] for i in range(n // LANES)], axis=1)
        ones = jnp.broadcast_to(ones_rows, (BF16_ROWS, n))
        return [jnp.concatenate([vT[grp_rows(g)], ones], axis=0) for g in range(groups)]

    def tq(n):
        return q0 + lax.broadcasted_iota(jnp.int32, (n, qb), 1)

    def krow(n, k0=0):
        return k0 + lax.broadcasted_iota(jnp.int32, (n, qb), 0)

    qT_scr[...] = jnp.zeros(qT_scr.shape, BF16)
    for r in range(qb // LANES):
        rows = slice(r * LANES, (r + 1) * LANES)
        for j in range(nh // 2):
            t = q_ref[rows, j * LANES:(j + 1) * LANES].astype(F32).T.astype(BF16)
            for half in range(2):
                h = 2 * j + half
                qT_scr[grp_rows(h // hg), h * qb + r * LANES:h * qb + (r + 1) * LANES] = (
                    t[half * dh:(half + 1) * dh])
        gT_scr[:, rows] = jax.nn.sigmoid(gate_ref[rows, :LANES].astype(F32).T)

    qT = qT_scr[:LANES, :]

    span = WINDOW + LANES

    def window_tile(r):
        t0 = q0 + r * LANES
        w0 = pl.multiple_of(jnp.maximum(t0 - WINDOW, 0), LANES)
        dist = (t0 + lax.broadcasted_iota(jnp.int32, (span, LANES), 1)
                - (w0 + lax.broadcasted_iota(jnp.int32, (span, LANES), 0)))
        bias_w = jnp.where((dist >= 0) & (dist < WINDOW), 0.0, NEG_INF)
        bias_w = jnp.concatenate([bias_w, bias_w], axis=1)
        qT_r = jnp.concatenate([qT[:, h * qb + r * LANES:h * qb + (r + 1) * LANES] for h in range(nh)], axis=1)
        s_w = jnp.dot(kw_ref[0, pl.ds(w0, span), :], qT_r, preferred_element_type=F32)
        vwT = v_tiles(vwT_ref, w0, span)
        outs_r = []
        for hp in range(nh // 2):
            sh = s_w[:, hp * 2 * LANES:(hp + 1) * 2 * LANES] + bias_w
            p = jnp.exp2(sh - jnp.max(sh, axis=0, keepdims=True)).astype(BF16)
            acc_w = jnp.dot(vwT[2 * hp // hg], p, preferred_element_type=F32)
            ow = acc_w[:dh] * (1.0 / acc_w[dh:dh + 1])
            outs_r += [ow[:, :LANES], ow[:, LANES:]]
        return outs_r

    n_lt = qb // LANES
    o_w = [window_tile(r) for r in range(n_lt // 2)]

    s_c = jnp.dot(kc_ref[0], qT, preferred_element_type=F32)
    cend = krow(ncp) * CMP_STRIDE + (CMP_BLOCK - 1)
    bias_c = jnp.where(cend <= tq(ncp), 0.0, NEG_INF)
    bias_c = jnp.concatenate([bias_c, bias_c], axis=1)
    nb = dh
    vcT = vcT_ref[0]
    lhs_c = [jnp.concatenate([vcT[grp_rows(g)], jnp.broadcast_to(ones_rows, (BF16_ROWS, ncp)),
                              c2sT_ref[:nb, :]], axis=0) for g in range(groups)]
    imps = [jnp.zeros((nb, qb), F32) for _ in range(groups)]
    o_c = []
    for hp in range(nh // 2):
        g = 2 * hp // hg
        sh = s_c[:, hp * 2 * qb:(hp + 1) * 2 * qb] + bias_c
        e = jnp.exp2(sh - jnp.max(sh, axis=0, keepdims=True)).astype(BF16)
        r = jnp.dot(lhs_c[g], e, preferred_element_type=F32)
        inv_l = 1.0 / r[dh:dh + 1]
        o_c.append(r[:dh] * inv_l)
        w = r[dh + BF16_ROWS:] * inv_l
        imps[g] = imps[g] + w[:, :qb] + w[:, qb:]
    o_c = jnp.concatenate(o_c, axis=1)
    sb = krow(nb)
    cur = jnp.right_shift(tq(nb), int(math.log2(SLC_BLOCK)))
    forced = (sb == 0) | (sb == cur) | (sb == cur - 1)
    future = sb > cur
    imps = [jnp.where(forced, FORCED_SCORE, jnp.where(future, -FORCED_SCORE, imp)) for imp in imps]
    sub = 8
    rank_scr[...] = jnp.zeros(rank_scr.shape, jnp.int32)
    last_block = (q0 + qb - 1) // SLC_BLOCK
    sb_l = sb[:, :LANES]
    for grp in range(n_slc // sub):
        @pl.when(grp * sub <= last_block)
        def _():
            for g in range(groups):
                for lc in range(qb // LANES):
                    imp_l = imps[g][:, lc * LANES:(lc + 1) * LANES]
                    chunks = [imp_l[r * sub:(r + 1) * sub] for r in range(nb // sub)]
                    ranks = [jnp.zeros((sub, LANES), jnp.int32) for _ in chunks]
                    for sp in range(grp * sub, (grp + 1) * sub):
                        row = imp_l[sp:sp + 1, :]
                        for r, blk in enumerate(chunks):
                            if r * sub > sp:
                                beats = row >= blk
                            elif (r + 1) * sub - 1 <= sp:
                                beats = row > blk
                            else:
                                beats = (row > blk) | ((row == blk) & (sb_l[r * sub:(r + 1) * sub] > sp))
                            ranks[r] = ranks[r] + jnp.where(beats, 1, 0)
                    rank_scr[g, :, lc * LANES:(lc + 1) * LANES] += jnp.concatenate(ranks, axis=0)
    for g in range(groups):
        selneg = jnp.where((rank_scr[g] < top_n) & jnp.logical_not(future), 0.0, NEG_INF).astype(BF16)
        for h in range(g * hg, (g + 1) * hg):
            qT_scr[LANES:LANES + nb, h * qb:(h + 1) * qb] = selneg

    qT_aug = qT_scr[...]

    def scores(j):
        k0 = pl.multiple_of(j * kt, kt)
        ka = jnp.concatenate([ks_ref[0, pl.ds(k0, kt), :], et_ref[pl.ds(k0, kt), :]], axis=1)
        return jnp.dot(ka, qT_aug, preferred_element_type=F32)

    def fold(s, j, causal):
        vT = v_tiles(vsT_ref, pl.multiple_of(j * kt, kt), kt)
        if causal:
            bias = jnp.where(krow(kt, j * kt) <= tq(kt), 0.0, NEG_INF)
            bias = jnp.concatenate([bias, bias], axis=1)
        for hp in range(nh // 2):
            sl = slice(hp * 2 * qb, (hp + 1) * 2 * qb)
            sh = s[:, sl] + bias if causal else s[:, sl]
            m_old = m_scr[:, sl]
            m_new = jnp.maximum(m_old, jnp.max(sh, axis=0, keepdims=True))
            pv = jnp.dot(vT[2 * hp // hg], jnp.exp2(sh - m_new).astype(BF16), preferred_element_type=F32)
            acc_scr[:, sl] = jnp.exp2(m_old - m_new) * acc_scr[:, sl] + pv
            m_scr[:, sl] = m_new

    m_scr[...] = jnp.full(m_scr.shape, NEG_INF, F32)
    acc_scr[...] = jnp.zeros(acc_scr.shape, F32)

    def run_tiles(j0, n):
        ss = [scores(j0 + t) for t in range(n)]
        for t in range(n):
            fold(ss[t], j0 + t, False)

    def main_pair(i, carry):
        run_tiles(2 * i, 2)
        return carry

    lax.fori_loop(0, c // 2, main_pair, 0)

    @pl.when(c % 2 == 1)
    def _():
        fold(scores(c - 1), c - 1, False)

    fold(scores(c), c, True)
    o_s = acc_scr[:dh, :] * (1.0 / acc_scr[dh:dh + 1, :])

    o_w += [window_tile(r) for r in range(n_lt // 2, n_lt)]
    o_w = jnp.concatenate([o_w[r][h] for h in range(nh) for r in range(n_lt)], axis=1)

    has_cmp = (tq(1) >= CMP_BLOCK - 1).astype(F32)
    gts = gT_scr[:3 * nh, :]
    outs = []
    for h in range(nh):
        sl = slice(h * qb, (h + 1) * qb)
        outs.append(gts[3 * h:3 * h + 1] * has_cmp * o_c[:, sl] + gts[3 * h + 1:3 * h + 2] * o_s[:, sl]
                    + gts[3 * h + 2:3 * h + 3] * o_w[:, sl])
    for j in range(nh // 2):
        pair = jnp.concatenate([outs[2 * j], outs[2 * j + 1]], axis=0)
        for r in range(qb // LANES):
            rows = slice(r * LANES, (r + 1) * LANES)
            o_ref[rows, j * LANES:(j + 1) * LANES] = pair[:, rows].T.astype(o_ref.dtype)


def _nsa(qk, gates, kc, vcT, vsT, vwT, c2sT, et, B, S, *, qb=256):
    G, Hg, dh = NSA_GROUPS, NSA_HG, NSA_DH
    T = B * S
    nq = S // qb
    n_slc = S // SLC_BLOCK
    kt = qb
    assert n_slc <= dh and 2 * dh == LANES and qb % LANES == 0 and S % qb == 0 and WINDOW % qb == 0
    assert S >= WINDOW + qb
    qw = G * Hg * dh
    assert 3 * G * Hg <= LANES and G * dh == LANES
    ncp = kc.shape[1]
    k_col = qw // LANES
    g_col = gates.shape[1] // MXU_N - 1
    seq_k = lambda col: pl.BlockSpec((1, S, LANES), lambda b, c: (b, 0, col))
    seq_vT = pl.BlockSpec((1, S // LANES, LANES, LANES), lambda b, c: (b, 0, 0, 0))
    qk3 = qk.reshape(B, S, qk.shape[1])
    cols = G * Hg * qb
    return pl.pallas_call(
        functools.partial(_nsa_body, kt=kt, top_n=min(SLC_TOPN, n_slc), n_slc=n_slc, hg=Hg, dh=dh),
        grid=(B, nq),
        in_specs=[pl.BlockSpec((qb, qw), lambda b, c: (b * nq + c, 0)),
                  pl.BlockSpec((qb, MXU_N), lambda b, c: (b * nq + c, g_col)),
                  pl.BlockSpec((1, ncp, LANES), lambda b, c: (b, 0, 0)),
                  pl.BlockSpec((1, LANES, ncp), lambda b, c: (b, 0, 0)),
                  seq_k(k_col), seq_vT, seq_k(k_col + 1), seq_vT,
                  _resident(c2sT.shape), _resident(et.shape)],
        out_specs=pl.BlockSpec((qb, qw), lambda b, c: (b * nq + c, 0)),
        out_shape=jax.ShapeDtypeStruct((T, qw), BF16),
        scratch_shapes=[pltpu.VMEM((2 * LANES, cols), BF16), pltpu.VMEM((LANES, qb), F32),
                        pltpu.VMEM((1, cols), F32), pltpu.VMEM((dh + BF16_ROWS, cols), F32),
                        pltpu.VMEM((G, dh, qb), jnp.int32)],
        compiler_params=pltpu.CompilerParams(dimension_semantics=("parallel", "arbitrary"),
                                             vmem_limit_bytes=VMEM_LIMIT),
        name="nsa",
    )(qk, gates, kc, vcT, qk3, vsT, qk3, vwT, c2sT, et)


def _nsa_constants(S):
    ncp = S // CMP_STRIDE
    n_slc = S // SLC_BLOCK
    cmp_start = np.arange(ncp) * CMP_STRIDE
    s_start = np.arange(n_slc) * SLC_BLOCK
    overlap = np.clip(np.minimum(cmp_start[:, None] + CMP_BLOCK, s_start[None, :] + SLC_BLOCK)
                      - np.maximum(cmp_start[:, None], s_start[None, :]), 0, None)
    c2sT = np.zeros((LANES, ncp), np.float32)
    c2sT[:n_slc, :] = overlap.T.astype(np.float32) / CMP_BLOCK
    c2sT[:, ncp - 1] = 0.0
    et = np.zeros((S, LANES), np.float32)
    et[np.arange(S), np.arange(S) // SLC_BLOCK] = 1.0
    return jnp.asarray(c2sT, BF16), jnp.asarray(et, BF16)


def _ret_body(q_ref, kT_ref, v_ref, decay_ref, zeta_ref, xi_ref, gch_ref, gn_ref, o_ref, *, C, dk):
    S = q_ref.shape[1]
    dv = gn_ref.shape[2]
    heads = LANES // dk
    N = S // C
    units = [(n, a) for n in range(N) for a in range(heads)]
    rows = lambda n: slice(n * C, (n + 1) * C)
    val = lambda n, a: v_ref[0, rows(n), a * dv:(a + 1) * dv]
    qp = lambda n: q_ref[0, rows(n), :]

    def own_rows(x, a):
        return jnp.concatenate([x if b == a else jnp.zeros_like(x) for b in range(heads)], axis=0)

    kT = {(n, a): kT_ref[0, 0, n, a * dk:(a + 1) * dk, :] for n, a in units}
    inner = {u: jnp.dot(qp(u[0]), own_rows(kT[u], u[1]), preferred_element_type=F32) for u in units}
    inner = {u: (inner[u] * decay_ref[u[1]]).astype(BF16) for u in units}
    o = {u: jnp.dot(inner[u], val(*u), preferred_element_type=F32) for u in units}
    kv = {(n, a): jnp.dot((kT[(n, a)].astype(F32) * zeta_ref[a]).astype(BF16), val(n, a),
                          preferred_element_type=F32) for n, a in units}
    state = {}
    for a in range(heads):
        R = jnp.zeros((dk, dv), F32)
        for n in range(N):
            state[(n, a)] = own_rows(R.astype(BF16), a)
            R = gch_ref[a] * R + kv[(n, a)]
    o = {u: o[u] + xi_ref[u[1]] * jnp.dot(qp(u[0]), state[u], preferred_element_type=F32) for u in units}
    mu = {u: jnp.mean(o[u], axis=-1, keepdims=True) for u in units}
    d = {u: o[u] - mu[u] for u in units}
    var = {u: jnp.mean(d[u] * d[u], axis=-1, keepdims=True) for u in units}
    for n, a in units:
        o_ref[0, rows(n), a * dv:(a + 1) * dv] = (d[(n, a)] * lax.rsqrt(var[(n, a)] + GN_EPS)
                                                  * gn_ref[a]).astype(o_ref.dtype)


def _retention(qk, kT, v, gn_gain, B, S, *, C=RET_CHUNK):
    H, dk, dv = RET_HEADS, RET_DK, RET_DV
    hp = LANES // dk
    assert C == LANES and dv == LANES
    log_g = jnp.log(1.0 - 2.0 ** (-5.0 - jnp.arange(H, dtype=F32)))
    i = jnp.arange(C, dtype=F32)
    diff = i[:, None] - i[None, :]
    decay = jnp.where(diff >= 0, jnp.exp(jnp.maximum(diff, 0.0) * log_g[:, None, None]), 0.0)
    zeta = jnp.exp((C - 1.0 - i)[None, :] * log_g[:, None]).reshape(H, 1, C)
    xi = jnp.broadcast_to(jnp.exp((i + 1.0)[None, :] * log_g[:, None])[:, :, None], (H, C, LANES))
    gch = jnp.broadcast_to(jnp.exp(C * log_g)[:, None, None], (H, 1, dv))
    per_pair = lambda shape: pl.BlockSpec((hp,) + shape, lambda b, p: (p, 0, 0))
    qk3 = qk.reshape(B, S, qk.shape[1])
    return pl.pallas_call(
        functools.partial(_ret_body, C=C, dk=dk),
        grid=(B, H // hp),
        in_specs=[pl.BlockSpec((1, S, LANES), lambda b, p: (b, 0, p)),
                  pl.BlockSpec((1, 1, S // LANES, LANES, LANES), lambda b, p: (b, p, 0, 0, 0)),
                  pl.BlockSpec((1, S, hp * dv), lambda b, p: (b, 0, p)),
                  per_pair((C, C)), per_pair((1, C)), per_pair((C, LANES)), per_pair((1, dv)),
                  per_pair((1, dv))],
        out_specs=pl.BlockSpec((1, S, hp * dv), lambda b, p: (b, 0, p)),
        out_shape=jax.ShapeDtypeStruct((B, S, H * dv), BF16),
        compiler_params=pltpu.CompilerParams(dimension_semantics=("parallel", "parallel"),
                                             vmem_limit_bytes=VMEM_LIMIT),
        name="retention",
    )(qk3, kT, v.reshape(B, S, H * dv), decay, zeta, xi, gch, gn_gain.reshape(H, 1, dv))


def _merge_body(x_ref, a_ref, r_ref, gr_ref, ga_ref, gb_ref, wa_ref, wr_ref, wo_ref, o_ref):
    g = gr_ref[...].astype(F32)
    r = (g * jax.nn.sigmoid(g) * r_ref[...].astype(F32)).astype(BF16)
    ya = jnp.dot(a_ref[...], wa_ref[...], preferred_element_type=F32)
    yr = jnp.dot(r, wr_ref[...], preferred_element_type=F32)
    mixed = jax.nn.sigmoid(ga_ref[...].astype(F32)) * ya + jax.nn.sigmoid(gb_ref[...].astype(F32)) * yr
    o_ref[...] = x_ref[...] + jnp.dot(mixed.astype(BF16), wo_ref[...], preferred_element_type=F32)


def _merge(x2d, a, r, gates, wa, wr, wo, *, tm=512):
    T, D = x2d.shape
    col = lambda j: pl.BlockSpec((tm, D), lambda i: (i, j))
    return pl.pallas_call(
        _merge_body,
        grid=(T // tm,),
        in_specs=[col(0), col(0), col(0), col(0), col(1), col(2),
                  _resident(wa.shape), _resident(wr.shape), _resident(wo.shape)],
        out_specs=col(0),
        out_shape=jax.ShapeDtypeStruct((T, D), F32),
        compiler_params=pltpu.CompilerParams(dimension_semantics=("parallel",),
                                             vmem_limit_bytes=VMEM_LIMIT),
        name="merge",
    )(x2d, a, r, gates, gates, gates, wa, wr, wo)


def _layer(x, p, final_norm):
    B, S, D = x.shape
    T = B * S
    G, Hg, dh = NSA_GROUPS, NSA_HG, NSA_DH
    H, dk, dv = RET_HEADS, RET_DK, RET_DV
    bf = lambda w: w.astype(BF16)

    x1 = _ffn(x.reshape(T, D), p["ffn1_norm"], bf(p["ffn1_w_gate"]), bf(p["ffn1_w_up"]),
              bf(p["ffn1_w_down"]), final_norm, final=False)

    w = p["w_in"].T
    widths = (G * Hg * dh,) + (G * dh,) * 6 + (3 * G * Hg, H * dk, H * dk, H * dv, H * dv, 2 * D)
    offs = np.concatenate([[0], np.cumsum(widths)])
    seg = lambda i: w[int(offs[i]):int(offs[i + 1])]
    (q_a, kc_a, vc_a, ks_a, vs_a, kw_a, vw_a, g_a, q_r, k_r, v_r, g_r, g_m) = [seg(i) for i in range(13)]
    gate_pad = MXU_N - g_a.shape[0]
    wa = bf(jnp.concatenate([q_a * (dh ** -0.5 * LOG2E), ks_a, kw_a], axis=0))
    wb = bf(jnp.concatenate([q_r * dk ** -0.5, k_r], axis=0))
    wc = bf(jnp.concatenate([v_r, kc_a, vc_a, vs_a, vw_a], axis=0))
    wd = bf(jnp.concatenate([g_r, g_m, g_a, jnp.zeros((gate_pad, D), F32)], axis=0))
    pos = jnp.arange(S)
    tab_a = _rope_tables(pos, ROPE_DIM, dh, ROPE_THETA)
    tab_b = _rope_tables(pos, dk, dk, RET_ROPE_THETA)
    qk_n, qk_r, krT, v_ret, kc_tok, vc_tok, vsT, vwT, gates = _proj(
        x1, p["mix_norm"], wa, wb, wc, wd, tab_a, tab_b, B, S)

    chunk = lambda t: t.reshape(B, S // CMP_STRIDE, CMP_STRIDE * G * dh)
    kw1, kw2, pe = _compress_weights(p["cmp_k_w1"], p["cmp_k_w2"], p["cmp_pos_emb"])
    vw1, vw2, _ = _compress_weights(p["cmp_v_w1"], p["cmp_v_w2"], p["cmp_pos_emb"])
    cmp_end = jnp.arange(S // CMP_STRIDE) * CMP_STRIDE + (CMP_BLOCK - 1)
    kc, vcT = _compress(chunk(kc_tok), chunk(vc_tok), pe, kw1, kw2, vw1, vw2,
                        _rope_tables(cmp_end, ROPE_DIM, dh, ROPE_THETA))
    c2sT, et = _nsa_constants(S)
    a_out = _nsa(qk_n, gates, kc, vcT, vsT, vwT, c2sT, et, B, S)
    r_out = _retention(qk_r, krT, v_ret, p["ret_gn_gain"], B, S).reshape(T, H * dv)

    x2 = _merge(x1, a_out, r_out, gates, bf(p["w_branch_nsa"]), bf(p["w_branch_ret"]), bf(p["w_out"]))
    x3 = _ffn(x2, p["ffn2_norm"], bf(p["ffn2_w_gate"]), bf(p["ffn2_w_up"]), bf(p["ffn2_w_down"]),
              final_norm, final=True)
    return x3.reshape(B, S, D)


def kernel(x, ffn1_norm, ffn1_w_gate, ffn1_w_up, ffn1_w_down, mix_norm, w_in, cmp_pos_emb, cmp_k_w1,
           cmp_k_w2, cmp_v_w1, cmp_v_w2, ret_gn_gain, w_branch_nsa, w_branch_ret, w_out, ffn2_norm,
           ffn2_w_gate, ffn2_w_up, ffn2_w_down, final_norm):
    assert ffn1_norm.shape[0] == 1, "single-layer stack"
    names = ("ffn1_norm", "ffn1_w_gate", "ffn1_w_up", "ffn1_w_down", "mix_norm", "w_in", "cmp_pos_emb",
             "cmp_k_w1", "cmp_k_w2", "cmp_v_w1", "cmp_v_w2", "ret_gn_gain", "w_branch_nsa",
             "w_branch_ret", "w_out", "ffn2_norm", "ffn2_w_gate", "ffn2_w_up", "ffn2_w_down")
    vals = (ffn1_norm, ffn1_w_gate, ffn1_w_up, ffn1_w_down, mix_norm, w_in, cmp_pos_emb, cmp_k_w1,
            cmp_k_w2, cmp_v_w1, cmp_v_w2, ret_gn_gain, w_branch_nsa, w_branch_ret, w_out, ffn2_norm,
            ffn2_w_gate, ffn2_w_up, ffn2_w_down)
    p = {n: v[0] for n, v in zip(names, vals)}
    return _layer(x, p, final_norm)
```

```python
import functools
import math

import numpy as np
import jax
import jax.numpy as jnp
from jax import lax
from jax.experimental import pallas as pl
from jax.experimental.pallas import tpu as pltpu

F32 = jnp.float32
BF16 = jnp.bfloat16

NSA_HEADS = 16
NSA_GROUPS = 2
NSA_HG = NSA_HEADS // NSA_GROUPS
NSA_DH = 64
CMP_BLOCK = 32
CMP_STRIDE = 16
SLC_BLOCK = 64
SLC_TOPN = 16
WINDOW = 512
Q_BLOCK = 128
ROPE_THETA = 500000.0
ROPE_DIM = NSA_DH // 4
FORCED_SCORE = 1.0e4
RET_HEADS = 8
RET_DK = 64
RET_DV = 128
RET_CHUNK = 128
RET_ROPE_THETA = 10000.0
EPS = 1e-6
GN_EPS = 1e-5
NEG_INF = -1e30
LOG2E = math.log2(math.e)

LANES = 128
BF16_ROWS = 16
MXU_N = 256
VMEM_LIMIT = 56 * 1024 * 1024

NT_DIMS = (((1,), (1,)), ((), ()))


def _rms(x, g, eps=EPS):
    return x * lax.rsqrt(jnp.mean(x * x, axis=-1, keepdims=True) + eps) * g


def _resident(shape):
    nd = len(shape)
    return pl.BlockSpec(shape, lambda *_: (0,) * nd, pipeline_mode=pl.Buffered(1))


def _ffn_body(x_ref, g_ref, wg_ref, wu_ref, wd_ref, fg_ref, o_ref, *, cuts, final):
    x = x_ref[...]
    h = _rms(x, g_ref[...]).astype(BF16)
    acc = jnp.zeros(x.shape, F32)
    for lo, hi in zip(cuts[:-1], cuts[1:]):
        sl = slice(lo, hi)
        g = jnp.dot(h, wg_ref[:, sl], preferred_element_type=F32)
        u = jnp.dot(h, wu_ref[:, sl], preferred_element_type=F32)
        a = (g * jax.nn.sigmoid(g) * u).astype(BF16)
        acc = acc + jnp.dot(a, wd_ref[sl, :], preferred_element_type=F32)
    y = x + 0.5 * acc
    if final:
        y = _rms(y, fg_ref[...])
    o_ref[...] = y


def _ffn(x2d, gain, wg, wu, wd, final_gain, *, final, tm=512):
    T, D = x2d.shape
    F = wg.shape[1]
    assert F % MXU_N == 0
    n_tiles = F // MXU_N
    cuts = (0, (n_tiles + 1) // 2 * MXU_N, F)
    tok = pl.BlockSpec((tm, D), lambda i: (i, 0))
    return pl.pallas_call(
        functools.partial(_ffn_body, cuts=cuts, final=final),
        grid=(T // tm,),
        in_specs=[tok, _resident((1, D)), _resident((D, F)), _resident((D, F)), _resident((F, D)),
                  _resident((1, D))],
        out_specs=tok,
        out_shape=jax.ShapeDtypeStruct((T, D), F32),
        compiler_params=pltpu.CompilerParams(dimension_semantics=("parallel",),
                                             vmem_limit_bytes=VMEM_LIMIT),
        name="ffn_final" if final else "ffn",
    )(x2d, gain.reshape(1, D), wg, wu, wd, final_gain.reshape(1, D))


def _rope_slab(y, tab_ref, shift):
    return (y * tab_ref[0] + pltpu.roll(y, LANES - shift, 1) * tab_ref[1]
            + pltpu.roll(y, shift, 1) * tab_ref[2])


def _proj_body(x_ref, g_ref, wa_ref, wb_ref, wc_ref, wd_ref, ta_ref, tb_ref,
               oa_ref, ob_ref, okrT_ref, ov_ref, okc_ref, ovc_ref, ovsT_ref, ovwT_ref, od_ref):
    h = _rms(x_ref[...], g_ref[...]).astype(BF16)
    tm = h.shape[0]
    nv = ov_ref.shape[1] // LANES
    nkr = okrT_ref.shape[1]

    def slabs(w_ref):
        for c in range(w_ref.shape[0] // MXU_N):
            y = lax.dot_general(h, w_ref[c * MXU_N:(c + 1) * MXU_N, :], NT_DIMS, preferred_element_type=F32)
            for s in range(MXU_N // LANES):
                yield c * (MXU_N // LANES) + s, y[:, s * LANES:(s + 1) * LANES]

    def put(o_ref, i, ys):
        o_ref[:, i * LANES:(i + 1) * LANES] = ys.astype(o_ref.dtype)

    def put_t(o_ref, ys, *lead):
        for r in range(tm // LANES):
            o_ref[(0,) + lead + (r,)] = ys[r * LANES:(r + 1) * LANES, :].T.astype(o_ref.dtype)

    for i, ys in slabs(wa_ref):
        put(oa_ref, i, _rope_slab(ys, ta_ref, ROPE_DIM // 2))
    nb = wb_ref.shape[0] // LANES
    for i, ys in slabs(wb_ref):
        ys = _rope_slab(ys, tb_ref, RET_DK // 2)
        if i < nb - nkr:
            put(ob_ref, i, ys)
        else:
            put_t(okrT_ref, ys, i - (nb - nkr))
    for i, ys in slabs(wc_ref):
        if i < nv:
            put(ov_ref, i, ys)
        elif i == nv:
            put(okc_ref, 0, ys)
        elif i == nv + 1:
            put(ovc_ref, 0, ys)
        elif i == nv + 2:
            put_t(ovsT_ref, ys)
        else:
            put_t(ovwT_ref, ys)
    for i, ys in slabs(wd_ref):
        put(od_ref, i, ys)


def _proj(x2d, gain, wa, wb, wc, wd, tab_a, tab_b, B, S, *, tm=512):
    T, D = x2d.shape
    spt = S // tm
    na, nbw, nd = wa.shape[0], wb.shape[0], wd.shape[0]
    nv = wc.shape[0] - 4 * LANES
    tok = lambda n: pl.BlockSpec((tm, n), lambda i: (i, 0))
    tab = pl.BlockSpec((3, tm, LANES), lambda i: (0, i % spt, 0))
    tr = pl.BlockSpec((1, tm // LANES, LANES, LANES), lambda i: (i // spt, i % spt, 0, 0))
    nkr = RET_HEADS * RET_DK // LANES
    tr_kr = pl.BlockSpec((1, nkr, tm // LANES, LANES, LANES), lambda i: (i // spt, 0, i % spt, 0, 0))
    return pl.pallas_call(
        _proj_body,
        grid=(T // tm,),
        in_specs=[tok(D), _resident((1, D)), _resident(wa.shape), _resident(wb.shape),
                  _resident(wc.shape), _resident(wd.shape), tab, tab],
        out_specs=[tok(na), tok(nbw - nkr * LANES), tr_kr, tok(nv), tok(LANES), tok(LANES),
                   tr, tr, tok(nd)],
        out_shape=[jax.ShapeDtypeStruct((T, na), BF16),
                   jax.ShapeDtypeStruct((T, nbw - nkr * LANES), BF16),
                   jax.ShapeDtypeStruct((B, nkr, S // LANES, LANES, LANES), BF16),
                   jax.ShapeDtypeStruct((T, nv), BF16),
                   jax.ShapeDtypeStruct((T, LANES), BF16),
                   jax.ShapeDtypeStruct((T, LANES), BF16),
                   jax.ShapeDtypeStruct((B, S // LANES, LANES, LANES), BF16),
                   jax.ShapeDtypeStruct((B, S // LANES, LANES, LANES), BF16),
                   jax.ShapeDtypeStruct((T, nd), BF16)],
        compiler_params=pltpu.CompilerParams(dimension_semantics=("parallel",),
                                             vmem_limit_bytes=VMEM_LIMIT),
        name="proj",
    )(x2d, gain.reshape(1, D), wa, wb, wc, wd, tab_a, tab_b)


def _rope_tables(pos, rot_dim, head_dim, theta):
    half = rot_dim // 2
    freqs = theta ** (-(np.arange(half, dtype=np.float64) * 2.0 / rot_dim))
    ang = np.asarray(pos, np.float64)[:, None] * freqs[None, :]
    cos, sin = np.cos(ang), np.sin(ang)
    n = ang.shape[0]
    rest = head_dim - rot_dim
    cos_h = np.concatenate([cos, cos, np.ones((n, rest))], axis=-1)
    sl_h = np.concatenate([-sin, np.zeros((n, half + rest))], axis=-1)
    sr_h = np.concatenate([np.zeros((n, half)), sin, np.zeros((n, rest))], axis=-1)
    rep = LANES // head_dim
    return jnp.asarray(np.stack([np.tile(t, (1, rep)) for t in (cos_h, sl_h, sr_h)]), F32)


def _gelu_tanh(x):
    return 0.5 * x * (1.0 + jnp.tanh(math.sqrt(2.0 / math.pi) * (x + 0.044715 * (x * x * x))))


def _compress_body(kt_ref, vt_ref, pe_ref, kw1_ref, kw2_ref, vw1_ref, vw2_ref, tab_ref,
                   kc_ref, vcT_ref):
    pe = pe_ref[...]
    n = kt_ref.shape[1]

    def mlp(tok_ref, w1_ref, w2_ref):
        c = tok_ref[0].astype(F32)
        lo = (c + pe[0:1]).astype(BF16)
        hi = (c + pe[1:2]).astype(BF16)
        out = jnp.zeros((n, LANES), F32)
        for g in range(NSA_GROUPS):
            top = jnp.dot(lo, w1_ref[g, 0], preferred_element_type=F32)
            bot = jnp.dot(hi, w1_ref[g, 1], preferred_element_type=F32)
            hid = _gelu_tanh(top + pltpu.roll(bot, n - 1, 0))
            out = out + jnp.dot(hid.astype(BF16), w2_ref[g], preferred_element_type=F32)
        return out

    kc_ref[0] = _rope_slab(mlp(kt_ref, kw1_ref, kw2_ref), tab_ref, ROPE_DIM // 2).astype(kc_ref.dtype)
    vc = mlp(vt_ref, vw1_ref, vw2_ref)
    for r in range(n // LANES):
        vcT_ref[0, :, r * LANES:(r + 1) * LANES] = vc[r * LANES:(r + 1) * LANES, :].T.astype(vcT_ref.dtype)


def _compress(k_tok, v_tok, pe, kw1, kw2, vw1, vw2, tab):
    B, n, w = k_tok.shape
    tok = pl.BlockSpec((1, n, w), lambda b: (b, 0, 0))
    return pl.pallas_call(
        _compress_body,
        grid=(B,),
        in_specs=[tok, tok, _resident(pe.shape), _resident(kw1.shape), _resident(kw2.shape),
                  _resident(vw1.shape), _resident(vw2.shape), _resident(tab.shape)],
        out_specs=[pl.BlockSpec((1, n, LANES), lambda b: (b, 0, 0)),
                   pl.BlockSpec((1, LANES, n), lambda b: (b, 0, 0))],
        out_shape=[jax.ShapeDtypeStruct((B, n, LANES), BF16), jax.ShapeDtypeStruct((B, LANES, n), BF16)],
        compiler_params=pltpu.CompilerParams(dimension_semantics=("parallel",),
                                             vmem_limit_bytes=VMEM_LIMIT),
        name="compress",
    )(k_tok, v_tok, pe, kw1, kw2, vw1, vw2, tab)


def _compress_weights(w1, w2, pe):
    G, dh = NSA_GROUPS, NSA_DH
    hid = w1.shape[1]
    halves = w1.reshape(2, CMP_STRIDE, dh, hid)
    w1e = jnp.zeros((G, 2, CMP_STRIDE, G, dh, hid), F32)
    w2e = jnp.zeros((G, hid, G, dh), F32)
    for g in range(G):
        w1e = w1e.at[g, :, :, g].set(halves)
        w2e = w2e.at[g, :, g].set(w2)
    pe_e = jnp.broadcast_to(pe.reshape(2, CMP_STRIDE, 1, dh), (2, CMP_STRIDE, G, dh))
    return (w1e.reshape(G, 2, CMP_STRIDE * G * dh, hid).astype(BF16), w2e.reshape(G, hid, G * dh).astype(BF16),
            pe_e.reshape(2, CMP_STRIDE * G * dh))


def _nsa_body(q_ref, gate_ref, kc_ref, vcT_ref, ks_ref, vsT_ref, kw_ref, vwT_ref, c2sT_ref, et_ref,
              o_ref, qT_scr, gT_scr, m_scr, acc_scr, rank_scr, *, kt, top_n, n_slc, hg, dh):
    qb = q_ref.shape[0]
    nh = q_ref.shape[1] // dh
    groups = nh // hg
    ncp = kc_ref.shape[1]
    c = pl.program_id(1)
    q0 = c * qb
    ones_rows = jnp.ones((BF16_ROWS, 1), BF16)
    grp_rows = lambda g: slice(g * dh, (g + 1) * dh)

    def v_tiles(vT_ref, k0, n):
        t = vT_ref[0, pl.ds(k0 // LANES, n // LANES)]
        vT = jnp.concatenate([t[i] for i in range(n // LANES)], axis=1)
        ones = jnp.broadcast_to(ones_rows, (BF16_ROWS, n))
        return [jnp.concatenate([vT[grp_rows(g)], ones], axis=0) for g in range(groups)]

    def tq(n):
        return q0 + lax.broadcasted_iota(jnp.int32, (n, qb), 1)

    def krow(n, k0=0):
        return k0 + lax.broadcasted_iota(jnp.int32, (n, qb), 0)

    qT_scr[...] = jnp.zeros(qT_scr.shape, BF16)
    for r in range(qb // LANES):
        rows = slice(r * LANES, (r + 1) * LANES)
        for j in range(nh // 2):
            t = q_ref[rows, j * LANES:(j + 1) * LANES].astype(F32).T.astype(BF16)
            for half in range(2):
                h = 2 * j + half
                qT_scr[grp_rows(h // hg), h * qb + r * LANES:h * qb + (r + 1) * LANES] = (
                    t[half * dh:(half + 1) * dh])
        gT_scr[:, rows] = jax.nn.sigmoid(gate_ref[rows, :LANES].astype(F32).T)

    qT = qT_scr[:LANES, :]

    span = WINDOW + LANES

    def window_tile(r):
        t0 = q0 + r * LANES
        w0 = pl.multiple_of(jnp.maximum(t0 - WINDOW, 0), LANES)
        dist = (t0 + lax.broadcasted_iota(jnp.int32, (span, LANES), 1)
                - (w0 + lax.broadcasted_iota(jnp.int32, (span, LANES), 0)))
        bias_w = jnp.where((dist >= 0) & (dist < WINDOW), 0.0, NEG_INF)
        bias_w = jnp.concatenate([bias_w, bias_w], axis=1)
        qT_r = jnp.concatenate([qT[:, h * qb + r * LANES:h * qb + (r + 1) * LANES] for h in range(nh)], axis=1)
        s_w = jnp.dot(kw_ref[0, pl.ds(w0, span), :], qT_r, preferred_element_type=F32)
        vwT = v_tiles(vwT_ref, w0, span)
        outs_r = []
        for hp in range(nh // 2):
            sh = s_w[:, hp * 2 * LANES:(hp + 1) * 2 * LANES] + bias_w
            p = jnp.exp2(sh - jnp.max(sh, axis=0, keepdims=True)).astype(BF16)
            acc_w = jnp.dot(vwT[2 * hp // hg], p, preferred_element_type=F32)
            ow = acc_w[:dh] * (1.0 / acc_w[dh:dh + 1])
            outs_r += [ow[:, :LANES], ow[:, LANES:]]
        return outs_r

    n_lt = qb // LANES
    o_w = [window_tile(r) for r in range(n_lt // 2)]

    s_c = jnp.dot(kc_ref[0], qT, preferred_element_type=F32)
    cend = krow(ncp) * CMP_STRIDE + (CMP_BLOCK - 1)
    bias_c = jnp.where(cend <= tq(ncp), 0.0, NEG_INF)
    bias_c = jnp.concatenate([bias_c, bias_c], axis=1)
    nb = dh
    vcT = vcT_ref[0]
    lhs_c = [jnp.concatenate([vcT[grp_rows(g)], jnp.broadcast_to(ones_rows, (BF16_ROWS, ncp)),
                              c2sT_ref[:nb, :]], axis=0) for g in range(groups)]
    imps = [jnp.zeros((nb, qb), F32) for _ in range(groups)]
    o_c = []
    for hp in range(nh // 2):
        g = 2 * hp // hg
        sh = s_c[:, hp * 2 * qb:(hp + 1) * 2 * qb] + bias_c
        e = jnp.exp2(sh - jnp.max(sh, axis=0, keepdims=True)).astype(BF16)
        r = jnp.dot(lhs_c[g], e, preferred_element_type=F32)
        inv_l = 1.0 / r[dh:dh + 1]
        o_c.append(r[:dh] * inv_l)
        w = r[dh + BF16_ROWS:] * inv_l
        imps[g] = imps[g] + w[:, :qb] + w[:, qb:]
    o_c = jnp.concatenate(o_c, axis=1)
    sb = krow(nb)
    cur = jnp.right_shift(tq(nb), int(math.log2(SLC_BLOCK)))
    forced = (sb == 0) | (sb == cur) | (sb == cur - 1)
    future = sb > cur
    imps = [jnp.where(forced, FORCED_SCORE, jnp.where(future, -FORCED_SCORE, imp)) for imp in imps]
    sub = 8
    rank_scr[...] = jnp.zeros(rank_scr.shape, jnp.int32)
    last_block = (q0 + qb - 1) // SLC_BLOCK
    sb_l = sb[:, :LANES]
    for grp in range(n_slc // sub):
        @pl.when(grp * sub <= last_block)
        def _():
            for g in range(groups):
                for lc in range(qb // LANES):
                    imp_l = imps[g][:, lc * LANES:(lc + 1) * LANES]
                    chunks = [imp_l[r * sub:(r + 1) * sub] for r in range(nb // sub)]
                    ranks = [jnp.zeros((sub, LANES), jnp.int32) for _ in chunks]
                    for sp in range(grp * sub, (grp + 1) * sub):
                        row = imp_l[sp:sp + 1, :]
                        for r, blk in enumerate(chunks):
                            if r * sub > sp:
                                beats = row >= blk
                            elif (r + 1) * sub - 1 <= sp:
                                beats = row > blk
                            else:
                                beats = (row > blk) | ((row == blk) & (sb_l[r * sub:(r + 1) * sub] > sp))
                            ranks[r] = ranks[r] + jnp.where(beats, 1, 0)
                    rank_scr[g, :, lc * LANES:(lc + 1) * LANES] += jnp.concatenate(ranks, axis=0)
    for g in range(groups):
        selneg = jnp.where((rank_scr[g] < top_n) & jnp.logical_not(future), 0.0, NEG_INF).astype(BF16)
        for h in range(g * hg, (g + 1) * hg):
            qT_scr[LANES:LANES + nb, h * qb:(h + 1) * qb] = selneg

    qT_aug = qT_scr[...]

    def scores(j):
        k0 = pl.multiple_of(j * kt, kt)
        ka = jnp.concatenate([ks_ref[0, pl.ds(k0, kt), :], et_ref[pl.ds(k0, kt), :]], axis=1)
        return jnp.dot(ka, qT_aug, preferred_element_type=F32)

    def fold(s, j, causal):
        vT = v_tiles(vsT_ref, pl.multiple_of(j * kt, kt), kt)
        if causal:
            bias = jnp.where(krow(kt, j * kt) <= tq(kt), 0.0, NEG_INF)
            bias = jnp.concatenate([bias, bias], axis=1)
        for hp in range(nh // 2):
            sl = slice(hp * 2 * qb, (hp + 1) * 2 * qb)
            sh = s[:, sl] + bias if causal else s[:, sl]
            m_old = m_scr[:, sl]
            m_new = jnp.maximum(m_old, jnp.max(sh, axis=0, keepdims=True))
            pv = jnp.dot(vT[2 * hp // hg], jnp.exp2(sh - m_new).astype(BF16), preferred_element_type=F32)
            acc_scr[:, sl] = jnp.exp2(m_old - m_new) * acc_scr[:, sl] + pv
            m_scr[:, sl] = m_new

    m_scr[...] = jnp.full(m_scr.shape, NEG_INF, F32)
    acc_scr[...] = jnp.zeros(acc_scr.shape, F32)

    def run_tiles(j0, n):
        ss = [scores(j0 + t) for t in range(n)]
        for t in range(n):
            fold(ss[t], j0 + t, False)

    def main_pair(i, carry):
        run_tiles(2 * i, 2)
        return carry

    lax.fori_loop(0, c // 2, main_pair, 0)

    @pl.when(c % 2 == 1)
    def _():
        fold(scores(c - 1), c - 1, False)

    fold(scores(c), c, True)
    o_s = acc_scr[:dh, :] * (1.0 / acc_scr[dh:dh + 1, :])

    o_w += [window_tile(r) for r in range(n_lt // 2, n_lt)]
    o_w = jnp.concatenate([o_w[r][h] for h in range(nh) for r in range(n_lt)], axis=1)

    has_cmp = (tq(1) >= CMP_BLOCK - 1).astype(F32)
    gts = gT_scr[:3 * nh, :]
    outs = []
    for h in range(nh):
        sl = slice(h * qb, (h + 1) * qb)
        outs.append(gts[3 * h:3 * h + 1] * has_cmp * o_c[:, sl] + gts[3 * h + 1:3 * h + 2] * o_s[:, sl]
                    + gts[3 * h + 2:3 * h + 3] * o_w[:, sl])
    for j in range(nh // 2):
        pair = jnp.concatenate([outs[2 * j], outs[2 * j + 1]], axis=0)
        for r in range(qb // LANES):
            rows = slice(r * LANES, (r + 1) * LANES)
            o_ref[rows, j * LANES:(j + 1) * LANES] = pair[:, rows].T.astype(o_ref.dtype)


def _nsa(qk, gates, kc, vcT, vsT, vwT, c2sT, et, B, S, *, qb=256):
    G, Hg, dh = NSA_GROUPS, NSA_HG, NSA_DH
    T = B * S
    nq = S // qb
    n_slc = S // SLC_BLOCK
    kt = qb
    assert n_slc <= dh and 2 * dh == LANES and qb % LANES == 0 and S % qb == 0 and WINDOW % qb == 0
    assert S >= WINDOW + qb
    qw = G * Hg * dh
    assert 3 * G * Hg <= LANES and G * dh == LANES
    ncp = kc.shape[1]
    k_col = qw // LANES
    g_col = gates.shape[1] // MXU_N - 1
    seq_k = lambda col: pl.BlockSpec((1, S, LANES), lambda b, c: (b, 0, col))
    seq_vT = pl.BlockSpec((1, S // LANES, LANES, LANES), lambda b, c: (b, 0, 0, 0))
    qk3 = qk.reshape(B, S, qk.shape[1])
    cols = G * Hg * qb
    return pl.pallas_call(
        functools.partial(_nsa_body, kt=kt, top_n=min(SLC_TOPN, n_slc), n_slc=n_slc, hg=Hg, dh=dh),
        grid=(B, nq),
        in_specs=[pl.BlockSpec((qb, qw), lambda b, c: (b * nq + c, 0)),
                  pl.BlockSpec((qb, MXU_N), lambda b, c: (b * nq + c, g_col)),
                  pl.BlockSpec((1, ncp, LANES), lambda b, c: (b, 0, 0)),
                  pl.BlockSpec((1, LANES, ncp), lambda b, c: (b, 0, 0)),
                  seq_k(k_col), seq_vT, seq_k(k_col + 1), seq_vT,
                  _resident(c2sT.shape), _resident(et.shape)],
        out_specs=pl.BlockSpec((qb, qw), lambda b, c: (b * nq + c, 0)),
        out_shape=jax.ShapeDtypeStruct((T, qw), BF16),
        scratch_shapes=[pltpu.VMEM((2 * LANES, cols), BF16), pltpu.VMEM((LANES, qb), F32),
                        pltpu.VMEM((1, cols), F32), pltpu.VMEM((dh + BF16_ROWS, cols), F32),
                        pltpu.VMEM((G, dh, qb), jnp.int32)],
        compiler_params=pltpu.CompilerParams(dimension_semantics=("parallel", "arbitrary"),
                                             vmem_limit_bytes=VMEM_LIMIT),
        name="nsa",
    )(qk, gates, kc, vcT, qk3, vsT, qk3, vwT, c2sT, et)


def _nsa_constants(S):
    ncp = S // CMP_STRIDE
    n_slc = S // SLC_BLOCK
    cmp_start = np.arange(ncp) * CMP_STRIDE
    s_start = np.arange(n_slc) * SLC_BLOCK
    overlap = np.clip(np.minimum(cmp_start[:, None] + CMP_BLOCK, s_start[None, :] + SLC_BLOCK)
                      - np.maximum(cmp_start[:, None], s_start[None, :]), 0, None)
    c2sT = np.zeros((LANES, ncp), np.float32)
    c2sT[:n_slc, :] = overlap.T.astype(np.float32) / CMP_BLOCK
    c2sT[:, ncp - 1] = 0.0
    et = np.zeros((S, LANES), np.float32)
    et[np.arange(S), np.arange(S) // SLC_BLOCK] = 1.0
    return jnp.asarray(c2sT, BF16), jnp.asarray(et, BF16)


def _ret_body(q_ref, kT_ref, v_ref, decay_ref, zeta_ref, xi_ref, gch_ref, gn_ref, o_ref, *, C, dk):
    S = q_ref.shape[1]
    dv = gn_ref.shape[2]
    heads = LANES // dk
    N = S // C
    units = [(n, a) for n in range(N) for a in range(heads)]
    rows = lambda n: slice(n * C, (n + 1) * C)
    val = lambda n, a: v_ref[0, rows(n), a * dv:(a + 1) * dv]
    qp = lambda n: q_ref[0, rows(n), :]

    def own_rows(x, a):
        return jnp.concatenate([x if b == a else jnp.zeros_like(x) for b in range(heads)], axis=0)

    kT = {(n, a): kT_ref[0, 0, n, a * dk:(a + 1) * dk, :] for n, a in units}
    inner = {u: jnp.dot(qp(u[0]), own_rows(kT[u], u[1]), preferred_element_type=F32) for u in units}
    inner = {u: (inner[u] * decay_ref[u[1]]).astype(BF16) for u in units}
    o = {u: jnp.dot(inner[u], val(*u), preferred_element_type=F32) for u in units}
    kv = {(n, a): jnp.dot((kT[(n, a)].astype(F32) * zeta_ref[a]).astype(BF16), val(n, a),
                          preferred_element_type=F32) for n, a in units}
    state = {}
    for a in range(heads):
        R = jnp.zeros((dk, dv), F32)
        for n in range(N):
            state[(n, a)] = own_rows(R.astype(BF16), a)
            R = gch_ref[a] * R + kv[(n, a)]
    o = {u: o[u] + xi_ref[u[1]] * jnp.dot(qp(u[0]), state[u], preferred_element_type=F32) for u in units}
    mu = {u: jnp.mean(o[u], axis=-1, keepdims=True) for u in units}
    d = {u: o[u] - mu[u] for u in units}
    var = {u: jnp.mean(d[u] * d[u], axis=-1, keepdims=True) for u in units}
    for n, a in units:
        o_ref[0, rows(n), a * dv:(a + 1) * dv] = (d[(n, a)] * lax.rsqrt(var[(n, a)] + GN_EPS)
                                                  * gn_ref[a]).astype(o_ref.dtype)


def _retention(qk, kT, v, gn_gain, B, S, *, C=RET_CHUNK):
    H, dk, dv = RET_HEADS, RET_DK, RET_DV
    hp = LANES // dk
    assert C == LANES and dv == LANES
    log_g = np.log(1.0 - 2.0 ** (-5.0 - np.arange(H, dtype=np.float64)))
    i = np.arange(C, dtype=np.float64)
    diff = i[:, None] - i[None, :]
    const = lambda a: jnp.asarray(a, F32)
    decay = const(np.where(diff >= 0, np.exp(np.maximum(diff, 0.0) * log_g[:, None, None]), 0.0))
    zeta = const(np.exp((C - 1.0 - i)[None, :] * log_g[:, None]).reshape(H, 1, C))
    xi = const(np.broadcast_to(np.exp((i + 1.0)[None, :] * log_g[:, None])[:, :, None], (H, C, LANES)))
    gch = const(np.broadcast_to(np.exp(C * log_g)[:, None, None], (H, 1, dv)))
    per_pair = lambda shape: pl.BlockSpec((hp,) + shape, lambda b, p: (p, 0, 0))
    qk3 = qk.reshape(B, S, qk.shape[1])
    return pl.pallas_call(
        functools.partial(_ret_body, C=C, dk=dk),
        grid=(B, H // hp),
        in_specs=[pl.BlockSpec((1, S, LANES), lambda b, p: (b, 0, p)),
                  pl.BlockSpec((1, 1, S // LANES, LANES, LANES), lambda b, p: (b, p, 0, 0, 0)),
                  pl.BlockSpec((1, S, hp * dv), lambda b, p: (b, 0, p)),
                  per_pair((C, C)), per_pair((1, C)), per_pair((C, LANES)), per_pair((1, dv)),
                  per_pair((1, dv))],
        out_specs=pl.BlockSpec((1, S, hp * dv), lambda b, p: (b, 0, p)),
        out_shape=jax.ShapeDtypeStruct((B, S, H * dv), BF16),
        compiler_params=pltpu.CompilerParams(dimension_semantics=("parallel", "parallel"),
                                             vmem_limit_bytes=VMEM_LIMIT),
        name="retention",
    )(qk3, kT, v.reshape(B, S, H * dv), decay, zeta, xi, gch, gn_gain.reshape(H, 1, dv))


def _merge_body(x_ref, a_ref, r_ref, gr_ref, ga_ref, gb_ref, wa_ref, wr_ref, wo_ref, o_ref):
    g = gr_ref[...].astype(F32)
    r = (g * jax.nn.sigmoid(g) * r_ref[...].astype(F32)).astype(BF16)
    ya = jnp.dot(a_ref[...], wa_ref[...], preferred_element_type=F32)
    yr = jnp.dot(r, wr_ref[...], preferred_element_type=F32)
    mixed = jax.nn.sigmoid(ga_ref[...].astype(F32)) * ya + jax.nn.sigmoid(gb_ref[...].astype(F32)) * yr
    o_ref[...] = x_ref[...] + jnp.dot(mixed.astype(BF16), wo_ref[...], preferred_element_type=F32)


def _merge(x2d, a, r, gates, wa, wr, wo, *, tm=512):
    T, D = x2d.shape
    col = lambda j: pl.BlockSpec((tm, D), lambda i: (i, j))
    return pl.pallas_call(
        _merge_body,
        grid=(T // tm,),
        in_specs=[col(0), col(0), col(0), col(0), col(1), col(2),
                  _resident(wa.shape), _resident(wr.shape), _resident(wo.shape)],
        out_specs=col(0),
        out_shape=jax.ShapeDtypeStruct((T, D), F32),
        compiler_params=pltpu.CompilerParams(dimension_semantics=("parallel",),
                                             vmem_limit_bytes=VMEM_LIMIT),
        name="merge",
    )(x2d, a, r, gates, gates, gates, wa, wr, wo)


def _layer(x, p, final_norm):
    B, S, D = x.shape
    T = B * S
    G, Hg, dh = NSA_GROUPS, NSA_HG, NSA_DH
    H, dk, dv = RET_HEADS, RET_DK, RET_DV
    bf = lambda w: w.astype(BF16)

    x1 = _ffn(x.reshape(T, D), p["ffn1_norm"], bf(p["ffn1_w_gate"]), bf(p["ffn1_w_up"]),
              bf(p["ffn1_w_down"]), final_norm, final=False)

    w = p["w_in"].T
    widths = (G * Hg * dh,) + (G * dh,) * 6 + (3 * G * Hg, H * dk, H * dk, H * dv, H * dv, 2 * D)
    offs = np.concatenate([[0], np.cumsum(widths)])
    seg = lambda i: w[int(offs[i]):int(offs[i + 1])]
    (q_a, kc_a, vc_a, ks_a, vs_a, kw_a, vw_a, g_a, q_r, k_r, v_r, g_r, g_m) = [seg(i) for i in range(13)]
    gate_pad = MXU_N - g_a.shape[0]
    wa = bf(jnp.concatenate([q_a * (dh ** -0.5 * LOG2E), ks_a, kw_a], axis=0))
    wb = bf(jnp.concatenate([q_r * dk ** -0.5, k_r], axis=0))
    wc = bf(jnp.concatenate([v_r, kc_a, vc_a, vs_a, vw_a], axis=0))
    wd = bf(jnp.concatenate([g_r, g_m, g_a, jnp.zeros((gate_pad, D), F32)], axis=0))
    pos = np.arange(S)
    tab_a = _rope_tables(pos, ROPE_DIM, dh, ROPE_THETA)
    tab_b = _rope_tables(pos, dk, dk, RET_ROPE_THETA)
    qk_n, qk_r, krT, v_ret, kc_tok, vc_tok, vsT, vwT, gates = _proj(
        x1, p["mix_norm"], wa, wb, wc, wd, tab_a, tab_b, B, S)

    chunk = lambda t: t.reshape(B, S // CMP_STRIDE, CMP_STRIDE * G * dh)
    kw1, kw2, pe = _compress_weights(p["cmp_k_w1"], p["cmp_k_w2"], p["cmp_pos_emb"])
    vw1, vw2, _ = _compress_weights(p["cmp_v_w1"], p["cmp_v_w2"], p["cmp_pos_emb"])
    cmp_end = np.arange(S // CMP_STRIDE) * CMP_STRIDE + (CMP_BLOCK - 1)
    kc, vcT = _compress(chunk(kc_tok), chunk(vc_tok), pe, kw1, kw2, vw1, vw2,
                        _rope_tables(cmp_end, ROPE_DIM, dh, ROPE_THETA))
    c2sT, et = _nsa_constants(S)
    a_out = _nsa(qk_n, gates, kc, vcT, vsT, vwT, c2sT, et, B, S)
    r_out = _retention(qk_r, krT, v_ret, p["ret_gn_gain"], B, S).reshape(T, H * dv)

    x2 = _merge(x1, a_out, r_out, gates, bf(p["w_branch_nsa"]), bf(p["w_branch_ret"]), bf(p["w_out"]))
    x3 = _ffn(x2, p["ffn2_norm"], bf(p["ffn2_w_gate"]), bf(p["ffn2_w_up"]), bf(p["ffn2_w_down"]),
              final_norm, final=True)
    return x3.reshape(B, S, D)


def kernel(x, ffn1_norm, ffn1_w_gate, ffn1_w_up, ffn1_w_down, mix_norm, w_in, cmp_pos_emb, cmp_k_w1,
           cmp_k_w2, cmp_v_w1, cmp_v_w2, ret_gn_gain, w_branch_nsa, w_branch_ret, w_out, ffn2_norm,
           ffn2_w_gate, ffn2_w_up, ffn2_w_down, final_norm):
    assert ffn1_norm.shape[0] == 1, "single-layer stack"
    names = ("ffn1_norm", "ffn1_w_gate", "ffn1_w_up", "ffn1_w_down", "mix_norm", "w_in", "cmp_pos_emb",
             "cmp_k_w1", "cmp_k_w2", "cmp_v_w1", "cmp_v_w2", "ret_gn_gain", "w_branch_nsa",
             "w_branch_ret", "w_out", "ffn2_norm", "ffn2_w_gate", "ffn2_w_up", "ffn2_w_down")
    vals = (ffn1_norm, ffn1_w_gate, ffn1_w_up, ffn1_w_down, mix_norm, w_in, cmp_pos_emb, cmp_k_w1,
            cmp_k_w2, cmp_v_w1, cmp_v_w2, ret_gn_gain, w_branch_nsa, w_branch_ret, w_out, ffn2_norm,
            ffn2_w_gate, ffn2_w_up, ffn2_w_down)
    p = {n: v[0] for n, v in zip(names, vals)}
    return _layer(x, p, final_norm)
```

```python
import functools
import math

import numpy as np
import jax
import jax.numpy as jnp
from jax import lax
from jax.experimental import pallas as pl
from jax.experimental.pallas import tpu as pltpu

F32 = jnp.float32
BF16 = jnp.bfloat16

NSA_HEADS = 16
NSA_GROUPS = 2
NSA_HG = NSA_HEADS // NSA_GROUPS
NSA_DH = 64
CMP_BLOCK = 32
CMP_STRIDE = 16
SLC_BLOCK = 64
SLC_TOPN = 16
WINDOW = 512
ROPE_THETA = 500000.0
ROPE_DIM = NSA_DH // 4
FORCED_SCORE = 1.0e4
RET_HEADS = 8
RET_DK = 64
RET_DV = 128
RET_CHUNK = 128
RET_ROPE_THETA = 10000.0
EPS = 1e-6
GN_EPS = 1e-5
NEG_INF = -1e30
LOG2E = math.log2(math.e)

LANES = 128
SUBLANES = 8
BF16_ROWS = 16
MXU_N = 256
VMEM_LIMIT = 56 * 1024 * 1024

NT_DIMS = (((1,), (1,)), ((), ()))


def _rms(x, g, eps=EPS):
    return x * lax.rsqrt(jnp.mean(x * x, axis=-1, keepdims=True) + eps) * g


def _resident(shape):
    nd = len(shape)
    return pl.BlockSpec(shape, lambda *_: (0,) * nd, pipeline_mode=pl.Buffered(1))


def _ffn_body(x_ref, g_ref, wg_ref, wu_ref, wd_ref, fg_ref, o_ref, *, cuts, final):
    x = x_ref[...]
    h = _rms(x, g_ref[...]).astype(BF16)
    acc = jnp.zeros(x.shape, F32)
    for lo, hi in zip(cuts[:-1], cuts[1:]):
        sl = slice(lo, hi)
        g = jnp.dot(h, wg_ref[:, sl], preferred_element_type=F32)
        u = jnp.dot(h, wu_ref[:, sl], preferred_element_type=F32)
        a = (g * jax.nn.sigmoid(g) * u).astype(BF16)
        acc = acc + jnp.dot(a, wd_ref[sl, :], preferred_element_type=F32)
    y = x + 0.5 * acc
    if final:
        y = _rms(y, fg_ref[...])
    o_ref[...] = y


def _ffn(x2d, gain, wg, wu, wd, final_gain, *, final, tm=512):
    T, D = x2d.shape
    F = wg.shape[1]
    assert F % MXU_N == 0
    n_tiles = F // MXU_N
    cuts = (0, (n_tiles + 1) // 2 * MXU_N, F)
    tok = pl.BlockSpec((tm, D), lambda i: (i, 0))
    return pl.pallas_call(
        functools.partial(_ffn_body, cuts=cuts, final=final),
        grid=(T // tm,),
        in_specs=[tok, _resident((1, D)), _resident((D, F)), _resident((D, F)), _resident((F, D)),
                  _resident((1, D))],
        out_specs=tok,
        out_shape=jax.ShapeDtypeStruct((T, D), F32),
        compiler_params=pltpu.CompilerParams(dimension_semantics=("parallel",),
                                             vmem_limit_bytes=VMEM_LIMIT),
        name="ffn_final" if final else "ffn",
    )(x2d, gain.reshape(1, D), wg, wu, wd, final_gain.reshape(1, D))


def _rope_slab(y, tab_ref, shift):
    return (y * tab_ref[0] + pltpu.roll(y, LANES - shift, 1) * tab_ref[1]
            + pltpu.roll(y, shift, 1) * tab_ref[2])


def _proj_body(x_ref, g_ref, wa_ref, wb_ref, wc_ref, wd_ref, ta_ref, tb_ref,
               oa_ref, ob_ref, okrT_ref, ov_ref, okc_ref, ovc_ref, ovsT_ref, ovwT_ref, od_ref):
    h = _rms(x_ref[...], g_ref[...]).astype(BF16)
    tm = h.shape[0]
    nv = ov_ref.shape[1] // LANES
    nkr = okrT_ref.shape[1]

    def slabs(w_ref):
        for c in range(w_ref.shape[0] // MXU_N):
            y = lax.dot_general(h, w_ref[c * MXU_N:(c + 1) * MXU_N, :], NT_DIMS, preferred_element_type=F32)
            for s in range(MXU_N // LANES):
                yield c * (MXU_N // LANES) + s, y[:, s * LANES:(s + 1) * LANES]

    def put(o_ref, i, ys):
        o_ref[:, i * LANES:(i + 1) * LANES] = ys.astype(o_ref.dtype)

    def put_t(o_ref, ys, *lead):
        for r in range(tm // LANES):
            o_ref[(0,) + lead + (r,)] = ys[r * LANES:(r + 1) * LANES, :].T.astype(o_ref.dtype)

    for i, ys in slabs(wa_ref):
        put(oa_ref, i, _rope_slab(ys, ta_ref, ROPE_DIM // 2))
    nb = wb_ref.shape[0] // LANES
    for i, ys in slabs(wb_ref):
        ys = _rope_slab(ys, tb_ref, RET_DK // 2)
        if i < nb - nkr:
            put(ob_ref, i, ys)
        else:
            put_t(okrT_ref, ys, i - (nb - nkr))
    for i, ys in slabs(wc_ref):
        if i < nv:
            put(ov_ref, i, ys)
        elif i == nv:
            okc_ref[...] = ys.reshape(tm // CMP_STRIDE, CMP_STRIDE * LANES).astype(okc_ref.dtype)
        elif i == nv + 1:
            ovc_ref[...] = ys.reshape(tm // CMP_STRIDE, CMP_STRIDE * LANES).astype(ovc_ref.dtype)
        elif i == nv + 2:
            put_t(ovsT_ref, ys)
        else:
            put_t(ovwT_ref, ys)
    for i, ys in slabs(wd_ref):
        put(od_ref, i, ys)


def _proj(x2d, gain, wa, wb, wc, wd, tab_a, tab_b, B, S, *, tm=512):
    T, D = x2d.shape
    spt = S // tm
    na, nbw, nd = wa.shape[0], wb.shape[0], wd.shape[0]
    nv = wc.shape[0] - 4 * LANES
    tok = lambda n: pl.BlockSpec((tm, n), lambda i: (i, 0))
    tab = pl.BlockSpec((3, tm, LANES), lambda i: (0, i % spt, 0))
    tr = pl.BlockSpec((1, tm // LANES, LANES, LANES), lambda i: (i // spt, i % spt, 0, 0))
    chunked = pl.BlockSpec((tm // CMP_STRIDE, CMP_STRIDE * LANES), lambda i: (i, 0))
    nkr = RET_HEADS * RET_DK // LANES
    tr_kr = pl.BlockSpec((1, nkr, tm // LANES, LANES, LANES), lambda i: (i // spt, 0, i % spt, 0, 0))
    return pl.pallas_call(
        _proj_body,
        grid=(T // tm,),
        in_specs=[tok(D), _resident((1, D)), _resident(wa.shape), _resident(wb.shape),
                  _resident(wc.shape), _resident(wd.shape), tab, tab],
        out_specs=[tok(na), tok(nbw - nkr * LANES), tr_kr, tok(nv), chunked, chunked,
                   tr, tr, tok(nd)],
        out_shape=[jax.ShapeDtypeStruct((T, na), BF16),
                   jax.ShapeDtypeStruct((T, nbw - nkr * LANES), BF16),
                   jax.ShapeDtypeStruct((B, nkr, S // LANES, LANES, LANES), BF16),
                   jax.ShapeDtypeStruct((T, nv), BF16),
                   jax.ShapeDtypeStruct((T // CMP_STRIDE, CMP_STRIDE * LANES), BF16),
                   jax.ShapeDtypeStruct((T // CMP_STRIDE, CMP_STRIDE * LANES), BF16),
                   jax.ShapeDtypeStruct((B, S // LANES, LANES, LANES), BF16),
                   jax.ShapeDtypeStruct((B, S // LANES, LANES, LANES), BF16),
                   jax.ShapeDtypeStruct((T, nd), BF16)],
        compiler_params=pltpu.CompilerParams(dimension_semantics=("parallel",),
                                             vmem_limit_bytes=VMEM_LIMIT),
        name="proj",
    )(x2d, gain.reshape(1, D), wa, wb, wc, wd, tab_a, tab_b)


def _rope_tables(pos, rot_dim, head_dim, theta):
    half = rot_dim // 2
    freqs = theta ** (-(np.arange(half, dtype=np.float64) * 2.0 / rot_dim))
    ang = np.asarray(pos, np.float64)[:, None] * freqs[None, :]
    cos, sin = np.cos(ang), np.sin(ang)
    n = ang.shape[0]
    rest = head_dim - rot_dim
    cos_h = np.concatenate([cos, cos, np.ones((n, rest))], axis=-1)
    sl_h = np.concatenate([-sin, np.zeros((n, half + rest))], axis=-1)
    sr_h = np.concatenate([np.zeros((n, half)), sin, np.zeros((n, rest))], axis=-1)
    rep = LANES // head_dim
    return jnp.asarray(np.stack([np.tile(t, (1, rep)) for t in (cos_h, sl_h, sr_h)]), F32)


def _gelu_tanh(x):
    return 0.5 * x * (1.0 + jnp.tanh(math.sqrt(2.0 / math.pi) * (x + 0.044715 * (x * x * x))))


def _compress_body(kt_ref, vt_ref, pe_ref, kw1_ref, kw2_ref, vw1_ref, vw2_ref, tab_ref,
                   kc_ref, vcT_ref):
    pe = pe_ref[...]
    n = kt_ref.shape[1]

    def mlp(tok_ref, w1_ref, w2_ref):
        c = tok_ref[0].astype(F32)
        lo = (c + pe[0:1]).astype(BF16)
        hi = (c + pe[1:2]).astype(BF16)
        out = jnp.zeros((n, LANES), F32)
        for g in range(NSA_GROUPS):
            top = jnp.dot(lo, w1_ref[g, 0], preferred_element_type=F32)
            bot = jnp.dot(hi, w1_ref[g, 1], preferred_element_type=F32)
            hid = _gelu_tanh(top + pltpu.roll(bot, n - 1, 0))
            out = out + jnp.dot(hid.astype(BF16), w2_ref[g], preferred_element_type=F32)
        return out

    kc_ref[0] = _rope_slab(mlp(kt_ref, kw1_ref, kw2_ref), tab_ref, ROPE_DIM // 2).astype(kc_ref.dtype)
    vc = mlp(vt_ref, vw1_ref, vw2_ref)
    for r in range(n // LANES):
        vcT_ref[0, :, r * LANES:(r + 1) * LANES] = vc[r * LANES:(r + 1) * LANES, :].T.astype(vcT_ref.dtype)


def _compress(k_tok, v_tok, pe, kw1, kw2, vw1, vw2, tab):
    B, n, w = k_tok.shape
    tok = pl.BlockSpec((1, n, w), lambda b: (b, 0, 0))
    return pl.pallas_call(
        _compress_body,
        grid=(B,),
        in_specs=[tok, tok, _resident(pe.shape), _resident(kw1.shape), _resident(kw2.shape),
                  _resident(vw1.shape), _resident(vw2.shape), _resident(tab.shape)],
        out_specs=[pl.BlockSpec((1, n, LANES), lambda b: (b, 0, 0)),
                   pl.BlockSpec((1, LANES, n), lambda b: (b, 0, 0))],
        out_shape=[jax.ShapeDtypeStruct((B, n, LANES), BF16), jax.ShapeDtypeStruct((B, LANES, n), BF16)],
        compiler_params=pltpu.CompilerParams(dimension_semantics=("parallel",),
                                             vmem_limit_bytes=VMEM_LIMIT),
        name="compress",
    )(k_tok, v_tok, pe, kw1, kw2, vw1, vw2, tab)


def _compress_weights(w1, w2, pe):
    G, dh = NSA_GROUPS, NSA_DH
    hid = w1.shape[1]
    halves = w1.reshape(2, CMP_STRIDE, dh, hid)
    w1e = jnp.zeros((G, 2, CMP_STRIDE, G, dh, hid), F32)
    w2e = jnp.zeros((G, hid, G, dh), F32)
    for g in range(G):
        w1e = w1e.at[g, :, :, g].set(halves)
        w2e = w2e.at[g, :, g].set(w2)
    pe_e = jnp.broadcast_to(pe.reshape(2, CMP_STRIDE, 1, dh), (2, CMP_STRIDE, G, dh))
    return (w1e.reshape(G, 2, CMP_STRIDE * G * dh, hid).astype(BF16), w2e.reshape(G, hid, G * dh).astype(BF16),
            pe_e.reshape(2, CMP_STRIDE * G * dh))


def _nsa_body(q_ref, gate_ref, kc_ref, vcT_ref, ks_ref, vsT_ref, kw_ref, vwT_ref, c2sT_ref, et_ref,
              o_ref, qT_scr, gT_scr, m_scr, acc_scr, rank_scr, *, kt, top_n, n_slc, hg, dh):
    qb = q_ref.shape[0]
    nh = q_ref.shape[1] // dh
    groups = nh // hg
    ncp = kc_ref.shape[1]
    c = pl.program_id(1)
    q0 = c * qb
    ones_rows = jnp.ones((BF16_ROWS, 1), BF16)
    grp_rows = lambda g: slice(g * dh, (g + 1) * dh)

    def v_tiles(vT_ref, k0, n):
        t = vT_ref[0, pl.ds(k0 // LANES, n // LANES)]
        vT = jnp.concatenate([t[i] for i in range(n // LANES)], axis=1)
        ones = jnp.broadcast_to(ones_rows, (BF16_ROWS, n))
        return [jnp.concatenate([vT[grp_rows(g)], ones], axis=0) for g in range(groups)]

    def tq(n):
        return q0 + lax.broadcasted_iota(jnp.int32, (n, qb), 1)

    def krow(n, k0=0):
        return k0 + lax.broadcasted_iota(jnp.int32, (n, qb), 0)

    qT_scr[...] = jnp.zeros(qT_scr.shape, BF16)
    for r in range(qb // LANES):
        rows = slice(r * LANES, (r + 1) * LANES)
        for j in range(nh // 2):
            t = q_ref[rows, j * LANES:(j + 1) * LANES].astype(F32).T.astype(BF16)
            for half in range(2):
                h = 2 * j + half
                qT_scr[grp_rows(h // hg), h * qb + r * LANES:h * qb + (r + 1) * LANES] = (
                    t[half * dh:(half + 1) * dh])
        gT_scr[:, rows] = jax.nn.sigmoid(gate_ref[rows, :LANES].astype(F32).T)

    qT = qT_scr[:LANES, :]

    span = WINDOW + LANES

    def window_tile(r):
        t0 = q0 + r * LANES
        w0 = pl.multiple_of(jnp.maximum(t0 - WINDOW, 0), LANES)
        dist = (t0 + lax.broadcasted_iota(jnp.int32, (span, LANES), 1)
                - (w0 + lax.broadcasted_iota(jnp.int32, (span, LANES), 0)))
        bias_w = jnp.where((dist >= 0) & (dist < WINDOW), 0.0, NEG_INF)
        bias_w = jnp.concatenate([bias_w, bias_w], axis=1)
        qT_r = jnp.concatenate([qT[:, h * qb + r * LANES:h * qb + (r + 1) * LANES] for h in range(nh)], axis=1)
        s_w = jnp.dot(kw_ref[0, pl.ds(w0, span), :], qT_r, preferred_element_type=F32)
        vwT = v_tiles(vwT_ref, w0, span)
        outs_r = []
        for hp in range(nh // 2):
            sh = s_w[:, hp * 2 * LANES:(hp + 1) * 2 * LANES] + bias_w
            p = jnp.exp2(sh - jnp.max(sh, axis=0, keepdims=True)).astype(BF16)
            acc_w = jnp.dot(vwT[2 * hp // hg], p, preferred_element_type=F32)
            ow = acc_w[:dh] * (1.0 / acc_w[dh:dh + 1])
            outs_r += [ow[:, :LANES], ow[:, LANES:]]
        return outs_r

    n_lt = qb // LANES
    o_w = [window_tile(r) for r in range(n_lt // 2)]

    s_c = jnp.dot(kc_ref[0], qT, preferred_element_type=F32)
    cend = krow(ncp) * CMP_STRIDE + (CMP_BLOCK - 1)
    bias_c = jnp.where(cend <= tq(ncp), 0.0, NEG_INF)
    bias_c = jnp.concatenate([bias_c, bias_c], axis=1)
    nb = dh
    vcT = vcT_ref[0]
    lhs_c = [jnp.concatenate([vcT[grp_rows(g)], jnp.broadcast_to(ones_rows, (BF16_ROWS, ncp)),
                              c2sT_ref[:nb, :]], axis=0) for g in range(groups)]
    imps = [jnp.zeros((nb, qb), F32) for _ in range(groups)]
    o_c = []
    for hp in range(nh // 2):
        g = 2 * hp // hg
        sh = s_c[:, hp * 2 * qb:(hp + 1) * 2 * qb] + bias_c
        e = jnp.exp2(sh - jnp.max(sh, axis=0, keepdims=True)).astype(BF16)
        r = jnp.dot(lhs_c[g], e, preferred_element_type=F32)
        inv_l = 1.0 / r[dh:dh + 1]
        o_c.append(r[:dh] * inv_l)
        w = r[dh + BF16_ROWS:] * inv_l
        imps[g] = imps[g] + w[:, :qb] + w[:, qb:]
    o_c = jnp.concatenate(o_c, axis=1)
    sb = krow(nb)
    cur = jnp.right_shift(tq(nb), int(math.log2(SLC_BLOCK)))
    forced = (sb == 0) | (sb == cur) | (sb == cur - 1)
    future = sb > cur
    imps = [jnp.where(forced, FORCED_SCORE, jnp.where(future, -FORCED_SCORE, imp)) for imp in imps]
    sub = SUBLANES
    rank_scr[...] = jnp.zeros(rank_scr.shape, jnp.int32)
    last_block = (q0 + qb - 1) // SLC_BLOCK
    sb_l = sb[:, :LANES]
    for grp in range(n_slc // sub):
        @pl.when(grp * sub <= last_block)
        def _():
            for g in range(groups):
                for lc in range(qb // LANES):
                    imp_l = imps[g][:, lc * LANES:(lc + 1) * LANES]
                    chunks = [imp_l[r * sub:(r + 1) * sub] for r in range(nb // sub)]
                    ranks = [jnp.zeros((sub, LANES), jnp.int32) for _ in chunks]
                    for sp in range(grp * sub, (grp + 1) * sub):
                        row = imp_l[sp:sp + 1, :]
                        for r, blk in enumerate(chunks):
                            if r * sub > sp:
                                beats = row >= blk
                            elif (r + 1) * sub - 1 <= sp:
                                beats = row > blk
                            else:
                                beats = (row > blk) | ((row == blk) & (sb_l[r * sub:(r + 1) * sub] > sp))
                            ranks[r] = ranks[r] + jnp.where(beats, 1, 0)
                    rank_scr[g, :, lc * LANES:(lc + 1) * LANES] += jnp.concatenate(ranks, axis=0)
    for g in range(groups):
        selneg = jnp.where((rank_scr[g] < top_n) & jnp.logical_not(future), 0.0, NEG_INF).astype(BF16)
        for h in range(g * hg, (g + 1) * hg):
            qT_scr[LANES:LANES + nb, h * qb:(h + 1) * qb] = selneg

    qT_aug = qT_scr[...]

    def scores(j):
        k0 = pl.multiple_of(j * kt, kt)
        ka = jnp.concatenate([ks_ref[0, pl.ds(k0, kt), :], et_ref[pl.ds(k0, kt), :]], axis=1)
        return jnp.dot(ka, qT_aug, preferred_element_type=F32)

    def fold(s, j, causal):
        vT = v_tiles(vsT_ref, pl.multiple_of(j * kt, kt), kt)
        if causal:
            bias = jnp.where(krow(kt, j * kt) <= tq(kt), 0.0, NEG_INF)
            bias = jnp.concatenate([bias, bias], axis=1)
        for hp in range(nh // 2):
            sl = slice(hp * 2 * qb, (hp + 1) * 2 * qb)
            sh = s[:, sl] + bias if causal else s[:, sl]
            m_old = m_scr[:, sl]
            m_new = jnp.maximum(m_old, jnp.max(sh, axis=0, keepdims=True))
            pv = jnp.dot(vT[2 * hp // hg], jnp.exp2(sh - m_new).astype(BF16), preferred_element_type=F32)
            acc_scr[:, sl] = jnp.exp2(m_old - m_new) * acc_scr[:, sl] + pv
            m_scr[:, sl] = m_new

    m_scr[...] = jnp.full(m_scr.shape, NEG_INF, F32)
    acc_scr[...] = jnp.zeros(acc_scr.shape, F32)

    def run_tiles(j0, n):
        k0 = pl.multiple_of(j0 * kt, kt)
        ka = jnp.concatenate([ks_ref[0, pl.ds(k0, n * kt), :], et_ref[pl.ds(k0, n * kt), :]], axis=1)
        ss = jnp.dot(ka, qT_aug, preferred_element_type=F32)
        for t in range(n):
            fold(ss[t * kt:(t + 1) * kt], j0 + t, False)

    def main_pair(i, carry):
        run_tiles(2 * i, 2)
        return carry

    lax.fori_loop(0, c // 2, main_pair, 0)

    @pl.when(c % 2 == 1)
    def _():
        fold(scores(c - 1), c - 1, False)

    fold(scores(c), c, True)
    o_s = acc_scr[:dh, :] * (1.0 / acc_scr[dh:dh + 1, :])

    o_w += [window_tile(r) for r in range(n_lt // 2, n_lt)]
    o_w = jnp.concatenate([o_w[r][h] for h in range(nh) for r in range(n_lt)], axis=1)

    has_cmp = (tq(1) >= CMP_BLOCK - 1).astype(F32)
    gts = gT_scr[:3 * nh, :]
    outs = []
    for h in range(nh):
        sl = slice(h * qb, (h + 1) * qb)
        outs.append(gts[3 * h:3 * h + 1] * has_cmp * o_c[:, sl] + gts[3 * h + 1:3 * h + 2] * o_s[:, sl]
                    + gts[3 * h + 2:3 * h + 3] * o_w[:, sl])
    for j in range(nh // 2):
        pair = jnp.concatenate([outs[2 * j], outs[2 * j + 1]], axis=0)
        for r in range(qb // LANES):
            rows = slice(r * LANES, (r + 1) * LANES)
            o_ref[rows, j * LANES:(j + 1) * LANES] = pair[:, rows].T.astype(o_ref.dtype)


def _nsa(qk, gates, kc, vcT, vsT, vwT, c2sT, et, B, S, *, qb=256):
    G, Hg, dh = NSA_GROUPS, NSA_HG, NSA_DH
    T = B * S
    nq = S // qb
    n_slc = S // SLC_BLOCK
    kt = qb
    assert n_slc <= dh and 2 * dh == LANES and qb % LANES == 0 and S % qb == 0 and WINDOW % qb == 0
    assert S >= WINDOW + qb
    qw = G * Hg * dh
    assert 3 * G * Hg <= LANES and G * dh == LANES
    ncp = kc.shape[1]
    k_col = qw // LANES
    g_col = gates.shape[1] // MXU_N - 1
    seq_k = lambda col: pl.BlockSpec((1, S, LANES), lambda b, c: (b, 0, col))
    seq_vT = pl.BlockSpec((1, S // LANES, LANES, LANES), lambda b, c: (b, 0, 0, 0))
    qk3 = qk.reshape(B, S, qk.shape[1])
    cols = G * Hg * qb
    return pl.pallas_call(
        functools.partial(_nsa_body, kt=kt, top_n=min(SLC_TOPN, n_slc), n_slc=n_slc, hg=Hg, dh=dh),
        grid=(B, nq),
        in_specs=[pl.BlockSpec((qb, qw), lambda b, c: (b * nq + c, 0)),
                  pl.BlockSpec((qb, MXU_N), lambda b, c: (b * nq + c, g_col)),
                  pl.BlockSpec((1, ncp, LANES), lambda b, c: (b, 0, 0)),
                  pl.BlockSpec((1, LANES, ncp), lambda b, c: (b, 0, 0)),
                  seq_k(k_col), seq_vT, seq_k(k_col + 1), seq_vT,
                  _resident(c2sT.shape), _resident(et.shape)],
        out_specs=pl.BlockSpec((qb, qw), lambda b, c: (b * nq + c, 0)),
        out_shape=jax.ShapeDtypeStruct((T, qw), BF16),
        scratch_shapes=[pltpu.VMEM((2 * LANES, cols), BF16), pltpu.VMEM((LANES, qb), F32),
                        pltpu.VMEM((1, cols), F32), pltpu.VMEM((dh + BF16_ROWS, cols), F32),
                        pltpu.VMEM((G, dh, qb), jnp.int32)],
        compiler_params=pltpu.CompilerParams(dimension_semantics=("parallel", "arbitrary"),
                                             vmem_limit_bytes=VMEM_LIMIT),
        name="nsa",
    )(qk, gates, kc, vcT, qk3, vsT, qk3, vwT, c2sT, et)


def _nsa_constants(S):
    ncp = S // CMP_STRIDE
    n_slc = S // SLC_BLOCK
    cmp_start = np.arange(ncp) * CMP_STRIDE
    s_start = np.arange(n_slc) * SLC_BLOCK
    overlap = np.clip(np.minimum(cmp_start[:, None] + CMP_BLOCK, s_start[None, :] + SLC_BLOCK)
                      - np.maximum(cmp_start[:, None], s_start[None, :]), 0, None)
    c2sT = np.zeros((LANES, ncp), np.float32)
    c2sT[:n_slc, :] = overlap.T.astype(np.float32) / CMP_BLOCK
    c2sT[:, ncp - 1] = 0.0
    et = np.zeros((S, LANES), np.float32)
    et[np.arange(S), np.arange(S) // SLC_BLOCK] = 1.0
    return jnp.asarray(c2sT, BF16), jnp.asarray(et, BF16)


def _ret_body(q_ref, kT_ref, v_ref, decay_ref, zeta_ref, xi_ref, gch_ref, gn_ref, o_ref, *, C, dk):
    S = q_ref.shape[1]
    dv = gn_ref.shape[2]
    heads = LANES // dk
    N = S // C
    units = [(n, a) for n in range(N) for a in range(heads)]
    rows = lambda n: slice(n * C, (n + 1) * C)
    val = lambda n, a: v_ref[0, rows(n), a * dv:(a + 1) * dv]
    qp = lambda n: q_ref[0, rows(n), :]

    def own_rows(x, a):
        return jnp.concatenate([x if b == a else jnp.zeros_like(x) for b in range(heads)], axis=0)

    kT = {(n, a): kT_ref[0, 0, n, a * dk:(a + 1) * dk, :] for n, a in units}
    inner = {u: jnp.dot(qp(u[0]), own_rows(kT[u], u[1]), preferred_element_type=F32) for u in units}
    inner = {u: (inner[u] * decay_ref[u[1]]).astype(BF16) for u in units}
    o = {u: jnp.dot(inner[u], val(*u), preferred_element_type=F32) for u in units}
    kv = {(n, a): jnp.dot((kT[(n, a)].astype(F32) * zeta_ref[a]).astype(BF16), val(n, a),
                          preferred_element_type=F32) for n, a in units}
    state = {}
    for a in range(heads):
        R = jnp.zeros((dk, dv), F32)
        for n in range(N):
            state[(n, a)] = own_rows(R.astype(BF16), a)
            R = gch_ref[a] * R + kv[(n, a)]
    o = {u: o[u] + xi_ref[u[1]] * jnp.dot(qp(u[0]), state[u], preferred_element_type=F32) for u in units}
    mu = {u: jnp.mean(o[u], axis=-1, keepdims=True) for u in units}
    d = {u: o[u] - mu[u] for u in units}
    var = {u: jnp.mean(d[u] * d[u], axis=-1, keepdims=True) for u in units}
    for n, a in units:
        o_ref[0, rows(n), a * dv:(a + 1) * dv] = (d[(n, a)] * lax.rsqrt(var[(n, a)] + GN_EPS)
                                                  * gn_ref[a]).astype(o_ref.dtype)


def _retention(qk, kT, v, gn_gain, B, S, *, C=RET_CHUNK):
    H, dk, dv = RET_HEADS, RET_DK, RET_DV
    hp = LANES // dk
    assert C == LANES and dv == LANES
    log_g = np.log(1.0 - 2.0 ** (-5.0 - np.arange(H, dtype=np.float64)))
    i = np.arange(C, dtype=np.float64)
    diff = i[:, None] - i[None, :]
    const = lambda a: jnp.asarray(a, F32)
    decay = const(np.where(diff >= 0, np.exp(np.maximum(diff, 0.0) * log_g[:, None, None]), 0.0))
    zeta = const(np.exp((C - 1.0 - i)[None, :] * log_g[:, None]).reshape(H, 1, C))
    xi = const(np.broadcast_to(np.exp((i + 1.0)[None, :] * log_g[:, None])[:, :, None], (H, C, LANES)))
    gch = const(np.broadcast_to(np.exp(C * log_g)[:, None, None], (H, 1, dv)))
    per_pair = lambda shape: pl.BlockSpec((hp,) + shape, lambda b, p: (p, 0, 0))
    qk3 = qk.reshape(B, S, qk.shape[1])
    return pl.pallas_call(
        functools.partial(_ret_body, C=C, dk=dk),
        grid=(B, H // hp),
        in_specs=[pl.BlockSpec((1, S, LANES), lambda b, p: (b, 0, p)),
                  pl.BlockSpec((1, 1, S // LANES, LANES, LANES), lambda b, p: (b, p, 0, 0, 0)),
                  pl.BlockSpec((1, S, hp * dv), lambda b, p: (b, 0, p)),
                  per_pair((C, C)), per_pair((1, C)), per_pair((C, LANES)), per_pair((1, dv)),
                  per_pair((1, dv))],
        out_specs=pl.BlockSpec((1, S, hp * dv), lambda b, p: (b, 0, p)),
        out_shape=jax.ShapeDtypeStruct((B, S, H * dv), BF16),
        compiler_params=pltpu.CompilerParams(dimension_semantics=("parallel", "parallel"),
                                             vmem_limit_bytes=VMEM_LIMIT),
        name="retention",
    )(qk3, kT, v.reshape(B, S, H * dv), decay, zeta, xi, gch, gn_gain.reshape(H, 1, dv))


def _merge_body(x_ref, a_ref, r_ref, gr_ref, ga_ref, gb_ref, wa_ref, wr_ref, wo_ref, o_ref):
    g = gr_ref[...].astype(F32)
    r = (g * jax.nn.sigmoid(g) * r_ref[...].astype(F32)).astype(BF16)
    ya = jnp.dot(a_ref[...], wa_ref[...], preferred_element_type=F32)
    yr = jnp.dot(r, wr_ref[...], preferred_element_type=F32)
    mixed = jax.nn.sigmoid(ga_ref[...].astype(F32)) * ya + jax.nn.sigmoid(gb_ref[...].astype(F32)) * yr
    o_ref[...] = x_ref[...] + jnp.dot(mixed.astype(BF16), wo_ref[...], preferred_element_type=F32)


def _merge(x2d, a, r, gates, wa, wr, wo, *, tm=512):
    T, D = x2d.shape
    col = lambda j: pl.BlockSpec((tm, D), lambda i: (i, j))
    return pl.pallas_call(
        _merge_body,
        grid=(T // tm,),
        in_specs=[col(0), col(0), col(0), col(0), col(1), col(2),
                  _resident(wa.shape), _resident(wr.shape), _resident(wo.shape)],
        out_specs=col(0),
        out_shape=jax.ShapeDtypeStruct((T, D), F32),
        compiler_params=pltpu.CompilerParams(dimension_semantics=("parallel",),
                                             vmem_limit_bytes=VMEM_LIMIT),
        name="merge",
    )(x2d, a, r, gates, gates, gates, wa, wr, wo)


def _layer(x, p, final_norm):
    B, S, D = x.shape
    T = B * S
    G, Hg, dh = NSA_GROUPS, NSA_HG, NSA_DH
    H, dk, dv = RET_HEADS, RET_DK, RET_DV
    bf = lambda w: w.astype(BF16)

    x1 = _ffn(x.reshape(T, D), p["ffn1_norm"], bf(p["ffn1_w_gate"]), bf(p["ffn1_w_up"]),
              bf(p["ffn1_w_down"]), final_norm, final=False)

    w = p["w_in"].T
    widths = (G * Hg * dh,) + (G * dh,) * 6 + (3 * G * Hg, H * dk, H * dk, H * dv, H * dv, 2 * D)
    offs = np.concatenate([[0], np.cumsum(widths)])
    seg = lambda i: w[int(offs[i]):int(offs[i + 1])]
    (q_a, kc_a, vc_a, ks_a, vs_a, kw_a, vw_a, g_a, q_r, k_r, v_r, g_r, g_m) = [seg(i) for i in range(13)]
    gate_pad = MXU_N - g_a.shape[0]
    wa = bf(jnp.concatenate([q_a * (dh ** -0.5 * LOG2E), ks_a, kw_a], axis=0))
    wb = bf(jnp.concatenate([q_r * dk ** -0.5, k_r], axis=0))
    wc = bf(jnp.concatenate([v_r, kc_a, vc_a, vs_a, vw_a], axis=0))
    wd = bf(jnp.concatenate([g_r, g_m, g_a, jnp.zeros((gate_pad, D), F32)], axis=0))
    pos = np.arange(S)
    tab_a = _rope_tables(pos, ROPE_DIM, dh, ROPE_THETA)
    tab_b = _rope_tables(pos, dk, dk, RET_ROPE_THETA)
    qk_n, qk_r, krT, v_ret, kc_tok, vc_tok, vsT, vwT, gates = _proj(
        x1, p["mix_norm"], wa, wb, wc, wd, tab_a, tab_b, B, S)

    chunk = lambda t: t.reshape(B, S // CMP_STRIDE, CMP_STRIDE * G * dh)
    kw1, kw2, pe = _compress_weights(p["cmp_k_w1"], p["cmp_k_w2"], p["cmp_pos_emb"])
    vw1, vw2, _ = _compress_weights(p["cmp_v_w1"], p["cmp_v_w2"], p["cmp_pos_emb"])
    cmp_end = np.arange(S // CMP_STRIDE) * CMP_STRIDE + (CMP_BLOCK - 1)
    kc, vcT = _compress(chunk(kc_tok), chunk(vc_tok), pe, kw1, kw2, vw1, vw2,
                        _rope_tables(cmp_end, ROPE_DIM, dh, ROPE_THETA))
    c2sT, et = _nsa_constants(S)
    a_out = _nsa(qk_n, gates, kc, vcT, vsT, vwT, c2sT, et, B, S)
    r_out = _retention(qk_r, krT, v_ret, p["ret_gn_gain"], B, S).reshape(T, H * dv)

    x2 = _merge(x1, a_out, r_out, gates, bf(p["w_branch_nsa"]), bf(p["w_branch_ret"]), bf(p["w_out"]))
    x3 = _ffn(x2, p["ffn2_norm"], bf(p["ffn2_w_gate"]), bf(p["ffn2_w_up"]), bf(p["ffn2_w_down"]),
              final_norm, final=True)
    return x3.reshape(B, S, D)


def kernel(x, ffn1_norm, ffn1_w_gate, ffn1_w_up, ffn1_w_down, mix_norm, w_in, cmp_pos_emb, cmp_k_w1,
           cmp_k_w2, cmp_v_w1, cmp_v_w2, ret_gn_gain, w_branch_nsa, w_branch_ret, w_out, ffn2_norm,
           ffn2_w_gate, ffn2_w_up, ffn2_w_down, final_norm):
    assert ffn1_norm.shape[0] == 1, "single-layer stack"
    names = ("ffn1_norm", "ffn1_w_gate", "ffn1_w_up", "ffn1_w_down", "mix_norm", "w_in", "cmp_pos_emb",
             "cmp_k_w1", "cmp_k_w2", "cmp_v_w1", "cmp_v_w2", "ret_gn_gain", "w_branch_nsa",
             "w_branch_ret", "w_out", "ffn2_norm", "ffn2_w_gate", "ffn2_w_up", "ffn2_w_down")
    vals = (ffn1_norm, ffn1_w_gate, ffn1_w_up, ffn1_w_down, mix_norm, w_in, cmp_pos_emb, cmp_k_w1,
            cmp_k_w2, cmp_v_w1, cmp_v_w2, ret_gn_gain, w_branch_nsa, w_branch_ret, w_out, ffn2_norm,
            ffn2_w_gate, ffn2_w_up, ffn2_w_down)
    p = {n: v[0] for n, v in zip(names, vals)}
    return _layer(x, p, final_norm)
```

```python
import functools
import math

import numpy as np
import jax
import jax.numpy as jnp
from jax import lax
from jax.experimental import pallas as pl
from jax.experimental.pallas import tpu as pltpu

F32 = jnp.float32
BF16 = jnp.bfloat16

NSA_HEADS = 16
NSA_GROUPS = 2
NSA_HG = NSA_HEADS // NSA_GROUPS
NSA_DH = 64
CMP_BLOCK = 32
CMP_STRIDE = 16
SLC_BLOCK = 64
SLC_TOPN = 16
WINDOW = 512
ROPE_THETA = 500000.0
ROPE_DIM = NSA_DH // 4
FORCED_SCORE = 1.0e4
RET_HEADS = 8
RET_DK = 64
RET_DV = 128
RET_CHUNK = 128
RET_ROPE_THETA = 10000.0
EPS = 1e-6
GN_EPS = 1e-5
NEG_INF = -1e30
LOG2E = math.log2(math.e)

LANES = 128
SUBLANES = 8
BF16_ROWS = 16
MXU_N = 256
VMEM_LIMIT = 56 * 1024 * 1024

NT_DIMS = (((1,), (1,)), ((), ()))


def _rms(x, g, eps=EPS):
    return x * lax.rsqrt(jnp.mean(x * x, axis=-1, keepdims=True) + eps) * g


def _resident(shape):
    nd = len(shape)
    return pl.BlockSpec(shape, lambda *_: (0,) * nd, pipeline_mode=pl.Buffered(1))


def _ffn_body(x_ref, g_ref, wg_ref, wu_ref, wd_ref, fg_ref, o_ref, *, cuts, final):
    x = x_ref[...]
    h = _rms(x, g_ref[...]).astype(BF16)
    acc = jnp.zeros(x.shape, F32)
    for lo, hi in zip(cuts[:-1], cuts[1:]):
        sl = slice(lo, hi)
        g = jnp.dot(h, wg_ref[:, sl], preferred_element_type=F32)
        u = jnp.dot(h, wu_ref[:, sl], preferred_element_type=F32)
        a = (g * jax.nn.sigmoid(g) * u).astype(BF16)
        acc = acc + jnp.dot(a, wd_ref[sl, :], preferred_element_type=F32)
    y = x + 0.5 * acc
    if final:
        y = _rms(y, fg_ref[...])
    o_ref[...] = y


def _ffn(x2d, gain, wg, wu, wd, final_gain, *, final, tm=512):
    T, D = x2d.shape
    F = wg.shape[1]
    assert F % MXU_N == 0
    n_tiles = F // MXU_N
    cuts = (0, (n_tiles + 1) // 2 * MXU_N, F)
    tok = pl.BlockSpec((tm, D), lambda i: (i, 0))
    return pl.pallas_call(
        functools.partial(_ffn_body, cuts=cuts, final=final),
        grid=(T // tm,),
        in_specs=[tok, _resident((1, D)), _resident((D, F)), _resident((D, F)), _resident((F, D)),
                  _resident((1, D))],
        out_specs=tok,
        out_shape=jax.ShapeDtypeStruct((T, D), F32),
        compiler_params=pltpu.CompilerParams(dimension_semantics=("parallel",),
                                             vmem_limit_bytes=VMEM_LIMIT),
        name="ffn_final" if final else "ffn",
    )(x2d, gain.reshape(1, D), wg, wu, wd, final_gain.reshape(1, D))


def _in_segments(D):
    widths = dict(q_a=NSA_HEADS * NSA_DH, kc=NSA_GROUPS * NSA_DH, vc=NSA_GROUPS * NSA_DH,
                  ks=NSA_GROUPS * NSA_DH, vs=NSA_GROUPS * NSA_DH, kw=NSA_GROUPS * NSA_DH,
                  vw=NSA_GROUPS * NSA_DH, g_a=3 * NSA_HEADS, q_r=RET_HEADS * RET_DK, k_r=RET_HEADS * RET_DK,
                  v_r=RET_HEADS * RET_DV, g_r=RET_HEADS * RET_DV, g_m=2 * D)
    offs, o = {}, 0
    for name, wd in widths.items():
        offs[name] = o
        o += wd
    offs["end"] = o
    return offs


def _rope_slab(y, tab_ref, shift):
    return (y * tab_ref[0] + pltpu.roll(y, LANES - shift, 1) * tab_ref[1]
            + pltpu.roll(y, shift, 1) * tab_ref[2])


def _proj_body(x_ref, g_ref, w_ref, ta_ref, tb_ref,
               oa_ref, ob_ref, okrT_ref, ov_ref, okc_ref, ovc_ref, ovsT_ref, ovwT_ref, od_ref):
    x = x_ref[...]
    h = _rms(x, g_ref[...]).astype(BF16)
    tm = h.shape[0]
    seg = _in_segments(x.shape[1])
    per_chunk = MXU_N // LANES

    def put(o_ref, i):
        def route(ys):
            o_ref[:, i * LANES:(i + 1) * LANES] = ys.astype(o_ref.dtype)
        return route

    def put_t(o_ref, *lead):
        def route(ys):
            for r in range(tm // LANES):
                o_ref[(0,) + lead + (r,)] = ys[r * LANES:(r + 1) * LANES, :].T.astype(o_ref.dtype)
        return route

    def put_chunked(o_ref):
        def route(ys):
            o_ref[...] = ys.reshape(tm // CMP_STRIDE, CMP_STRIDE * LANES).astype(o_ref.dtype)
        return route

    def roped(route, tab_ref, shift, scale=None):
        def wrapped(ys):
            route(_rope_slab(ys if scale is None else ys * scale, tab_ref, shift))
        return wrapped

    nsa_rope = functools.partial(roped, tab_ref=ta_ref, shift=ROPE_DIM // 2)
    ret_rope = functools.partial(roped, tab_ref=tb_ref, shift=RET_DK // 2)
    n_gr = (seg["g_m"] - seg["g_r"]) // LANES
    n_gm = (seg["end"] - seg["g_m"]) // LANES
    nq = (seg["kc"] - seg["q_a"]) // LANES
    plan = []
    for i in range(0, nq, per_chunk):
        plan.append((seg["q_a"] + i * LANES,
                     [nsa_rope(put(oa_ref, i + s), scale=NSA_DH ** -0.5 * LOG2E) for s in range(per_chunk)]))
    plan.append((seg["kc"], [put_chunked(okc_ref), put_chunked(ovc_ref)]))
    plan.append((seg["ks"], [nsa_rope(put(oa_ref, nq)), put_t(ovsT_ref)]))
    plan.append((seg["kw"], [nsa_rope(put(oa_ref, nq + 1)), put_t(ovwT_ref)]))
    plan.append((seg["g_a"], [put(od_ref, n_gr + n_gm + s) for s in range(per_chunk)]))
    for name, count in (("q_r", ob_ref.shape[1] // LANES), ("k_r", okrT_ref.shape[1]),
                        ("v_r", ov_ref.shape[1] // LANES), ("g_r", n_gr), ("g_m", n_gm)):
        for i in range(0, count, per_chunk):
            routes = []
            for s in range(per_chunk):
                if name == "q_r":
                    routes.append(ret_rope(put(ob_ref, i + s), scale=RET_DK ** -0.5))
                elif name == "k_r":
                    routes.append(ret_rope(put_t(okrT_ref, i + s)))
                elif name == "v_r":
                    routes.append(put(ov_ref, i + s))
                else:
                    routes.append(put(od_ref, i + s + (n_gr if name == "g_m" else 0)))
            plan.append((seg[name] + i * LANES, routes))

    for row0, routes in plan:
        y = lax.dot_general(h, w_ref[row0:row0 + MXU_N, :], NT_DIMS, preferred_element_type=F32)
        for s, route in enumerate(routes):
            route(y[:, s * LANES:(s + 1) * LANES])


def _proj(x2d, gain, wT, tab_a, tab_b, B, S, *, tm=512):
    T, D = x2d.shape
    spt = S // tm
    seg = _in_segments(D)
    assert wT.shape == (seg["end"], D) and NSA_GROUPS * NSA_DH == LANES and seg["g_a"] + MXU_N <= seg["end"]
    assert all(o % BF16_ROWS == 0 for o in seg.values())
    na = seg["kc"] - seg["q_a"] + 2 * LANES
    nbw = seg["k_r"] - seg["q_r"]
    nkr = (seg["v_r"] - seg["k_r"]) // LANES
    nv = seg["g_r"] - seg["v_r"]
    nd = seg["end"] - seg["g_r"] + MXU_N
    tok = lambda n: pl.BlockSpec((tm, n), lambda i: (i, 0))
    tab = pl.BlockSpec((3, tm, LANES), lambda i: (0, i % spt, 0))
    tr = pl.BlockSpec((1, tm // LANES, LANES, LANES), lambda i: (i // spt, i % spt, 0, 0))
    chunked = pl.BlockSpec((tm // CMP_STRIDE, CMP_STRIDE * LANES), lambda i: (i, 0))
    tr_kr = pl.BlockSpec((1, nkr, tm // LANES, LANES, LANES), lambda i: (i // spt, 0, i % spt, 0, 0))
    return pl.pallas_call(
        _proj_body,
        grid=(T // tm,),
        in_specs=[tok(D), _resident((1, D)), _resident(wT.shape), tab, tab],
        out_specs=[tok(na), tok(nbw), tr_kr, tok(nv), chunked, chunked,
                   tr, tr, tok(nd)],
        out_shape=[jax.ShapeDtypeStruct((T, na), BF16),
                   jax.ShapeDtypeStruct((T, nbw), BF16),
                   jax.ShapeDtypeStruct((B, nkr, S // LANES, LANES, LANES), BF16),
                   jax.ShapeDtypeStruct((T, nv), BF16),
                   jax.ShapeDtypeStruct((T // CMP_STRIDE, CMP_STRIDE * LANES), BF16),
                   jax.ShapeDtypeStruct((T // CMP_STRIDE, CMP_STRIDE * LANES), BF16),
                   jax.ShapeDtypeStruct((B, S // LANES, LANES, LANES), BF16),
                   jax.ShapeDtypeStruct((B, S // LANES, LANES, LANES), BF16),
                   jax.ShapeDtypeStruct((T, nd), BF16)],
        compiler_params=pltpu.CompilerParams(dimension_semantics=("parallel",),
                                             vmem_limit_bytes=VMEM_LIMIT),
        name="proj",
    )(x2d, gain.reshape(1, D), wT, tab_a, tab_b)


def _rope_tables(pos, rot_dim, head_dim, theta):
    half = rot_dim // 2
    freqs = theta ** (-(np.arange(half, dtype=np.float64) * 2.0 / rot_dim))
    ang = np.asarray(pos, np.float64)[:, None] * freqs[None, :]
    cos, sin = np.cos(ang), np.sin(ang)
    n = ang.shape[0]
    rest = head_dim - rot_dim
    cos_h = np.concatenate([cos, cos, np.ones((n, rest))], axis=-1)
    sl_h = np.concatenate([-sin, np.zeros((n, half + rest))], axis=-1)
    sr_h = np.concatenate([np.zeros((n, half)), sin, np.zeros((n, rest))], axis=-1)
    rep = LANES // head_dim
    return jnp.asarray(np.stack([np.tile(t, (1, rep)) for t in (cos_h, sl_h, sr_h)]), F32)


def _gelu_tanh(x):
    return 0.5 * x * (1.0 + jnp.tanh(math.sqrt(2.0 / math.pi) * (x + 0.044715 * (x * x * x))))


def _compress_body(kt_ref, vt_ref, pe_ref, kw1_ref, kw2_ref, vw1_ref, vw2_ref, tab_ref,
                   kc_ref, vcT_ref):
    pe = pe_ref[...]
    n = kt_ref.shape[1]

    def mlp(tok_ref, w1_ref, w2_ref):
        c = tok_ref[0].astype(F32)
        lo = (c + pe[0:1]).astype(BF16)
        hi = (c + pe[1:2]).astype(BF16)
        out = jnp.zeros((n, LANES), F32)
        for g in range(NSA_GROUPS):
            top = jnp.dot(lo, w1_ref[g, 0], preferred_element_type=F32)
            bot = jnp.dot(hi, w1_ref[g, 1], preferred_element_type=F32)
            hid = _gelu_tanh(top + pltpu.roll(bot, n - 1, 0))
            out = out + jnp.dot(hid.astype(BF16), w2_ref[g], preferred_element_type=F32)
        return out

    kc_ref[0] = _rope_slab(mlp(kt_ref, kw1_ref, kw2_ref), tab_ref, ROPE_DIM // 2).astype(kc_ref.dtype)
    vc = mlp(vt_ref, vw1_ref, vw2_ref)
    for r in range(n // LANES):
        vcT_ref[0, :, r * LANES:(r + 1) * LANES] = vc[r * LANES:(r + 1) * LANES, :].T.astype(vcT_ref.dtype)


def _compress(k_tok, v_tok, pe, kw1, kw2, vw1, vw2, tab):
    B, n, w = k_tok.shape
    tok = pl.BlockSpec((1, n, w), lambda b: (b, 0, 0))
    return pl.pallas_call(
        _compress_body,
        grid=(B,),
        in_specs=[tok, tok, _resident(pe.shape), _resident(kw1.shape), _resident(kw2.shape),
                  _resident(vw1.shape), _resident(vw2.shape), _resident(tab.shape)],
        out_specs=[pl.BlockSpec((1, n, LANES), lambda b: (b, 0, 0)),
                   pl.BlockSpec((1, LANES, n), lambda b: (b, 0, 0))],
        out_shape=[jax.ShapeDtypeStruct((B, n, LANES), BF16), jax.ShapeDtypeStruct((B, LANES, n), BF16)],
        compiler_params=pltpu.CompilerParams(dimension_semantics=("parallel",),
                                             vmem_limit_bytes=VMEM_LIMIT),
        name="compress",
    )(k_tok, v_tok, pe, kw1, kw2, vw1, vw2, tab)


def _compress_weights(w1, w2, pe):
    G, dh = NSA_GROUPS, NSA_DH
    hid = w1.shape[1]
    halves = w1.reshape(2, CMP_STRIDE, dh, hid)
    w1e = jnp.zeros((G, 2, CMP_STRIDE, G, dh, hid), F32)
    w2e = jnp.zeros((G, hid, G, dh), F32)
    for g in range(G):
        w1e = w1e.at[g, :, :, g].set(halves)
        w2e = w2e.at[g, :, g].set(w2)
    pe_e = jnp.broadcast_to(pe.reshape(2, CMP_STRIDE, 1, dh), (2, CMP_STRIDE, G, dh))
    return (w1e.reshape(G, 2, CMP_STRIDE * G * dh, hid).astype(BF16), w2e.reshape(G, hid, G * dh).astype(BF16),
            pe_e.reshape(2, CMP_STRIDE * G * dh))


def _nsa_body(q_ref, gate_ref, kc_ref, vcT_ref, ks_ref, vsT_ref, kw_ref, vwT_ref, c2sT_ref, et_ref,
              o_ref, qT_scr, gT_scr, m_scr, acc_scr, rank_scr, *, kt, top_n, n_slc, hg, dh):
    qb = q_ref.shape[0]
    nh = q_ref.shape[1] // dh
    groups = nh // hg
    ncp = kc_ref.shape[1]
    c = pl.program_id(1)
    q0 = c * qb
    ones_rows = jnp.ones((BF16_ROWS, 1), BF16)
    grp_rows = lambda g: slice(g * dh, (g + 1) * dh)

    def v_tiles(vT_ref, k0, n):
        t = vT_ref[0, pl.ds(k0 // LANES, n // LANES)]
        vT = jnp.concatenate([t[i] for i in range(n // LANES)], axis=1)
        ones = jnp.broadcast_to(ones_rows, (BF16_ROWS, n))
        return [jnp.concatenate([vT[grp_rows(g)], ones], axis=0) for g in range(groups)]

    def tq(n):
        return q0 + lax.broadcasted_iota(jnp.int32, (n, qb), 1)

    def krow(n, k0=0):
        return k0 + lax.broadcasted_iota(jnp.int32, (n, qb), 0)

    qT_scr[...] = jnp.zeros(qT_scr.shape, BF16)
    for r in range(qb // LANES):
        rows = slice(r * LANES, (r + 1) * LANES)
        for j in range(nh // 2):
            t = q_ref[rows, j * LANES:(j + 1) * LANES].astype(F32).T.astype(BF16)
            for half in range(2):
                h = 2 * j + half
                qT_scr[grp_rows(h // hg), h * qb + r * LANES:h * qb + (r + 1) * LANES] = (
                    t[half * dh:(half + 1) * dh])
        gT_scr[:, rows] = jax.nn.sigmoid(gate_ref[rows, :LANES].astype(F32).T)

    qT = qT_scr[:LANES, :]

    span = WINDOW + LANES

    def window_tile(r):
        t0 = q0 + r * LANES
        w0 = pl.multiple_of(jnp.maximum(t0 - WINDOW, 0), LANES)
        dist = (t0 + lax.broadcasted_iota(jnp.int32, (span, LANES), 1)
                - (w0 + lax.broadcasted_iota(jnp.int32, (span, LANES), 0)))
        bias_w = jnp.where((dist >= 0) & (dist < WINDOW), 0.0, NEG_INF)
        bias_w = jnp.concatenate([bias_w, bias_w], axis=1)
        qT_r = jnp.concatenate([qT[:, h * qb + r * LANES:h * qb + (r + 1) * LANES] for h in range(nh)], axis=1)
        s_w = jnp.dot(kw_ref[0, pl.ds(w0, span), :], qT_r, preferred_element_type=F32)
        vwT = v_tiles(vwT_ref, w0, span)
        outs_r = []
        for hp in range(nh // 2):
            sh = s_w[:, hp * 2 * LANES:(hp + 1) * 2 * LANES] + bias_w
            p = jnp.exp2(sh - jnp.max(sh, axis=0, keepdims=True)).astype(BF16)
            acc_w = jnp.dot(vwT[2 * hp // hg], p, preferred_element_type=F32)
            ow = acc_w[:dh] * (1.0 / acc_w[dh:dh + 1])
            outs_r += [ow[:, :LANES], ow[:, LANES:]]
        return outs_r

    n_lt = qb // LANES
    o_w = [window_tile(r) for r in range(n_lt // 2)]

    s_c = jnp.dot(kc_ref[0], qT, preferred_element_type=F32)
    cend = krow(ncp) * CMP_STRIDE + (CMP_BLOCK - 1)
    bias_c = jnp.where(cend <= tq(ncp), 0.0, NEG_INF)
    bias_c = jnp.concatenate([bias_c, bias_c], axis=1)
    nb = dh
    vcT = vcT_ref[0]
    lhs_c = [jnp.concatenate([vcT[grp_rows(g)], jnp.broadcast_to(ones_rows, (BF16_ROWS, ncp)),
                              c2sT_ref[:nb, :]], axis=0) for g in range(groups)]
    imps = [jnp.zeros((nb, qb), F32) for _ in range(groups)]
    o_c = []
    for hp in range(nh // 2):
        g = 2 * hp // hg
        sh = s_c[:, hp * 2 * qb:(hp + 1) * 2 * qb] + bias_c
        e = jnp.exp2(sh - jnp.max(sh, axis=0, keepdims=True)).astype(BF16)
        r = jnp.dot(lhs_c[g], e, preferred_element_type=F32)
        inv_l = 1.0 / r[dh:dh + 1]
        o_c.append(r[:dh] * inv_l)
        w = r[dh + BF16_ROWS:] * inv_l
        imps[g] = imps[g] + w[:, :qb] + w[:, qb:]
    o_c = jnp.concatenate(o_c, axis=1)
    sb = krow(nb)
    cur = jnp.right_shift(tq(nb), int(math.log2(SLC_BLOCK)))
    forced = (sb == 0) | (sb == cur) | (sb == cur - 1)
    future = sb > cur
    imps = [jnp.where(forced, FORCED_SCORE, jnp.where(future, -FORCED_SCORE, imp)) for imp in imps]
    sub = SUBLANES
    rank_scr[...] = jnp.zeros(rank_scr.shape, jnp.int32)
    last_block = (q0 + qb - 1) // SLC_BLOCK
    sb_l = sb[:, :LANES]
    for grp in range(n_slc // sub):
        @pl.when(grp * sub <= last_block)
        def _():
            for g in range(groups):
                for lc in range(qb // LANES):
                    imp_l = imps[g][:, lc * LANES:(lc + 1) * LANES]
                    chunks = [imp_l[r * sub:(r + 1) * sub] for r in range(nb // sub)]
                    ranks = [jnp.zeros((sub, LANES), jnp.int32) for _ in chunks]
                    for sp in range(grp * sub, (grp + 1) * sub):
                        row = imp_l[sp:sp + 1, :]
                        for r, blk in enumerate(chunks):
                            if r * sub > sp:
                                beats = row >= blk
                            elif (r + 1) * sub - 1 <= sp:
                                beats = row > blk
                            else:
                                beats = (row > blk) | ((row == blk) & (sb_l[r * sub:(r + 1) * sub] > sp))
                            ranks[r] = ranks[r] + jnp.where(beats, 1, 0)
                    rank_scr[g, :, lc * LANES:(lc + 1) * LANES] += jnp.concatenate(ranks, axis=0)
    for g in range(groups):
        selneg = jnp.where((rank_scr[g] < top_n) & jnp.logical_not(future), 0.0, NEG_INF).astype(BF16)
        for h in range(g * hg, (g + 1) * hg):
            qT_scr[LANES:LANES + nb, h * qb:(h + 1) * qb] = selneg

    qT_aug = qT_scr[...]

    def scores(j):
        k0 = pl.multiple_of(j * kt, kt)
        ka = jnp.concatenate([ks_ref[0, pl.ds(k0, kt), :], et_ref[pl.ds(k0, kt), :]], axis=1)
        return jnp.dot(ka, qT_aug, preferred_element_type=F32)

    def fold(s, j, causal):
        vT = v_tiles(vsT_ref, pl.multiple_of(j * kt, kt), kt)
        if causal:
            bias = jnp.where(krow(kt, j * kt) <= tq(kt), 0.0, NEG_INF)
            bias = jnp.concatenate([bias, bias], axis=1)
        for hp in range(nh // 2):
            sl = slice(hp * 2 * qb, (hp + 1) * 2 * qb)
            sh = s[:, sl] + bias if causal else s[:, sl]
            m_old = m_scr[:, sl]
            m_new = jnp.maximum(m_old, jnp.max(sh, axis=0, keepdims=True))
            pv = jnp.dot(vT[2 * hp // hg], jnp.exp2(sh - m_new).astype(BF16), preferred_element_type=F32)
            acc_scr[:, sl] = jnp.exp2(m_old - m_new) * acc_scr[:, sl] + pv
            m_scr[:, sl] = m_new

    m_scr[...] = jnp.full(m_scr.shape, NEG_INF, F32)
    acc_scr[...] = jnp.zeros(acc_scr.shape, F32)

    def run_tiles(j0, n):
        k0 = pl.multiple_of(j0 * kt, kt)
        ka = jnp.concatenate([ks_ref[0, pl.ds(k0, n * kt), :], et_ref[pl.ds(k0, n * kt), :]], axis=1)
        ss = jnp.dot(ka, qT_aug, preferred_element_type=F32)
        for t in range(n):
            fold(ss[t * kt:(t + 1) * kt], j0 + t, False)

    def main_pair(i, carry):
        run_tiles(2 * i, 2)
        return carry

    lax.fori_loop(0, c // 2, main_pair, 0)

    @pl.when(c % 2 == 1)
    def _():
        fold(scores(c - 1), c - 1, False)

    fold(scores(c), c, True)
    o_s = acc_scr[:dh, :] * (1.0 / acc_scr[dh:dh + 1, :])

    o_w += [window_tile(r) for r in range(n_lt // 2, n_lt)]
    o_w = jnp.concatenate([o_w[r][h] for h in range(nh) for r in range(n_lt)], axis=1)

    has_cmp = (tq(1) >= CMP_BLOCK - 1).astype(F32)
    gts = gT_scr[:3 * nh, :]
    outs = []
    for h in range(nh):
        sl = slice(h * qb, (h + 1) * qb)
        outs.append(gts[3 * h:3 * h + 1] * has_cmp * o_c[:, sl] + gts[3 * h + 1:3 * h + 2] * o_s[:, sl]
                    + gts[3 * h + 2:3 * h + 3] * o_w[:, sl])
    for j in range(nh // 2):
        pair = jnp.concatenate([outs[2 * j], outs[2 * j + 1]], axis=0)
        for r in range(qb // LANES):
            rows = slice(r * LANES, (r + 1) * LANES)
            o_ref[rows, j * LANES:(j + 1) * LANES] = pair[:, rows].T.astype(o_ref.dtype)


def _nsa(qk, gates, kc, vcT, vsT, vwT, c2sT, et, B, S, *, qb=256):
    G, Hg, dh = NSA_GROUPS, NSA_HG, NSA_DH
    T = B * S
    nq = S // qb
    n_slc = S // SLC_BLOCK
    kt = qb
    assert n_slc <= dh and 2 * dh == LANES and qb % LANES == 0 and S % qb == 0 and WINDOW % qb == 0
    assert S >= WINDOW + qb
    qw = G * Hg * dh
    assert 3 * G * Hg <= LANES and G * dh == LANES
    ncp = kc.shape[1]
    k_col = qw // LANES
    g_col = gates.shape[1] // MXU_N - 1
    seq_k = lambda col: pl.BlockSpec((1, S, LANES), lambda b, c: (b, 0, col))
    seq_vT = pl.BlockSpec((1, S // LANES, LANES, LANES), lambda b, c: (b, 0, 0, 0))
    qk3 = qk.reshape(B, S, qk.shape[1])
    cols = G * Hg * qb
    return pl.pallas_call(
        functools.partial(_nsa_body, kt=kt, top_n=min(SLC_TOPN, n_slc), n_slc=n_slc, hg=Hg, dh=dh),
        grid=(B, nq),
        in_specs=[pl.BlockSpec((qb, qw), lambda b, c: (b * nq + c, 0)),
                  pl.BlockSpec((qb, MXU_N), lambda b, c: (b * nq + c, g_col)),
                  pl.BlockSpec((1, ncp, LANES), lambda b, c: (b, 0, 0)),
                  pl.BlockSpec((1, LANES, ncp), lambda b, c: (b, 0, 0)),
                  seq_k(k_col), seq_vT, seq_k(k_col + 1), seq_vT,
                  _resident(c2sT.shape), _resident(et.shape)],
        out_specs=pl.BlockSpec((qb, qw), lambda b, c: (b * nq + c, 0)),
        out_shape=jax.ShapeDtypeStruct((T, qw), BF16),
        scratch_shapes=[pltpu.VMEM((2 * LANES, cols), BF16), pltpu.VMEM((LANES, qb), F32),
                        pltpu.VMEM((1, cols), F32), pltpu.VMEM((dh + BF16_ROWS, cols), F32),
                        pltpu.VMEM((G, dh, qb), jnp.int32)],
        compiler_params=pltpu.CompilerParams(dimension_semantics=("parallel", "arbitrary"),
                                             vmem_limit_bytes=VMEM_LIMIT),
        name="nsa",
    )(qk, gates, kc, vcT, qk3, vsT, qk3, vwT, c2sT, et)


def _nsa_constants(S):
    ncp = S // CMP_STRIDE
    n_slc = S // SLC_BLOCK
    cmp_start = np.arange(ncp) * CMP_STRIDE
    s_start = np.arange(n_slc) * SLC_BLOCK
    overlap = np.clip(np.minimum(cmp_start[:, None] + CMP_BLOCK, s_start[None, :] + SLC_BLOCK)
                      - np.maximum(cmp_start[:, None], s_start[None, :]), 0, None)
    c2sT = np.zeros((LANES, ncp), np.float32)
    c2sT[:n_slc, :] = overlap.T.astype(np.float32) / CMP_BLOCK
    c2sT[:, ncp - 1] = 0.0
    et = np.zeros((S, LANES), np.float32)
    et[np.arange(S), np.arange(S) // SLC_BLOCK] = 1.0
    return jnp.asarray(c2sT, BF16), jnp.asarray(et, BF16)


def _ret_body(q_ref, kT_ref, v_ref, decay_ref, zeta_ref, xi_ref, gch_ref, gn_ref, o_ref, *, C, dk):
    S = q_ref.shape[1]
    dv = gn_ref.shape[2]
    heads = LANES // dk
    N = S // C
    units = [(n, a) for n in range(N) for a in range(heads)]
    rows = lambda n: slice(n * C, (n + 1) * C)
    val = lambda n, a: v_ref[0, rows(n), a * dv:(a + 1) * dv]
    qp = lambda n: q_ref[0, rows(n), :]

    def own_rows(x, a):
        return jnp.concatenate([x if b == a else jnp.zeros_like(x) for b in range(heads)], axis=0)

    kT = {(n, a): kT_ref[0, 0, n, a * dk:(a + 1) * dk, :] for n, a in units}
    inner = {u: jnp.dot(qp(u[0]), own_rows(kT[u], u[1]), preferred_element_type=F32) for u in units}
    inner = {u: (inner[u] * decay_ref[u[1]]).astype(BF16) for u in units}
    o = {u: jnp.dot(inner[u], val(*u), preferred_element_type=F32) for u in units}
    kv = {(n, a): jnp.dot((kT[(n, a)].astype(F32) * zeta_ref[a]).astype(BF16), val(n, a),
                          preferred_element_type=F32) for n, a in units}
    state = {}
    for a in range(heads):
        R = jnp.zeros((dk, dv), F32)
        for n in range(N):
            state[(n, a)] = own_rows(R.astype(BF16), a)
            R = gch_ref[a] * R + kv[(n, a)]
    o = {u: o[u] + xi_ref[u[1]] * jnp.dot(qp(u[0]), state[u], preferred_element_type=F32) for u in units}
    mu = {u: jnp.mean(o[u], axis=-1, keepdims=True) for u in units}
    d = {u: o[u] - mu[u] for u in units}
    var = {u: jnp.mean(d[u] * d[u], axis=-1, keepdims=True) for u in units}
    for n, a in units:
        o_ref[0, rows(n), a * dv:(a + 1) * dv] = (d[(n, a)] * lax.rsqrt(var[(n, a)] + GN_EPS)
                                                  * gn_ref[a]).astype(o_ref.dtype)


def _retention(qk, kT, v, gn_gain, B, S, *, C=RET_CHUNK):
    H, dk, dv = RET_HEADS, RET_DK, RET_DV
    hp = LANES // dk
    assert C == LANES and dv == LANES
    log_g = np.log(1.0 - 2.0 ** (-5.0 - np.arange(H, dtype=np.float64)))
    i = np.arange(C, dtype=np.float64)
    diff = i[:, None] - i[None, :]
    const = lambda a: jnp.asarray(a, F32)
    decay = const(np.where(diff >= 0, np.exp(np.maximum(diff, 0.0) * log_g[:, None, None]), 0.0))
    zeta = const(np.exp((C - 1.0 - i)[None, :] * log_g[:, None]).reshape(H, 1, C))
    xi = const(np.broadcast_to(np.exp((i + 1.0)[None, :] * log_g[:, None])[:, :, None], (H, C, LANES)))
    gch = const(np.broadcast_to(np.exp(C * log_g)[:, None, None], (H, 1, dv)))
    per_pair = lambda shape: pl.BlockSpec((hp,) + shape, lambda b, p: (p, 0, 0))
    qk3 = qk.reshape(B, S, qk.shape[1])
    return pl.pallas_call(
        functools.partial(_ret_body, C=C, dk=dk),
        grid=(B, H // hp),
        in_specs=[pl.BlockSpec((1, S, LANES), lambda b, p: (b, 0, p)),
                  pl.BlockSpec((1, 1, S // LANES, LANES, LANES), lambda b, p: (b, p, 0, 0, 0)),
                  pl.BlockSpec((1, S, hp * dv), lambda b, p: (b, 0, p)),
                  per_pair((C, C)), per_pair((1, C)), per_pair((C, LANES)), per_pair((1, dv)),
                  per_pair((1, dv))],
        out_specs=pl.BlockSpec((1, S, hp * dv), lambda b, p: (b, 0, p)),
        out_shape=jax.ShapeDtypeStruct((B, S, H * dv), BF16),
        compiler_params=pltpu.CompilerParams(dimension_semantics=("parallel", "parallel"),
                                             vmem_limit_bytes=VMEM_LIMIT),
        name="retention",
    )(qk3, kT, v.reshape(B, S, H * dv), decay, zeta, xi, gch, gn_gain.reshape(H, 1, dv))


def _merge_body(x_ref, a_ref, r_ref, gr_ref, ga_ref, gb_ref, wa_ref, wr_ref, wo_ref, o_ref):
    g = gr_ref[...].astype(F32)
    r = (g * jax.nn.sigmoid(g) * r_ref[...].astype(F32)).astype(BF16)
    ya = jnp.dot(a_ref[...], wa_ref[...], preferred_element_type=F32)
    yr = jnp.dot(r, wr_ref[...], preferred_element_type=F32)
    mixed = jax.nn.sigmoid(ga_ref[...].astype(F32)) * ya + jax.nn.sigmoid(gb_ref[...].astype(F32)) * yr
    o_ref[...] = x_ref[...] + jnp.dot(mixed.astype(BF16), wo_ref[...], preferred_element_type=F32)


def _merge(x2d, a, r, gates, wa, wr, wo, *, tm=512):
    T, D = x2d.shape
    col = lambda j: pl.BlockSpec((tm, D), lambda i: (i, j))
    return pl.pallas_call(
        _merge_body,
        grid=(T // tm,),
        in_specs=[col(0), col(0), col(0), col(0), col(1), col(2),
                  _resident(wa.shape), _resident(wr.shape), _resident(wo.shape)],
        out_specs=col(0),
        out_shape=jax.ShapeDtypeStruct((T, D), F32),
        compiler_params=pltpu.CompilerParams(dimension_semantics=("parallel",),
                                             vmem_limit_bytes=VMEM_LIMIT),
        name="merge",
    )(x2d, a, r, gates, gates, gates, wa, wr, wo)


def _layer(x, p, final_norm):
    B, S, D = x.shape
    T = B * S
    G, Hg, dh = NSA_GROUPS, NSA_HG, NSA_DH
    H, dk, dv = RET_HEADS, RET_DK, RET_DV
    bf = lambda w: w.astype(BF16)

    x1 = _ffn(x.reshape(T, D), p["ffn1_norm"], bf(p["ffn1_w_gate"]), bf(p["ffn1_w_up"]),
              bf(p["ffn1_w_down"]), final_norm, final=False)

    pos = np.arange(S)
    tab_a = _rope_tables(pos, ROPE_DIM, dh, ROPE_THETA)
    tab_b = _rope_tables(pos, dk, dk, RET_ROPE_THETA)
    qk_n, qk_r, krT, v_ret, kc_tok, vc_tok, vsT, vwT, gates = _proj(
        x1, p["mix_norm"], bf(p["w_in"].T), tab_a, tab_b, B, S)

    chunk = lambda t: t.reshape(B, S // CMP_STRIDE, CMP_STRIDE * G * dh)
    kw1, kw2, pe = _compress_weights(p["cmp_k_w1"], p["cmp_k_w2"], p["cmp_pos_emb"])
    vw1, vw2, _ = _compress_weights(p["cmp_v_w1"], p["cmp_v_w2"], p["cmp_pos_emb"])
    cmp_end = np.arange(S // CMP_STRIDE) * CMP_STRIDE + (CMP_BLOCK - 1)
    kc, vcT = _compress(chunk(kc_tok), chunk(vc_tok), pe, kw1, kw2, vw1, vw2,
                        _rope_tables(cmp_end, ROPE_DIM, dh, ROPE_THETA))
    c2sT, et = _nsa_constants(S)
    a_out = _nsa(qk_n, gates, kc, vcT, vsT, vwT, c2sT, et, B, S)
    r_out = _retention(qk_r, krT, v_ret, p["ret_gn_gain"], B, S).reshape(T, H * dv)

    x2 = _merge(x1, a_out, r_out, gates, bf(p["w_branch_nsa"]), bf(p["w_branch_ret"]), bf(p["w_out"]))
    x3 = _ffn(x2, p["ffn2_norm"], bf(p["ffn2_w_gate"]), bf(p["ffn2_w_up"]), bf(p["ffn2_w_down"]),
              final_norm, final=True)
    return x3.reshape(B, S, D)


def kernel(x, ffn1_norm, ffn1_w_gate, ffn1_w_up, ffn1_w_down, mix_norm, w_in, cmp_pos_emb, cmp_k_w1,
           cmp_k_w2, cmp_v_w1, cmp_v_w2, ret_gn_gain, w_branch_nsa, w_branch_ret, w_out, ffn2_norm,
           ffn2_w_gate, ffn2_w_up, ffn2_w_down, final_norm):
    assert ffn1_norm.shape[0] == 1, "single-layer stack"
    names = ("ffn1_norm", "ffn1_w_gate", "ffn1_w_up", "ffn1_w_down", "mix_norm", "w_in", "cmp_pos_emb",
             "cmp_k_w1", "cmp_k_w2", "cmp_v_w1", "cmp_v_w2", "ret_gn_gain", "w_branch_nsa",
             "w_branch_ret", "w_out", "ffn2_norm", "ffn2_w_gate", "ffn2_w_up", "ffn2_w_down")
    vals = (ffn1_norm, ffn1_w_gate, ffn1_w_up, ffn1_w_down, mix_norm, w_in, cmp_pos_emb, cmp_k_w1,
            cmp_k_w2, cmp_v_w1, cmp_v_w2, ret_gn_gain, w_branch_nsa, w_branch_ret, w_out, ffn2_norm,
            ffn2_w_gate, ffn2_w_up, ffn2_w_down)
    p = {n: v[0] for n, v in zip(names, vals)}
    return _layer(x, p, final_norm)
```

```python
import functools
import math

import numpy as np
import jax
import jax.numpy as jnp
from jax import lax
from jax.experimental import pallas as pl
from jax.experimental.pallas import tpu as pltpu

F32 = jnp.float32
BF16 = jnp.bfloat16

NSA_HEADS = 16
NSA_GROUPS = 2
NSA_HG = NSA_HEADS // NSA_GROUPS
NSA_DH = 64
CMP_BLOCK = 32
CMP_STRIDE = 16
SLC_BLOCK = 64
SLC_TOPN = 16
WINDOW = 512
ROPE_THETA = 500000.0
ROPE_DIM = NSA_DH // 4
FORCED_SCORE = 1.0e4
RET_HEADS = 8
RET_DK = 64
RET_DV = 128
RET_CHUNK = 128
RET_ROPE_THETA = 10000.0
EPS = 1e-6
GN_EPS = 1e-5
NEG_INF = -1e30
LOG2E = math.log2(math.e)

LANES = 128
SUBLANES = 8
BF16_ROWS = 16
MXU_N = 256
VMEM_LIMIT = 56 * 1024 * 1024

NT_DIMS = (((1,), (1,)), ((), ()))


def _rms(x, g, eps=EPS):
    return x * lax.rsqrt(jnp.mean(x * x, axis=-1, keepdims=True) + eps) * g


def _resident(shape):
    nd = len(shape)
    return pl.BlockSpec(shape, lambda *_: (0,) * nd, pipeline_mode=pl.Buffered(1))


def _ffn_body(x_ref, g_ref, wg_ref, wu_ref, wd_ref, fg_ref, o_ref, *, cuts, final):
    x = x_ref[...]
    h = _rms(x, g_ref[...]).astype(BF16)
    acc = jnp.zeros(x.shape, F32)
    for lo, hi in zip(cuts[:-1], cuts[1:]):
        sl = slice(lo, hi)
        g = jnp.dot(h, wg_ref[:, sl], preferred_element_type=F32)
        u = jnp.dot(h, wu_ref[:, sl], preferred_element_type=F32)
        a = (g * jax.nn.sigmoid(g) * u).astype(BF16)
        acc = acc + jnp.dot(a, wd_ref[sl, :], preferred_element_type=F32)
    y = x + 0.5 * acc
    if final:
        y = _rms(y, fg_ref[...])
    o_ref[...] = y


def _ffn(x2d, gain, wg, wu, wd, final_gain, *, final, tm=512):
    T, D = x2d.shape
    F = wg.shape[1]
    assert F % MXU_N == 0
    n_tiles = F // MXU_N
    cuts = (0, (n_tiles + 1) // 2 * MXU_N, F)
    tok = pl.BlockSpec((tm, D), lambda i: (i, 0))
    return pl.pallas_call(
        functools.partial(_ffn_body, cuts=cuts, final=final),
        grid=(T // tm,),
        in_specs=[tok, _resident((1, D)), _resident((D, F)), _resident((D, F)), _resident((F, D)),
                  _resident((1, D))],
        out_specs=tok,
        out_shape=jax.ShapeDtypeStruct((T, D), F32),
        compiler_params=pltpu.CompilerParams(dimension_semantics=("parallel",),
                                             vmem_limit_bytes=VMEM_LIMIT),
        name="ffn_final" if final else "ffn",
    )(x2d, gain.reshape(1, D), wg, wu, wd, final_gain.reshape(1, D))


def _in_segments(D):
    widths = dict(q_a=NSA_HEADS * NSA_DH, kc=NSA_GROUPS * NSA_DH, vc=NSA_GROUPS * NSA_DH,
                  ks=NSA_GROUPS * NSA_DH, vs=NSA_GROUPS * NSA_DH, kw=NSA_GROUPS * NSA_DH,
                  vw=NSA_GROUPS * NSA_DH, g_a=3 * NSA_HEADS, q_r=RET_HEADS * RET_DK, k_r=RET_HEADS * RET_DK,
                  v_r=RET_HEADS * RET_DV, g_r=RET_HEADS * RET_DV, g_m=2 * D)
    offs, o = {}, 0
    for name, wd in widths.items():
        offs[name] = o
        o += wd
    offs["end"] = o
    return offs


def _rope_slab(y, tab_ref, shift):
    return (y * tab_ref[0] + pltpu.roll(y, LANES - shift, 1) * tab_ref[1]
            + pltpu.roll(y, shift, 1) * tab_ref[2])


def _proj_body(x_ref, g_ref, w_ref, ta_ref, tb_ref,
               oa_ref, ob_ref, okrT_ref, ov_ref, okc_ref, ovc_ref, ovsT_ref, ovwT_ref, od_ref):
    x = x_ref[...]
    h = _rms(x, g_ref[...]).astype(BF16)
    tm = h.shape[0]
    seg = _in_segments(x.shape[1])
    per_chunk = MXU_N // LANES

    def put(o_ref, i):
        def route(ys):
            o_ref[:, i * LANES:(i + 1) * LANES] = ys.astype(o_ref.dtype)
        return route

    def put_t(o_ref, *lead):
        def route(ys):
            for r in range(tm // LANES):
                o_ref[(0,) + lead + (r,)] = ys[r * LANES:(r + 1) * LANES, :].T.astype(o_ref.dtype)
        return route

    def put_chunked(o_ref):
        def route(ys):
            o_ref[...] = ys.reshape(tm // CMP_STRIDE, CMP_STRIDE * LANES).astype(o_ref.dtype)
        return route

    def roped(route, tab_ref, shift, scale=None):
        def wrapped(ys):
            route(_rope_slab(ys if scale is None else ys * scale, tab_ref, shift))
        return wrapped

    nsa_rope = functools.partial(roped, tab_ref=ta_ref, shift=ROPE_DIM // 2)
    ret_rope = functools.partial(roped, tab_ref=tb_ref, shift=RET_DK // 2)
    n_gr = (seg["g_m"] - seg["g_r"]) // LANES
    n_gm = (seg["end"] - seg["g_m"]) // LANES
    nq = (seg["kc"] - seg["q_a"]) // LANES
    plan = []
    for i in range(0, nq, per_chunk):
        plan.append((seg["q_a"] + i * LANES,
                     [nsa_rope(put(oa_ref, i + s), scale=NSA_DH ** -0.5 * LOG2E) for s in range(per_chunk)]))
    plan.append((seg["kc"], [put_chunked(okc_ref), put_chunked(ovc_ref)]))
    plan.append((seg["ks"], [nsa_rope(put(oa_ref, nq)), put_t(ovsT_ref)]))
    plan.append((seg["kw"], [nsa_rope(put(oa_ref, nq + 1)), put_t(ovwT_ref)]))
    plan.append((seg["g_a"], [put(od_ref, n_gr + n_gm + s) for s in range(per_chunk)]))
    for name, count in (("q_r", ob_ref.shape[1] // LANES), ("k_r", okrT_ref.shape[1]),
                        ("v_r", ov_ref.shape[1] // LANES), ("g_r", n_gr), ("g_m", n_gm)):
        for i in range(0, count, per_chunk):
            routes = []
            for s in range(per_chunk):
                if name == "q_r":
                    routes.append(ret_rope(put(ob_ref, i + s), scale=RET_DK ** -0.5))
                elif name == "k_r":
                    routes.append(ret_rope(put_t(okrT_ref, i + s)))
                elif name == "v_r":
                    routes.append(put(ov_ref, i + s))
                else:
                    routes.append(put(od_ref, i + s + (n_gr if name == "g_m" else 0)))
            plan.append((seg[name] + i * LANES, routes))

    for row0, routes in plan:
        y = lax.dot_general(h, w_ref[row0:row0 + MXU_N, :], NT_DIMS, preferred_element_type=F32)
        for s, route in enumerate(routes):
            route(y[:, s * LANES:(s + 1) * LANES])


def _proj(x2d, gain, wT, tab_a, tab_b, B, S, *, tm=512):
    T, D = x2d.shape
    spt = S // tm
    seg = _in_segments(D)
    assert wT.shape == (seg["end"], D) and NSA_GROUPS * NSA_DH == LANES and seg["g_a"] + MXU_N <= seg["end"]
    assert all(o % BF16_ROWS == 0 for o in seg.values())
    na = seg["kc"] - seg["q_a"] + 2 * LANES
    nbw = seg["k_r"] - seg["q_r"]
    nkr = (seg["v_r"] - seg["k_r"]) // LANES
    nv = seg["g_r"] - seg["v_r"]
    nd = seg["end"] - seg["g_r"] + MXU_N
    tok = lambda n: pl.BlockSpec((tm, n), lambda i: (i, 0))
    tab = pl.BlockSpec((3, tm, LANES), lambda i: (0, i % spt, 0))
    tr = pl.BlockSpec((1, tm // LANES, LANES, LANES), lambda i: (i // spt, i % spt, 0, 0))
    chunked = pl.BlockSpec((tm // CMP_STRIDE, CMP_STRIDE * LANES), lambda i: (i, 0))
    tr_kr = pl.BlockSpec((1, nkr, tm // LANES, LANES, LANES), lambda i: (i // spt, 0, i % spt, 0, 0))
    return pl.pallas_call(
        _proj_body,
        grid=(T // tm,),
        in_specs=[tok(D), _resident((1, D)), _resident(wT.shape), tab, tab],
        out_specs=[tok(na), tok(nbw), tr_kr, tok(nv), chunked, chunked,
                   tr, tr, tok(nd)],
        out_shape=[jax.ShapeDtypeStruct((T, na), BF16),
                   jax.ShapeDtypeStruct((T, nbw), BF16),
                   jax.ShapeDtypeStruct((B, nkr, S // LANES, LANES, LANES), BF16),
                   jax.ShapeDtypeStruct((T, nv), BF16),
                   jax.ShapeDtypeStruct((T // CMP_STRIDE, CMP_STRIDE * LANES), BF16),
                   jax.ShapeDtypeStruct((T // CMP_STRIDE, CMP_STRIDE * LANES), BF16),
                   jax.ShapeDtypeStruct((B, S // LANES, LANES, LANES), BF16),
                   jax.ShapeDtypeStruct((B, S // LANES, LANES, LANES), BF16),
                   jax.ShapeDtypeStruct((T, nd), BF16)],
        compiler_params=pltpu.CompilerParams(dimension_semantics=("parallel",),
                                             vmem_limit_bytes=VMEM_LIMIT),
        name="proj",
    )(x2d, gain.reshape(1, D), wT, tab_a, tab_b)


def _rope_tables(pos, rot_dim, head_dim, theta):
    half = rot_dim // 2
    freqs = theta ** (-(np.arange(half, dtype=np.float64) * 2.0 / rot_dim))
    ang = np.asarray(pos, np.float64)[:, None] * freqs[None, :]
    cos, sin = np.cos(ang), np.sin(ang)
    n = ang.shape[0]
    rest = head_dim - rot_dim
    cos_h = np.concatenate([cos, cos, np.ones((n, rest))], axis=-1)
    sl_h = np.concatenate([-sin, np.zeros((n, half + rest))], axis=-1)
    sr_h = np.concatenate([np.zeros((n, half)), sin, np.zeros((n, rest))], axis=-1)
    rep = LANES // head_dim
    return jnp.asarray(np.stack([np.tile(t, (1, rep)) for t in (cos_h, sl_h, sr_h)]), F32)


def _gelu_tanh(x):
    return 0.5 * x * (1.0 + jnp.tanh(math.sqrt(2.0 / math.pi) * (x + 0.044715 * (x * x * x))))


def _compress_body(kt_ref, vt_ref, pe_ref, kw1_ref, kw2_ref, vw1_ref, vw2_ref, tab_ref,
                   kc_ref, vcT_ref):
    pe = pe_ref[...]
    n = kt_ref.shape[1]

    def mlp(tok_ref, w1_ref, w2_ref):
        c = tok_ref[0].astype(F32)
        lo = (c + pe[0:1]).astype(BF16)
        hi = (c + pe[1:2]).astype(BF16)
        out = jnp.zeros((n, LANES), F32)
        for g in range(NSA_GROUPS):
            top = jnp.dot(lo, w1_ref[g, 0], preferred_element_type=F32)
            bot = jnp.dot(hi, w1_ref[g, 1], preferred_element_type=F32)
            hid = _gelu_tanh(top + pltpu.roll(bot, n - 1, 0))
            out = out + jnp.dot(hid.astype(BF16), w2_ref[g], preferred_element_type=F32)
        return out

    kc_ref[0] = _rope_slab(mlp(kt_ref, kw1_ref, kw2_ref), tab_ref, ROPE_DIM // 2).astype(kc_ref.dtype)
    vc = mlp(vt_ref, vw1_ref, vw2_ref)
    for r in range(n // LANES):
        vcT_ref[0, :, r * LANES:(r + 1) * LANES] = vc[r * LANES:(r + 1) * LANES, :].T.astype(vcT_ref.dtype)


def _compress(k_tok, v_tok, pe, kw1, kw2, vw1, vw2, tab):
    B, n, w = k_tok.shape
    tok = pl.BlockSpec((1, n, w), lambda b: (b, 0, 0))
    return pl.pallas_call(
        _compress_body,
        grid=(B,),
        in_specs=[tok, tok, _resident(pe.shape), _resident(kw1.shape), _resident(kw2.shape),
                  _resident(vw1.shape), _resident(vw2.shape), _resident(tab.shape)],
        out_specs=[pl.BlockSpec((1, n, LANES), lambda b: (b, 0, 0)),
                   pl.BlockSpec((1, LANES, n), lambda b: (b, 0, 0))],
        out_shape=[jax.ShapeDtypeStruct((B, n, LANES), BF16), jax.ShapeDtypeStruct((B, LANES, n), BF16)],
        compiler_params=pltpu.CompilerParams(dimension_semantics=("parallel",),
                                             vmem_limit_bytes=VMEM_LIMIT),
        name="compress",
    )(k_tok, v_tok, pe, kw1, kw2, vw1, vw2, tab)


def _compress_weights(w1, w2, pe):
    G, dh = NSA_GROUPS, NSA_DH
    hid = w1.shape[1]
    halves = w1.reshape(2, CMP_STRIDE, dh, hid)
    w1e = jnp.zeros((G, 2, CMP_STRIDE, G, dh, hid), F32)
    w2e = jnp.zeros((G, hid, G, dh), F32)
    for g in range(G):
        w1e = w1e.at[g, :, :, g].set(halves)
        w2e = w2e.at[g, :, g].set(w2)
    pe_e = jnp.broadcast_to(pe.reshape(2, CMP_STRIDE, 1, dh), (2, CMP_STRIDE, G, dh))
    return (w1e.reshape(G, 2, CMP_STRIDE * G * dh, hid).astype(BF16), w2e.reshape(G, hid, G * dh).astype(BF16),
            pe_e.reshape(2, CMP_STRIDE * G * dh))


def _nsa_body(q_ref, gate_ref, kc_ref, vcT_ref, ks_ref, vsT_ref, kw_ref, vwT_ref, c2sT_ref, et_ref,
              o_ref, qT_scr, gT_scr, m_scr, acc_scr, rank_scr, *, kt, top_n, n_slc, hg, dh):
    qb = q_ref.shape[0]
    nh = q_ref.shape[1] // dh
    groups = nh // hg
    ncp = kc_ref.shape[1]
    c = pl.program_id(1)
    q0 = c * qb
    ones_rows = jnp.ones((BF16_ROWS, 1), BF16)
    grp_rows = lambda g: slice(g * dh, (g + 1) * dh)

    def v_tiles(vT_ref, k0, n):
        t = vT_ref[0, pl.ds(k0 // LANES, n // LANES)]
        vT = jnp.concatenate([t[i] for i in range(n // LANES)], axis=1)
        ones = jnp.broadcast_to(ones_rows, (BF16_ROWS, n))
        return [jnp.concatenate([vT[grp_rows(g)], ones], axis=0) for g in range(groups)]

    def tq(n):
        return q0 + lax.broadcasted_iota(jnp.int32, (n, qb), 1)

    def krow(n, k0=0):
        return k0 + lax.broadcasted_iota(jnp.int32, (n, qb), 0)

    qT_scr[...] = jnp.zeros(qT_scr.shape, BF16)
    for r in range(qb // LANES):
        rows = slice(r * LANES, (r + 1) * LANES)
        for j in range(nh // 2):
            t = q_ref[rows, j * LANES:(j + 1) * LANES].astype(F32).T.astype(BF16)
            for half in range(2):
                h = 2 * j + half
                qT_scr[grp_rows(h // hg), h * qb + r * LANES:h * qb + (r + 1) * LANES] = (
                    t[half * dh:(half + 1) * dh])
        gT_scr[:, rows] = jax.nn.sigmoid(gate_ref[rows, :LANES].astype(F32).T)

    qT = qT_scr[:LANES, :]

    span = WINDOW + LANES

    def window_tile(r):
        t0 = q0 + r * LANES
        w0 = pl.multiple_of(jnp.maximum(t0 - WINDOW, 0), LANES)
        dist = (t0 + lax.broadcasted_iota(jnp.int32, (span, LANES), 1)
                - (w0 + lax.broadcasted_iota(jnp.int32, (span, LANES), 0)))
        bias_w = jnp.where((dist >= 0) & (dist < WINDOW), 0.0, NEG_INF)
        bias_w = jnp.concatenate([bias_w, bias_w], axis=1)
        qT_r = jnp.concatenate([qT[:, h * qb + r * LANES:h * qb + (r + 1) * LANES] for h in range(nh)], axis=1)
        s_w = jnp.dot(kw_ref[0, pl.ds(w0, span), :], qT_r, preferred_element_type=F32)
        vwT = v_tiles(vwT_ref, w0, span)
        outs_r = []
        for hp in range(nh // 2):
            sh = s_w[:, hp * 2 * LANES:(hp + 1) * 2 * LANES] + bias_w
            p = jnp.exp2(sh - jnp.max(sh, axis=0, keepdims=True)).astype(BF16)
            acc_w = jnp.dot(vwT[2 * hp // hg], p, preferred_element_type=F32)
            ow = acc_w[:dh] * (1.0 / acc_w[dh:dh + 1])
            outs_r += [ow[:, :LANES], ow[:, LANES:]]
        return outs_r

    n_lt = qb // LANES

    s_c = jnp.dot(kc_ref[0], qT, preferred_element_type=F32)
    cend = krow(ncp) * CMP_STRIDE + (CMP_BLOCK - 1)
    bias_c = jnp.where(cend <= tq(ncp), 0.0, NEG_INF)
    bias_c = jnp.concatenate([bias_c, bias_c], axis=1)
    nb = dh
    vcT = vcT_ref[0]
    lhs_c = [jnp.concatenate([vcT[grp_rows(g)], jnp.broadcast_to(ones_rows, (BF16_ROWS, ncp)),
                              c2sT_ref[:nb, :]], axis=0) for g in range(groups)]
    imps = [jnp.zeros((nb, qb), F32) for _ in range(groups)]
    o_c = []
    for hp in range(nh // 2):
        g = 2 * hp // hg
        sh = s_c[:, hp * 2 * qb:(hp + 1) * 2 * qb] + bias_c
        e = jnp.exp2(sh - jnp.max(sh, axis=0, keepdims=True)).astype(BF16)
        r = jnp.dot(lhs_c[g], e, preferred_element_type=F32)
        inv_l = 1.0 / r[dh:dh + 1]
        o_c.append(r[:dh] * inv_l)
        w = r[dh + BF16_ROWS:] * inv_l
        imps[g] = imps[g] + w[:, :qb] + w[:, qb:]
    o_c = jnp.concatenate(o_c, axis=1)
    sb = krow(nb)
    cur = jnp.right_shift(tq(nb), int(math.log2(SLC_BLOCK)))
    forced = (sb == 0) | (sb == cur) | (sb == cur - 1)
    future = sb > cur
    imps = [jnp.where(forced, FORCED_SCORE, jnp.where(future, -FORCED_SCORE, imp)) for imp in imps]
    sub = SUBLANES
    rank_scr[...] = jnp.zeros(rank_scr.shape, jnp.int32)
    last_block = (q0 + qb - 1) // SLC_BLOCK
    sb_l = sb[:, :LANES]
    for grp in range(n_slc // sub):
        @pl.when((grp * sub <= last_block) & (last_block >= top_n))
        def _():
            for g in range(groups):
                for lc in range(qb // LANES):
                    imp_l = imps[g][:, lc * LANES:(lc + 1) * LANES]
                    chunks = [imp_l[r * sub:(r + 1) * sub] for r in range(nb // sub)]
                    ranks = [jnp.zeros((sub, LANES), jnp.int32) for _ in chunks]
                    for sp in range(grp * sub, (grp + 1) * sub):
                        row = imp_l[sp:sp + 1, :]
                        for r, blk in enumerate(chunks):
                            if r * sub > sp:
                                beats = row >= blk
                            elif (r + 1) * sub - 1 <= sp:
                                beats = row > blk
                            else:
                                beats = (row > blk) | ((row == blk) & (sb_l[r * sub:(r + 1) * sub] > sp))
                            ranks[r] = ranks[r] + jnp.where(beats, 1, 0)
                    rank_scr[g, :, lc * LANES:(lc + 1) * LANES] += jnp.concatenate(ranks, axis=0)
    for g in range(groups):
        selneg = jnp.where((rank_scr[g] < top_n) & jnp.logical_not(future), 0.0, NEG_INF).astype(BF16)
        for h in range(g * hg, (g + 1) * hg):
            qT_scr[LANES:LANES + nb, h * qb:(h + 1) * qb] = selneg

    qT_aug = qT_scr[...]

    def scores(j):
        k0 = pl.multiple_of(j * kt, kt)
        ka = jnp.concatenate([ks_ref[0, pl.ds(k0, kt), :], et_ref[pl.ds(k0, kt), :]], axis=1)
        return jnp.dot(ka, qT_aug, preferred_element_type=F32)

    def fold(s, j, causal):
        vT = v_tiles(vsT_ref, pl.multiple_of(j * kt, kt), kt)
        if causal:
            bias = jnp.where(krow(kt, j * kt) <= tq(kt), 0.0, NEG_INF)
            bias = jnp.concatenate([bias, bias], axis=1)
        for hp in range(nh // 2):
            sl = slice(hp * 2 * qb, (hp + 1) * 2 * qb)
            sh = s[:, sl] + bias if causal else s[:, sl]
            m_old = m_scr[:, sl]
            m_new = jnp.maximum(m_old, jnp.max(sh, axis=0, keepdims=True))
            pv = jnp.dot(vT[2 * hp // hg], jnp.exp2(sh - m_new).astype(BF16), preferred_element_type=F32)
            acc_scr[:, sl] = jnp.exp2(m_old - m_new) * acc_scr[:, sl] + pv
            m_scr[:, sl] = m_new

    m_scr[...] = jnp.full(m_scr.shape, NEG_INF, F32)
    acc_scr[...] = jnp.zeros(acc_scr.shape, F32)

    def run_tiles(j0, n):
        k0 = pl.multiple_of(j0 * kt, kt)
        ka = jnp.concatenate([ks_ref[0, pl.ds(k0, n * kt), :], et_ref[pl.ds(k0, n * kt), :]], axis=1)
        ss = jnp.dot(ka, qT_aug, preferred_element_type=F32)
        for t in range(n):
            fold(ss[t * kt:(t + 1) * kt], j0 + t, False)

    def main_pair(i, carry):
        run_tiles(2 * i, 2)
        return carry

    lax.fori_loop(0, c // 2, main_pair, 0)

    @pl.when(c % 2 == 1)
    def _():
        fold(scores(c - 1), c - 1, False)

    fold(scores(c), c, True)
    o_s = acc_scr[:dh, :] * (1.0 / acc_scr[dh:dh + 1, :])

    o_w = [window_tile(r) for r in range(n_lt)]
    o_w = jnp.concatenate([o_w[r][h] for h in range(nh) for r in range(n_lt)], axis=1)

    has_cmp = (tq(1) >= CMP_BLOCK - 1).astype(F32)
    gts = gT_scr[:3 * nh, :]
    outs = []
    for h in range(nh):
        sl = slice(h * qb, (h + 1) * qb)
        outs.append(gts[3 * h:3 * h + 1] * has_cmp * o_c[:, sl] + gts[3 * h + 1:3 * h + 2] * o_s[:, sl]
                    + gts[3 * h + 2:3 * h + 3] * o_w[:, sl])
    for j in range(nh // 2):
        pair = jnp.concatenate([outs[2 * j], outs[2 * j + 1]], axis=0)
        for r in range(qb // LANES):
            rows = slice(r * LANES, (r + 1) * LANES)
            o_ref[rows, j * LANES:(j + 1) * LANES] = pair[:, rows].T.astype(o_ref.dtype)


def _nsa(qk, gates, kc, vcT, vsT, vwT, c2sT, et, B, S, *, qb=256):
    G, Hg, dh = NSA_GROUPS, NSA_HG, NSA_DH
    T = B * S
    nq = S // qb
    n_slc = S // SLC_BLOCK
    kt = qb
    assert n_slc <= dh and 2 * dh == LANES and qb % LANES == 0 and S % qb == 0 and WINDOW % qb == 0
    assert S >= WINDOW + qb
    qw = G * Hg * dh
    assert 3 * G * Hg <= LANES and G * dh == LANES
    ncp = kc.shape[1]
    k_col = qw // LANES
    g_col = gates.shape[1] // MXU_N - 1
    seq_k = lambda col: pl.BlockSpec((1, S, LANES), lambda b, c: (b, 0, col))
    seq_vT = pl.BlockSpec((1, S // LANES, LANES, LANES), lambda b, c: (b, 0, 0, 0))
    qk3 = qk.reshape(B, S, qk.shape[1])
    cols = G * Hg * qb
    return pl.pallas_call(
        functools.partial(_nsa_body, kt=kt, top_n=min(SLC_TOPN, n_slc), n_slc=n_slc, hg=Hg, dh=dh),
        grid=(B, nq),
        in_specs=[pl.BlockSpec((qb, qw), lambda b, c: (b * nq + c, 0)),
                  pl.BlockSpec((qb, MXU_N), lambda b, c: (b * nq + c, g_col)),
                  pl.BlockSpec((1, ncp, LANES), lambda b, c: (b, 0, 0)),
                  pl.BlockSpec((1, LANES, ncp), lambda b, c: (b, 0, 0)),
                  seq_k(k_col), seq_vT, seq_k(k_col + 1), seq_vT,
                  _resident(c2sT.shape), _resident(et.shape)],
        out_specs=pl.BlockSpec((qb, qw), lambda b, c: (b * nq + c, 0)),
        out_shape=jax.ShapeDtypeStruct((T, qw), BF16),
        scratch_shapes=[pltpu.VMEM((2 * LANES, cols), BF16), pltpu.VMEM((LANES, qb), F32),
                        pltpu.VMEM((1, cols), F32), pltpu.VMEM((dh + BF16_ROWS, cols), F32),
                        pltpu.VMEM((G, dh, qb), jnp.int32)],
        compiler_params=pltpu.CompilerParams(dimension_semantics=("parallel", "arbitrary"),
                                             vmem_limit_bytes=VMEM_LIMIT),
        name="nsa",
    )(qk, gates, kc, vcT, qk3, vsT, qk3, vwT, c2sT, et)


def _nsa_constants(S):
    ncp = S // CMP_STRIDE
    n_slc = S // SLC_BLOCK
    cmp_start = np.arange(ncp) * CMP_STRIDE
    s_start = np.arange(n_slc) * SLC_BLOCK
    overlap = np.clip(np.minimum(cmp_start[:, None] + CMP_BLOCK, s_start[None, :] + SLC_BLOCK)
                      - np.maximum(cmp_start[:, None], s_start[None, :]), 0, None)
    c2sT = np.zeros((LANES, ncp), np.float32)
    c2sT[:n_slc, :] = overlap.T.astype(np.float32) / CMP_BLOCK
    c2sT[:, ncp - 1] = 0.0
    et = np.zeros((S, LANES), np.float32)
    et[np.arange(S), np.arange(S) // SLC_BLOCK] = 1.0
    return jnp.asarray(c2sT, BF16), jnp.asarray(et, BF16)


def _ret_body(q_ref, kT_ref, v_ref, decay_ref, zeta_ref, xi_ref, gch_ref, gn_ref, o_ref, *, C, dk):
    S = q_ref.shape[1]
    dv = gn_ref.shape[2]
    heads = LANES // dk
    N = S // C
    units = [(n, a) for n in range(N) for a in range(heads)]
    rows = lambda n: slice(n * C, (n + 1) * C)
    val = lambda n, a: v_ref[0, rows(n), a * dv:(a + 1) * dv]
    qp = lambda n: q_ref[0, rows(n), :]

    def own_rows(x, a):
        return jnp.concatenate([x if b == a else jnp.zeros_like(x) for b in range(heads)], axis=0)

    kT = {(n, a): kT_ref[0, 0, n, a * dk:(a + 1) * dk, :] for n, a in units}
    inner = {u: jnp.dot(qp(u[0]), own_rows(kT[u], u[1]), preferred_element_type=F32) for u in units}
    inner = {u: (inner[u] * decay_ref[u[1]]).astype(BF16) for u in units}
    o = {u: jnp.dot(inner[u], val(*u), preferred_element_type=F32) for u in units}
    kv = {(n, a): jnp.dot((kT[(n, a)].astype(F32) * zeta_ref[a]).astype(BF16), val(n, a),
                          preferred_element_type=F32) for n, a in units}
    state = {}
    for a in range(heads):
        R = jnp.zeros((dk, dv), F32)
        for n in range(N):
            state[(n, a)] = own_rows(R.astype(BF16), a)
            R = gch_ref[a] * R + kv[(n, a)]
    o = {u: o[u] + xi_ref[u[1]] * jnp.dot(qp(u[0]), state[u], preferred_element_type=F32) for u in units}
    mu = {u: jnp.mean(o[u], axis=-1, keepdims=True) for u in units}
    d = {u: o[u] - mu[u] for u in units}
    var = {u: jnp.mean(d[u] * d[u], axis=-1, keepdims=True) for u in units}
    for n, a in units:
        o_ref[0, rows(n), a * dv:(a + 1) * dv] = (d[(n, a)] * lax.rsqrt(var[(n, a)] + GN_EPS)
                                                  * gn_ref[a]).astype(o_ref.dtype)


def _retention(qk, kT, v, gn_gain, B, S, *, C=RET_CHUNK):
    H, dk, dv = RET_HEADS, RET_DK, RET_DV
    hp = LANES // dk
    assert C == LANES and dv == LANES
    log_g = np.log(1.0 - 2.0 ** (-5.0 - np.arange(H, dtype=np.float64)))
    i = np.arange(C, dtype=np.float64)
    diff = i[:, None] - i[None, :]
    const = lambda a: jnp.asarray(a, F32)
    decay = const(np.where(diff >= 0, np.exp(np.maximum(diff, 0.0) * log_g[:, None, None]), 0.0))
    zeta = const(np.exp((C - 1.0 - i)[None, :] * log_g[:, None]).reshape(H, 1, C))
    xi = const(np.broadcast_to(np.exp((i + 1.0)[None, :] * log_g[:, None])[:, :, None], (H, C, LANES)))
    gch = const(np.broadcast_to(np.exp(C * log_g)[:, None, None], (H, 1, dv)))
    per_pair = lambda shape: pl.BlockSpec((hp,) + shape, lambda b, p: (p, 0, 0))
    qk3 = qk.reshape(B, S, qk.shape[1])
    return pl.pallas_call(
        functools.partial(_ret_body, C=C, dk=dk),
        grid=(B, H // hp),
        in_specs=[pl.BlockSpec((1, S, LANES), lambda b, p: (b, 0, p)),
                  pl.BlockSpec((1, 1, S // LANES, LANES, LANES), lambda b, p: (b, p, 0, 0, 0)),
                  pl.BlockSpec((1, S, hp * dv), lambda b, p: (b, 0, p)),
                  per_pair((C, C)), per_pair((1, C)), per_pair((C, LANES)), per_pair((1, dv)),
                  per_pair((1, dv))],
        out_specs=pl.BlockSpec((1, S, hp * dv), lambda b, p: (b, 0, p)),
        out_shape=jax.ShapeDtypeStruct((B, S, H * dv), BF16),
        compiler_params=pltpu.CompilerParams(dimension_semantics=("parallel", "parallel"),
                                             vmem_limit_bytes=VMEM_LIMIT),
        name="retention",
    )(qk3, kT, v.reshape(B, S, H * dv), decay, zeta, xi, gch, gn_gain.reshape(H, 1, dv))


def _merge_body(x_ref, a_ref, r_ref, gr_ref, ga_ref, gb_ref, wa_ref, wr_ref, wo_ref, o_ref):
    g = gr_ref[...].astype(F32)
    r = (g * jax.nn.sigmoid(g) * r_ref[...].astype(F32)).astype(BF16)
    ya = jnp.dot(a_ref[...], wa_ref[...], preferred_element_type=F32)
    yr = jnp.dot(r, wr_ref[...], preferred_element_type=F32)
    mixed = jax.nn.sigmoid(ga_ref[...].astype(F32)) * ya + jax.nn.sigmoid(gb_ref[...].astype(F32)) * yr
    o_ref[...] = x_ref[...] + jnp.dot(mixed.astype(BF16), wo_ref[...], preferred_element_type=F32)


def _merge(x2d, a, r, gates, wa, wr, wo, *, tm=512):
    T, D = x2d.shape
    col = lambda j: pl.BlockSpec((tm, D), lambda i: (i, j))
    return pl.pallas_call(
        _merge_body,
        grid=(T // tm,),
        in_specs=[col(0), col(0), col(0), col(0), col(1), col(2),
                  _resident(wa.shape), _resident(wr.shape), _resident(wo.shape)],
        out_specs=col(0),
        out_shape=jax.ShapeDtypeStruct((T, D), F32),
        compiler_params=pltpu.CompilerParams(dimension_semantics=("parallel",),
                                             vmem_limit_bytes=VMEM_LIMIT),
        name="merge",
    )(x2d, a, r, gates, gates, gates, wa, wr, wo)


def _layer(x, p, final_norm):
    B, S, D = x.shape
    T = B * S
    G, Hg, dh = NSA_GROUPS, NSA_HG, NSA_DH
    H, dk, dv = RET_HEADS, RET_DK, RET_DV
    bf = lambda w: w.astype(BF16)

    x1 = _ffn(x.reshape(T, D), p["ffn1_norm"], bf(p["ffn1_w_gate"]), bf(p["ffn1_w_up"]),
              bf(p["ffn1_w_down"]), final_norm, final=False)

    pos = np.arange(S)
    tab_a = _rope_tables(pos, ROPE_DIM, dh, ROPE_THETA)
    tab_b = _rope_tables(pos, dk, dk, RET_ROPE_THETA)
    qk_n, qk_r, krT, v_ret, kc_tok, vc_tok, vsT, vwT, gates = _proj(
        x1, p["mix_norm"], bf(p["w_in"].T), tab_a, tab_b, B, S)

    chunk = lambda t: t.reshape(B, S // CMP_STRIDE, CMP_STRIDE * G * dh)
    kw1, kw2, pe = _compress_weights(p["cmp_k_w1"], p["cmp_k_w2"], p["cmp_pos_emb"])
    vw1, vw2, _ = _compress_weights(p["cmp_v_w1"], p["cmp_v_w2"], p["cmp_pos_emb"])
    cmp_end = np.arange(S // CMP_STRIDE) * CMP_STRIDE + (CMP_BLOCK - 1)
    kc, vcT = _compress(chunk(kc_tok), chunk(vc_tok), pe, kw1, kw2, vw1, vw2,
                        _rope_tables(cmp_end, ROPE_DIM, dh, ROPE_THETA))
    c2sT, et = _nsa_constants(S)
    a_out = _nsa(qk_n, gates, kc, vcT, vsT, vwT, c2sT, et, B, S)
    r_out = _retention(qk_r, krT, v_ret, p["ret_gn_gain"], B, S).reshape(T, H * dv)

    x2 = _merge(x1, a_out, r_out, gates, bf(p["w_branch_nsa"]), bf(p["w_branch_ret"]), bf(p["w_out"]))
    x3 = _ffn(x2, p["ffn2_norm"], bf(p["ffn2_w_gate"]), bf(p["ffn2_w_up"]), bf(p["ffn2_w_down"]),
              final_norm, final=True)
    return x3.reshape(B, S, D)


def kernel(x, ffn1_norm, ffn1_w_gate, ffn1_w_up, ffn1_w_down, mix_norm, w_in, cmp_pos_emb, cmp_k_w1,
           cmp_k_w2, cmp_v_w1, cmp_v_w2, ret_gn_gain, w_branch_nsa, w_branch_ret, w_out, ffn2_norm,
           ffn2_w_gate, ffn2_w_up, ffn2_w_down, final_norm):
    assert ffn1_norm.shape[0] == 1, "single-layer stack"
    names = ("ffn1_norm", "ffn1_w_gate", "ffn1_w_up", "ffn1_w_down", "mix_norm", "w_in", "cmp_pos_emb",
             "cmp_k_w1", "cmp_k_w2", "cmp_v_w1", "cmp_v_w2", "ret_gn_gain", "w_branch_nsa",
             "w_branch_ret", "w_out", "ffn2_norm", "ffn2_w_gate", "ffn2_w_up", "ffn2_w_down")
    vals = (ffn1_norm, ffn1_w_gate, ffn1_w_up, ffn1_w_down, mix_norm, w_in, cmp_pos_emb, cmp_k_w1,
            cmp_k_w2, cmp_v_w1, cmp_v_w2, ret_gn_gain, w_branch_nsa, w_branch_ret, w_out, ffn2_norm,
            ffn2_w_gate, ffn2_w_up, ffn2_w_down)
    p = {n: v[0] for n, v in zip(names, vals)}
    return _layer(x, p, final_norm)
```

```python
import functools
import math

import numpy as np
import jax
import jax.numpy as jnp
from jax import lax
from jax.experimental import pallas as pl
from jax.experimental.pallas import tpu as pltpu

F32 = jnp.float32
BF16 = jnp.bfloat16

NSA_HEADS = 16
NSA_GROUPS = 2
NSA_HG = NSA_HEADS // NSA_GROUPS
NSA_DH = 64
CMP_BLOCK = 32
CMP_STRIDE = 16
SLC_BLOCK = 64
SLC_TOPN = 16
WINDOW = 512
ROPE_THETA = 500000.0
ROPE_DIM = NSA_DH // 4
FORCED_SCORE = 1.0e4
RET_HEADS = 8
RET_DK = 64
RET_DV = 128
RET_CHUNK = 128
RET_ROPE_THETA = 10000.0
EPS = 1e-6
GN_EPS = 1e-5
NEG_INF = -1e30
LOG2E = math.log2(math.e)
NORM_SLACK = 1.0 + 2.0 ** -6
MIN_DENOMINATOR = 2.0 ** -60

LANES = 128
SUBLANES = 8
BF16_ROWS = 16
MXU_N = 256
VMEM_LIMIT = 56 * 1024 * 1024

NT_DIMS = (((1,), (1,)), ((), ()))


def _rms(x, g, eps=EPS):
    return x * lax.rsqrt(jnp.mean(x * x, axis=-1, keepdims=True) + eps) * g


def _resident(shape):
    nd = len(shape)
    return pl.BlockSpec(shape, lambda *_: (0,) * nd, pipeline_mode=pl.Buffered(1))


def _ffn_body(x_ref, g_ref, wg_ref, wu_ref, wd_ref, fg_ref, o_ref, *, cuts, final):
    x = x_ref[...]
    h = _rms(x, g_ref[...]).astype(BF16)
    acc = jnp.zeros(x.shape, F32)
    for lo, hi in zip(cuts[:-1], cuts[1:]):
        sl = slice(lo, hi)
        g = jnp.dot(h, wg_ref[:, sl], preferred_element_type=F32)
        u = jnp.dot(h, wu_ref[:, sl], preferred_element_type=F32)
        a = (g * jax.nn.sigmoid(g) * u).astype(BF16)
        acc = acc + jnp.dot(a, wd_ref[sl, :], preferred_element_type=F32)
    y = x + 0.5 * acc
    if final:
        y = _rms(y, fg_ref[...])
    o_ref[...] = y


def _ffn(x2d, gain, wg, wu, wd, final_gain, *, final, tm=512):
    T, D = x2d.shape
    F = wg.shape[1]
    assert F % MXU_N == 0
    n_tiles = F // MXU_N
    cuts = (0, (n_tiles + 1) // 2 * MXU_N, F)
    tok = pl.BlockSpec((tm, D), lambda i: (i, 0))
    return pl.pallas_call(
        functools.partial(_ffn_body, cuts=cuts, final=final),
        grid=(T // tm,),
        in_specs=[tok, _resident((1, D)), _resident((D, F)), _resident((D, F)), _resident((F, D)),
                  _resident((1, D))],
        out_specs=tok,
        out_shape=jax.ShapeDtypeStruct((T, D), F32),
        compiler_params=pltpu.CompilerParams(dimension_semantics=("parallel",),
                                             vmem_limit_bytes=VMEM_LIMIT),
        name="ffn_final" if final else "ffn",
    )(x2d, gain.reshape(1, D), wg, wu, wd, final_gain.reshape(1, D))


def _in_segments(D):
    widths = dict(q_a=NSA_HEADS * NSA_DH, kc=NSA_GROUPS * NSA_DH, vc=NSA_GROUPS * NSA_DH,
                  ks=NSA_GROUPS * NSA_DH, vs=NSA_GROUPS * NSA_DH, kw=NSA_GROUPS * NSA_DH,
                  vw=NSA_GROUPS * NSA_DH, g_a=3 * NSA_HEADS, q_r=RET_HEADS * RET_DK, k_r=RET_HEADS * RET_DK,
                  v_r=RET_HEADS * RET_DV, g_r=RET_HEADS * RET_DV, g_m=2 * D)
    offs, o = {}, 0
    for name, wd in widths.items():
        offs[name] = o
        o += wd
    offs["end"] = o
    return offs


def _rope_slab(y, tab_ref, shift):
    return (y * tab_ref[0] + pltpu.roll(y, LANES - shift, 1) * tab_ref[1]
            + pltpu.roll(y, shift, 1) * tab_ref[2])


def _proj_body(x_ref, g_ref, w_ref, ta_ref, tb_ref,
               oa_ref, ob_ref, okrT_ref, ov_ref, okc_ref, ovc_ref, ovsT_ref, ovwT_ref, od_ref):
    x = x_ref[...]
    h = _rms(x, g_ref[...]).astype(BF16)
    tm = h.shape[0]
    seg = _in_segments(x.shape[1])
    per_chunk = MXU_N // LANES

    def put(o_ref, i):
        def route(ys):
            o_ref[:, i * LANES:(i + 1) * LANES] = ys.astype(o_ref.dtype)
        return route

    def put_t(o_ref, *lead):
        def route(ys):
            for r in range(tm // LANES):
                o_ref[(0,) + lead + (r,)] = ys[r * LANES:(r + 1) * LANES, :].T.astype(o_ref.dtype)
        return route

    def put_chunked(o_ref):
        def route(ys):
            o_ref[...] = ys.reshape(tm // CMP_STRIDE, CMP_STRIDE * LANES).astype(o_ref.dtype)
        return route

    def roped(route, tab_ref, shift, scale=None):
        def wrapped(ys):
            route(_rope_slab(ys if scale is None else ys * scale, tab_ref, shift))
        return wrapped

    nsa_rope = functools.partial(roped, tab_ref=ta_ref, shift=ROPE_DIM // 2)
    ret_rope = functools.partial(roped, tab_ref=tb_ref, shift=RET_DK // 2)
    n_gr = (seg["g_m"] - seg["g_r"]) // LANES
    n_gm = (seg["end"] - seg["g_m"]) // LANES
    nq = (seg["kc"] - seg["q_a"]) // LANES
    plan = []
    for i in range(0, nq, per_chunk):
        plan.append((seg["q_a"] + i * LANES,
                     [nsa_rope(put(oa_ref, i + s), scale=NSA_DH ** -0.5 * LOG2E) for s in range(per_chunk)]))
    plan.append((seg["kc"], [put_chunked(okc_ref), put_chunked(ovc_ref)]))
    plan.append((seg["ks"], [nsa_rope(put(oa_ref, nq)), put_t(ovsT_ref)]))
    plan.append((seg["kw"], [nsa_rope(put(oa_ref, nq + 1)), put_t(ovwT_ref)]))
    plan.append((seg["g_a"], [put(od_ref, n_gr + n_gm + s) for s in range(per_chunk)]))
    for name, count in (("q_r", ob_ref.shape[1] // LANES), ("k_r", okrT_ref.shape[1]),
                        ("v_r", ov_ref.shape[1] // LANES), ("g_r", n_gr), ("g_m", n_gm)):
        for i in range(0, count, per_chunk):
            routes = []
            for s in range(per_chunk):
                if name == "q_r":
                    routes.append(ret_rope(put(ob_ref, i + s), scale=RET_DK ** -0.5))
                elif name == "k_r":
                    routes.append(ret_rope(put_t(okrT_ref, i + s)))
                elif name == "v_r":
                    routes.append(put(ov_ref, i + s))
                else:
                    routes.append(put(od_ref, i + s + (n_gr if name == "g_m" else 0)))
            plan.append((seg[name] + i * LANES, routes))

    for row0, routes in plan:
        y = lax.dot_general(h, w_ref[row0:row0 + MXU_N, :], NT_DIMS, preferred_element_type=F32)
        for s, route in enumerate(routes):
            route(y[:, s * LANES:(s + 1) * LANES])


def _proj(x2d, gain, wT, tab_a, tab_b, B, S, *, tm=512):
    T, D = x2d.shape
    spt = S // tm
    seg = _in_segments(D)
    assert wT.shape == (seg["end"], D) and NSA_GROUPS * NSA_DH == LANES and seg["g_a"] + MXU_N <= seg["end"]
    assert all(o % BF16_ROWS == 0 for o in seg.values())
    na = seg["kc"] - seg["q_a"] + 2 * LANES
    nbw = seg["k_r"] - seg["q_r"]
    nkr = (seg["v_r"] - seg["k_r"]) // LANES
    nv = seg["g_r"] - seg["v_r"]
    nd = seg["end"] - seg["g_r"] + MXU_N
    tok = lambda n: pl.BlockSpec((tm, n), lambda i: (i, 0))
    tab = pl.BlockSpec((3, tm, LANES), lambda i: (0, i % spt, 0))
    tr = pl.BlockSpec((1, tm // LANES, LANES, LANES), lambda i: (i // spt, i % spt, 0, 0))
    chunked = pl.BlockSpec((tm // CMP_STRIDE, CMP_STRIDE * LANES), lambda i: (i, 0))
    tr_kr = pl.BlockSpec((1, nkr, tm // LANES, LANES, LANES), lambda i: (i // spt, 0, i % spt, 0, 0))
    return pl.pallas_call(
        _proj_body,
        grid=(T // tm,),
        in_specs=[tok(D), _resident((1, D)), _resident(wT.shape), tab, tab],
        out_specs=[tok(na), tok(nbw), tr_kr, tok(nv), chunked, chunked,
                   tr, tr, tok(nd)],
        out_shape=[jax.ShapeDtypeStruct((T, na), BF16),
                   jax.ShapeDtypeStruct((T, nbw), BF16),
                   jax.ShapeDtypeStruct((B, nkr, S // LANES, LANES, LANES), BF16),
                   jax.ShapeDtypeStruct((T, nv), BF16),
                   jax.ShapeDtypeStruct((T // CMP_STRIDE, CMP_STRIDE * LANES), BF16),
                   jax.ShapeDtypeStruct((T // CMP_STRIDE, CMP_STRIDE * LANES), BF16),
                   jax.ShapeDtypeStruct((B, S // LANES, LANES, LANES), BF16),
                   jax.ShapeDtypeStruct((B, S // LANES, LANES, LANES), BF16),
                   jax.ShapeDtypeStruct((T, nd), BF16)],
        compiler_params=pltpu.CompilerParams(dimension_semantics=("parallel",),
                                             vmem_limit_bytes=VMEM_LIMIT),
        name="proj",
    )(x2d, gain.reshape(1, D), wT, tab_a, tab_b)


def _rope_tables(pos, rot_dim, head_dim, theta):
    half = rot_dim // 2
    freqs = theta ** (-(np.arange(half, dtype=np.float64) * 2.0 / rot_dim))
    ang = np.asarray(pos, np.float64)[:, None] * freqs[None, :]
    cos, sin = np.cos(ang), np.sin(ang)
    n = ang.shape[0]
    rest = head_dim - rot_dim
    cos_h = np.concatenate([cos, cos, np.ones((n, rest))], axis=-1)
    sl_h = np.concatenate([-sin, np.zeros((n, half + rest))], axis=-1)
    sr_h = np.concatenate([np.zeros((n, half)), sin, np.zeros((n, rest))], axis=-1)
    rep = LANES // head_dim
    return jnp.asarray(np.stack([np.tile(t, (1, rep)) for t in (cos_h, sl_h, sr_h)]), F32)


def _gelu_tanh(x):
    return 0.5 * x * (1.0 + jnp.tanh(math.sqrt(2.0 / math.pi) * (x + 0.044715 * (x * x * x))))


def _compress_body(kt_ref, vt_ref, pe_ref, kw1_ref, kw2_ref, vw1_ref, vw2_ref, tab_ref,
                   kc_ref, vcT_ref):
    pe = pe_ref[...]
    n = kt_ref.shape[1]

    def mlp(tok_ref, w1_ref, w2_ref):
        c = tok_ref[0].astype(F32)
        lo = (c + pe[0:1]).astype(BF16)
        hi = (c + pe[1:2]).astype(BF16)
        out = jnp.zeros((n, LANES), F32)
        for g in range(NSA_GROUPS):
            top = jnp.dot(lo, w1_ref[g, 0], preferred_element_type=F32)
            bot = jnp.dot(hi, w1_ref[g, 1], preferred_element_type=F32)
            hid = _gelu_tanh(top + pltpu.roll(bot, n - 1, 0))
            out = out + jnp.dot(hid.astype(BF16), w2_ref[g], preferred_element_type=F32)
        return out

    kc_ref[0] = _rope_slab(mlp(kt_ref, kw1_ref, kw2_ref), tab_ref, ROPE_DIM // 2).astype(kc_ref.dtype)
    vc = mlp(vt_ref, vw1_ref, vw2_ref)
    for r in range(n // LANES):
        vcT_ref[0, :, r * LANES:(r + 1) * LANES] = vc[r * LANES:(r + 1) * LANES, :].T.astype(vcT_ref.dtype)


def _compress(k_tok, v_tok, pe, kw1, kw2, vw1, vw2, tab):
    B, n, w = k_tok.shape
    tok = pl.BlockSpec((1, n, w), lambda b: (b, 0, 0))
    return pl.pallas_call(
        _compress_body,
        grid=(B,),
        in_specs=[tok, tok, _resident(pe.shape), _resident(kw1.shape), _resident(kw2.shape),
                  _resident(vw1.shape), _resident(vw2.shape), _resident(tab.shape)],
        out_specs=[pl.BlockSpec((1, n, LANES), lambda b: (b, 0, 0)),
                   pl.BlockSpec((1, LANES, n), lambda b: (b, 0, 0))],
        out_shape=[jax.ShapeDtypeStruct((B, n, LANES), BF16), jax.ShapeDtypeStruct((B, LANES, n), BF16)],
        compiler_params=pltpu.CompilerParams(dimension_semantics=("parallel",),
                                             vmem_limit_bytes=VMEM_LIMIT),
        name="compress",
    )(k_tok, v_tok, pe, kw1, kw2, vw1, vw2, tab)


def _compress_weights(w1, w2, pe):
    G, dh = NSA_GROUPS, NSA_DH
    hid = w1.shape[1]
    halves = w1.reshape(2, CMP_STRIDE, dh, hid)
    w1e = jnp.zeros((G, 2, CMP_STRIDE, G, dh, hid), F32)
    w2e = jnp.zeros((G, hid, G, dh), F32)
    for g in range(G):
        w1e = w1e.at[g, :, :, g].set(halves)
        w2e = w2e.at[g, :, g].set(w2)
    pe_e = jnp.broadcast_to(pe.reshape(2, CMP_STRIDE, 1, dh), (2, CMP_STRIDE, G, dh))
    return (w1e.reshape(G, 2, CMP_STRIDE * G * dh, hid).astype(BF16), w2e.reshape(G, hid, G * dh).astype(BF16),
            pe_e.reshape(2, CMP_STRIDE * G * dh))


def _nsa_body(q_ref, gate_ref, kc_ref, vcT_ref, ks_ref, vsT_ref, kw_ref, vwT_ref, c2sT_ref, et_ref,
              o_ref, qT_scr, gT_scr, m_scr, acc_scr, rank_scr, kn_scr, *, kt, top_n, n_slc, hg, dh):
    qb = q_ref.shape[0]
    nh = q_ref.shape[1] // dh
    groups = nh // hg
    ncp = kc_ref.shape[1]
    c = pl.program_id(1)
    q0 = c * qb
    ones_rows = jnp.ones((BF16_ROWS, 1), BF16)
    grp_rows = lambda g: slice(g * dh, (g + 1) * dh)

    def v_tiles(vT_ref, k0, n):
        t = vT_ref[0, pl.ds(k0 // LANES, n // LANES)]
        vT = jnp.concatenate([t[i] for i in range(n // LANES)], axis=1)
        ones = jnp.broadcast_to(ones_rows, (BF16_ROWS, n))
        return [jnp.concatenate([vT[grp_rows(g)], ones], axis=0) for g in range(groups)]

    def tq(n):
        return q0 + lax.broadcasted_iota(jnp.int32, (n, qb), 1)

    def krow(n, k0=0):
        return k0 + lax.broadcasted_iota(jnp.int32, (n, qb), 0)

    @pl.when(c == 0)
    def _():
        k = ks_ref[0].astype(F32)
        lane_grp = lax.broadcasted_iota(jnp.int32, (LANES, LANES), 0) // dh
        col = lax.broadcasted_iota(jnp.int32, (LANES, LANES), 1)
        group_sum = jnp.where(lane_grp == col, 1.0, 0.0).astype(BF16)
        n2 = jnp.dot((k * k).astype(BF16), group_sum, preferred_element_type=F32)
        kn_scr[...] = jnp.broadcast_to(jnp.max(n2, axis=0, keepdims=True), kn_scr.shape)

    qT_scr[...] = jnp.zeros(qT_scr.shape, BF16)
    for r in range(qb // LANES):
        rows = slice(r * LANES, (r + 1) * LANES)
        for j in range(nh // 2):
            t = q_ref[rows, j * LANES:(j + 1) * LANES].astype(F32).T.astype(BF16)
            for half in range(2):
                h = 2 * j + half
                qT_scr[grp_rows(h // hg), h * qb + r * LANES:h * qb + (r + 1) * LANES] = (
                    t[half * dh:(half + 1) * dh])
        gT_scr[:, rows] = jax.nn.sigmoid(gate_ref[rows, :LANES].astype(F32).T)

    qT = qT_scr[:LANES, :]

    span = WINDOW + LANES

    def window_tile(r):
        t0 = q0 + r * LANES
        w0 = pl.multiple_of(jnp.maximum(t0 - WINDOW, 0), LANES)
        dist = (t0 + lax.broadcasted_iota(jnp.int32, (span, LANES), 1)
                - (w0 + lax.broadcasted_iota(jnp.int32, (span, LANES), 0)))
        bias_w = jnp.where((dist >= 0) & (dist < WINDOW), 0.0, NEG_INF)
        bias_w = jnp.concatenate([bias_w, bias_w], axis=1)
        qT_r = jnp.concatenate([qT[:, h * qb + r * LANES:h * qb + (r + 1) * LANES] for h in range(nh)], axis=1)
        s_w = jnp.dot(kw_ref[0, pl.ds(w0, span), :], qT_r, preferred_element_type=F32)
        vwT = v_tiles(vwT_ref, w0, span)
        outs_r = []
        for hp in range(nh // 2):
            sh = s_w[:, hp * 2 * LANES:(hp + 1) * 2 * LANES] + bias_w
            p = jnp.exp2(sh - jnp.max(sh, axis=0, keepdims=True)).astype(BF16)
            acc_w = jnp.dot(vwT[2 * hp // hg], p, preferred_element_type=F32)
            ow = acc_w[:dh] * (1.0 / acc_w[dh:dh + 1])
            outs_r += [ow[:, :LANES], ow[:, LANES:]]
        return outs_r

    n_lt = qb // LANES

    s_c = jnp.dot(kc_ref[0], qT, preferred_element_type=F32)
    cend = krow(ncp) * CMP_STRIDE + (CMP_BLOCK - 1)
    bias_c = jnp.where(cend <= tq(ncp), 0.0, NEG_INF)
    bias_c = jnp.concatenate([bias_c, bias_c], axis=1)
    nb = dh
    vcT = vcT_ref[0]
    lhs_c = [jnp.concatenate([vcT[grp_rows(g)], jnp.broadcast_to(ones_rows, (BF16_ROWS, ncp)),
                              c2sT_ref[:nb, :]], axis=0) for g in range(groups)]
    imps = [jnp.zeros((nb, qb), F32) for _ in range(groups)]
    o_c = []
    for hp in range(nh // 2):
        g = 2 * hp // hg
        sh = s_c[:, hp * 2 * qb:(hp + 1) * 2 * qb] + bias_c
        e = jnp.exp2(sh - jnp.max(sh, axis=0, keepdims=True)).astype(BF16)
        r = jnp.dot(lhs_c[g], e, preferred_element_type=F32)
        inv_l = 1.0 / r[dh:dh + 1]
        o_c.append(r[:dh] * inv_l)
        w = r[dh + BF16_ROWS:] * inv_l
        imps[g] = imps[g] + w[:, :qb] + w[:, qb:]
    o_c = jnp.concatenate(o_c, axis=1)
    sb = krow(nb)
    cur = jnp.right_shift(tq(nb), int(math.log2(SLC_BLOCK)))
    forced = (sb == 0) | (sb == cur) | (sb == cur - 1)
    future = sb > cur
    imps = [jnp.where(forced, FORCED_SCORE, jnp.where(future, -FORCED_SCORE, imp)) for imp in imps]
    sub = SUBLANES
    rank_scr[...] = jnp.zeros(rank_scr.shape, jnp.int32)
    last_block = (q0 + qb - 1) // SLC_BLOCK
    sb_l = sb[:, :LANES]
    for grp in range(n_slc // sub):
        @pl.when((grp * sub <= last_block) & (last_block >= top_n))
        def _():
            for g in range(groups):
                for lc in range(qb // LANES):
                    imp_l = imps[g][:, lc * LANES:(lc + 1) * LANES]
                    chunks = [imp_l[r * sub:(r + 1) * sub] for r in range(nb // sub)]
                    ranks = [jnp.zeros((sub, LANES), jnp.int32) for _ in chunks]
                    for sp in range(grp * sub, (grp + 1) * sub):
                        row = imp_l[sp:sp + 1, :]
                        for r, blk in enumerate(chunks):
                            if r * sub > sp:
                                beats = row >= blk
                            elif (r + 1) * sub - 1 <= sp:
                                beats = row > blk
                            else:
                                beats = (row > blk) | ((row == blk) & (sb_l[r * sub:(r + 1) * sub] > sp))
                            ranks[r] = ranks[r] + jnp.where(beats, 1, 0)
                    rank_scr[g, :, lc * LANES:(lc + 1) * LANES] += jnp.concatenate(ranks, axis=0)
    for g in range(groups):
        selneg = jnp.where((rank_scr[g] < top_n) & jnp.logical_not(future), 0.0, NEG_INF).astype(BF16)
        for h in range(g * hg, (g + 1) * hg):
            qT_scr[LANES:LANES + nb, h * qb:(h + 1) * qb] = selneg

    qT_aug = qT_scr[...]

    def scores(j):
        k0 = pl.multiple_of(j * kt, kt)
        ka = jnp.concatenate([ks_ref[0, pl.ds(k0, kt), :], et_ref[pl.ds(k0, kt), :]], axis=1)
        return jnp.dot(ka, qT_aug, preferred_element_type=F32)

    def fold(s, j, causal):
        vT = v_tiles(vsT_ref, pl.multiple_of(j * kt, kt), kt)
        if causal:
            bias = jnp.where(krow(kt, j * kt) <= tq(kt), 0.0, NEG_INF)
            bias = jnp.concatenate([bias, bias], axis=1)
        for hp in range(nh // 2):
            sl = slice(hp * 2 * qb, (hp + 1) * 2 * qb)
            sh = s[:, sl] + bias if causal else s[:, sl]
            m_old = m_scr[:, sl]
            m_new = jnp.maximum(m_old, jnp.max(sh, axis=0, keepdims=True))
            pv = jnp.dot(vT[2 * hp // hg], jnp.exp2(sh - m_new).astype(BF16), preferred_element_type=F32)
            acc_scr[:, sl] = jnp.exp2(m_old - m_new) * acc_scr[:, sl] + pv
            m_scr[:, sl] = m_new

    q2 = qT * qT
    qn2 = jnp.dot(jnp.broadcast_to(ones_rows, (BF16_ROWS, LANES)), q2, preferred_element_type=F32)[0:1]
    kn2 = kn_scr[0:1, :]
    bound = jnp.concatenate(
        [jnp.sqrt(qn2[:, g * hg * qb:(g + 1) * hg * qb] * kn2[:, g:g + 1]) for g in range(groups)], axis=1)
    bound = bound * NORM_SLACK

    def fold_bounded(s, j, causal):
        vT = v_tiles(vsT_ref, pl.multiple_of(j * kt, kt), kt)
        if causal:
            bias = jnp.where(krow(kt, j * kt) <= tq(kt), 0.0, NEG_INF)
            bias = jnp.concatenate([bias, bias], axis=1)
        for hp in range(nh // 2):
            sl = slice(hp * 2 * qb, (hp + 1) * 2 * qb)
            sh = s[:, sl] + bias if causal else s[:, sl]
            p = jnp.exp2(sh - bound[:, sl]).astype(BF16)
            acc_scr[:, sl] += jnp.dot(vT[2 * hp // hg], p, preferred_element_type=F32)

    def selected(fold_fn):
        acc_scr[...] = jnp.zeros(acc_scr.shape, F32)

        def main_pair(i, carry):
            k0 = pl.multiple_of(2 * i * kt, kt)
            ka = jnp.concatenate([ks_ref[0, pl.ds(k0, 2 * kt), :], et_ref[pl.ds(k0, 2 * kt), :]], axis=1)
            ss = jnp.dot(ka, qT_aug, preferred_element_type=F32)
            for t in range(2):
                fold_fn(ss[t * kt:(t + 1) * kt], 2 * i + t, False)
            return carry

        lax.fori_loop(0, c // 2, main_pair, 0)

        @pl.when(c % 2 == 1)
        def _():
            fold_fn(scores(c - 1), c - 1, False)

        fold_fn(scores(c), c, True)

    selected(fold_bounded)
    o_w = [window_tile(r) for r in range(n_lt)]
    o_w = jnp.concatenate([o_w[r][h] for h in range(nh) for r in range(n_lt)], axis=1)

    @pl.when(jnp.min(acc_scr[dh:dh + 1, :]) < MIN_DENOMINATOR)
    def _():
        m_scr[...] = jnp.full(m_scr.shape, NEG_INF, F32)
        selected(fold)

    o_s = acc_scr[:dh, :] * (1.0 / acc_scr[dh:dh + 1, :])

    has_cmp = (tq(1) >= CMP_BLOCK - 1).astype(F32)
    gts = gT_scr[:3 * nh, :]
    outs = []
    for h in range(nh):
        sl = slice(h * qb, (h + 1) * qb)
        outs.append(gts[3 * h:3 * h + 1] * has_cmp * o_c[:, sl] + gts[3 * h + 1:3 * h + 2] * o_s[:, sl]
                    + gts[3 * h + 2:3 * h + 3] * o_w[:, sl])
    for j in range(nh // 2):
        pair = jnp.concatenate([outs[2 * j], outs[2 * j + 1]], axis=0)
        for r in range(qb // LANES):
            rows = slice(r * LANES, (r + 1) * LANES)
            o_ref[rows, j * LANES:(j + 1) * LANES] = pair[:, rows].T.astype(o_ref.dtype)


def _nsa(qk, gates, kc, vcT, vsT, vwT, c2sT, et, B, S, *, qb=256):
    G, Hg, dh = NSA_GROUPS, NSA_HG, NSA_DH
    T = B * S
    nq = S // qb
    n_slc = S // SLC_BLOCK
    kt = qb
    assert n_slc <= dh and 2 * dh == LANES and qb % LANES == 0 and S % qb == 0 and WINDOW % qb == 0
    assert S >= WINDOW + qb
    qw = G * Hg * dh
    assert 3 * G * Hg <= LANES and G * dh == LANES
    ncp = kc.shape[1]
    k_col = qw // LANES
    g_col = gates.shape[1] // MXU_N - 1
    seq_k = lambda col: pl.BlockSpec((1, S, LANES), lambda b, c: (b, 0, col))
    seq_vT = pl.BlockSpec((1, S // LANES, LANES, LANES), lambda b, c: (b, 0, 0, 0))
    qk3 = qk.reshape(B, S, qk.shape[1])
    cols = G * Hg * qb
    return pl.pallas_call(
        functools.partial(_nsa_body, kt=kt, top_n=min(SLC_TOPN, n_slc), n_slc=n_slc, hg=Hg, dh=dh),
        grid=(B, nq),
        in_specs=[pl.BlockSpec((qb, qw), lambda b, c: (b * nq + c, 0)),
                  pl.BlockSpec((qb, MXU_N), lambda b, c: (b * nq + c, g_col)),
                  pl.BlockSpec((1, ncp, LANES), lambda b, c: (b, 0, 0)),
                  pl.BlockSpec((1, LANES, ncp), lambda b, c: (b, 0, 0)),
                  seq_k(k_col), seq_vT, seq_k(k_col + 1), seq_vT,
                  _resident(c2sT.shape), _resident(et.shape)],
        out_specs=pl.BlockSpec((qb, qw), lambda b, c: (b * nq + c, 0)),
        out_shape=jax.ShapeDtypeStruct((T, qw), BF16),
        scratch_shapes=[pltpu.VMEM((2 * LANES, cols), BF16), pltpu.VMEM((LANES, qb), F32),
                        pltpu.VMEM((1, cols), F32), pltpu.VMEM((dh + BF16_ROWS, cols), F32),
                        pltpu.VMEM((G, dh, qb), jnp.int32), pltpu.VMEM((SUBLANES, LANES), F32)],
        compiler_params=pltpu.CompilerParams(dimension_semantics=("parallel", "arbitrary"),
                                             vmem_limit_bytes=VMEM_LIMIT),
        name="nsa",
    )(qk, gates, kc, vcT, qk3, vsT, qk3, vwT, c2sT, et)


def _nsa_constants(S):
    ncp = S // CMP_STRIDE
    n_slc = S // SLC_BLOCK
    cmp_start = np.arange(ncp) * CMP_STRIDE
    s_start = np.arange(n_slc) * SLC_BLOCK
    overlap = np.clip(np.minimum(cmp_start[:, None] + CMP_BLOCK, s_start[None, :] + SLC_BLOCK)
                      - np.maximum(cmp_start[:, None], s_start[None, :]), 0, None)
    c2sT = np.zeros((LANES, ncp), np.float32)
    c2sT[:n_slc, :] = overlap.T.astype(np.float32) / CMP_BLOCK
    c2sT[:, ncp - 1] = 0.0
    et = np.zeros((S, LANES), np.float32)
    et[np.arange(S), np.arange(S) // SLC_BLOCK] = 1.0
    return jnp.asarray(c2sT, BF16), jnp.asarray(et, BF16)


def _ret_body(q_ref, kT_ref, v_ref, decay_ref, zeta_ref, xi_ref, gch_ref, gn_ref, o_ref, *, C, dk):
    S = q_ref.shape[1]
    dv = gn_ref.shape[2]
    heads = LANES // dk
    N = S // C
    units = [(n, a) for n in range(N) for a in range(heads)]
    rows = lambda n: slice(n * C, (n + 1) * C)
    val = lambda n, a: v_ref[0, rows(n), a * dv:(a + 1) * dv]
    qp = lambda n: q_ref[0, rows(n), :]

    def own_rows(x, a):
        return jnp.concatenate([x if b == a else jnp.zeros_like(x) for b in range(heads)], axis=0)

    kT = {(n, a): kT_ref[0, 0, n, a * dk:(a + 1) * dk, :] for n, a in units}
    inner = {u: jnp.dot(qp(u[0]), own_rows(kT[u], u[1]), preferred_element_type=F32) for u in units}
    inner = {u: (inner[u] * decay_ref[u[1]]).astype(BF16) for u in units}
    o = {u: jnp.dot(inner[u], val(*u), preferred_element_type=F32) for u in units}
    kv = {(n, a): jnp.dot((kT[(n, a)].astype(F32) * zeta_ref[a]).astype(BF16), val(n, a),
                          preferred_element_type=F32) for n, a in units}
    state = {}
    for a in range(heads):
        R = jnp.zeros((dk, dv), F32)
        for n in range(N):
            state[(n, a)] = own_rows(R.astype(BF16), a)
            R = gch_ref[a] * R + kv[(n, a)]
    o = {u: o[u] + xi_ref[u[1]] * jnp.dot(qp(u[0]), state[u], preferred_element_type=F32) for u in units}
    mu = {u: jnp.mean(o[u], axis=-1, keepdims=True) for u in units}
    d = {u: o[u] - mu[u] for u in units}
    var = {u: jnp.mean(d[u] * d[u], axis=-1, keepdims=True) for u in units}
    for n, a in units:
        o_ref[0, rows(n), a * dv:(a + 1) * dv] = (d[(n, a)] * lax.rsqrt(var[(n, a)] + GN_EPS)
                                                  * gn_ref[a]).astype(o_ref.dtype)


def _retention(qk, kT, v, gn_gain, B, S, *, C=RET_CHUNK):
    H, dk, dv = RET_HEADS, RET_DK, RET_DV
    hp = LANES // dk
    assert C == LANES and dv == LANES
    log_g = np.log(1.0 - 2.0 ** (-5.0 - np.arange(H, dtype=np.float64)))
    i = np.arange(C, dtype=np.float64)
    diff = i[:, None] - i[None, :]
    const = lambda a: jnp.asarray(a, F32)
    decay = const(np.where(diff >= 0, np.exp(np.maximum(diff, 0.0) * log_g[:, None, None]), 0.0))
    zeta = const(np.exp((C - 1.0 - i)[None, :] * log_g[:, None]).reshape(H, 1, C))
    xi = const(np.broadcast_to(np.exp((i + 1.0)[None, :] * log_g[:, None])[:, :, None], (H, C, LANES)))
    gch = const(np.broadcast_to(np.exp(C * log_g)[:, None, None], (H, 1, dv)))
    per_pair = lambda shape: pl.BlockSpec((hp,) + shape, lambda b, p: (p, 0, 0))
    qk3 = qk.reshape(B, S, qk.shape[1])
    return pl.pallas_call(
        functools.partial(_ret_body, C=C, dk=dk),
        grid=(B, H // hp),
        in_specs=[pl.BlockSpec((1, S, LANES), lambda b, p: (b, 0, p)),
                  pl.BlockSpec((1, 1, S // LANES, LANES, LANES), lambda b, p: (b, p, 0, 0, 0)),
                  pl.BlockSpec((1, S, hp * dv), lambda b, p: (b, 0, p)),
                  per_pair((C, C)), per_pair((1, C)), per_pair((C, LANES)), per_pair((1, dv)),
                  per_pair((1, dv))],
        out_specs=pl.BlockSpec((1, S, hp * dv), lambda b, p: (b, 0, p)),
        out_shape=jax.ShapeDtypeStruct((B, S, H * dv), BF16),
        compiler_params=pltpu.CompilerParams(dimension_semantics=("parallel", "parallel"),
                                             vmem_limit_bytes=VMEM_LIMIT),
        name="retention",
    )(qk3, kT, v.reshape(B, S, H * dv), decay, zeta, xi, gch, gn_gain.reshape(H, 1, dv))


def _merge_body(x_ref, a_ref, r_ref, gr_ref, ga_ref, gb_ref, wa_ref, wr_ref, wo_ref, o_ref):
    g = gr_ref[...].astype(F32)
    r = (g * jax.nn.sigmoid(g) * r_ref[...].astype(F32)).astype(BF16)
    ya = jnp.dot(a_ref[...], wa_ref[...], preferred_element_type=F32)
    yr = jnp.dot(r, wr_ref[...], preferred_element_type=F32)
    mixed = jax.nn.sigmoid(ga_ref[...].astype(F32)) * ya + jax.nn.sigmoid(gb_ref[...].astype(F32)) * yr
    o_ref[...] = x_ref[...] + jnp.dot(mixed.astype(BF16), wo_ref[...], preferred_element_type=F32)


def _merge(x2d, a, r, gates, wa, wr, wo, *, tm=512):
    T, D = x2d.shape
    col = lambda j: pl.BlockSpec((tm, D), lambda i: (i, j))
    return pl.pallas_call(
        _merge_body,
        grid=(T // tm,),
        in_specs=[col(0), col(0), col(0), col(0), col(1), col(2),
                  _resident(wa.shape), _resident(wr.shape), _resident(wo.shape)],
        out_specs=col(0),
        out_shape=jax.ShapeDtypeStruct((T, D), F32),
        compiler_params=pltpu.CompilerParams(dimension_semantics=("parallel",),
                                             vmem_limit_bytes=VMEM_LIMIT),
        name="merge",
    )(x2d, a, r, gates, gates, gates, wa, wr, wo)


def _layer(x, p, final_norm):
    B, S, D = x.shape
    T = B * S
    G, Hg, dh = NSA_GROUPS, NSA_HG, NSA_DH
    H, dk, dv = RET_HEADS, RET_DK, RET_DV
    bf = lambda w: w.astype(BF16)

    x1 = _ffn(x.reshape(T, D), p["ffn1_norm"], bf(p["ffn1_w_gate"]), bf(p["ffn1_w_up"]),
              bf(p["ffn1_w_down"]), final_norm, final=False)

    pos = np.arange(S)
    tab_a = _rope_tables(pos, ROPE_DIM, dh, ROPE_THETA)
    tab_b = _rope_tables(pos, dk, dk, RET_ROPE_THETA)
    qk_n, qk_r, krT, v_ret, kc_tok, vc_tok, vsT, vwT, gates = _proj(
        x1, p["mix_norm"], bf(p["w_in"].T), tab_a, tab_b, B, S)

    chunk = lambda t: t.reshape(B, S // CMP_STRIDE, CMP_STRIDE * G * dh)
    kw1, kw2, pe = _compress_weights(p["cmp_k_w1"], p["cmp_k_w2"], p["cmp_pos_emb"])
    vw1, vw2, _ = _compress_weights(p["cmp_v_w1"], p["cmp_v_w2"], p["cmp_pos_emb"])
    cmp_end = np.arange(S // CMP_STRIDE) * CMP_STRIDE + (CMP_BLOCK - 1)
    kc, vcT = _compress(chunk(kc_tok), chunk(vc_tok), pe, kw1, kw2, vw1, vw2,
                        _rope_tables(cmp_end, ROPE_DIM, dh, ROPE_THETA))
    c2sT, et = _nsa_constants(S)
    a_out = _nsa(qk_n, gates, kc, vcT, vsT, vwT, c2sT, et, B, S)
    r_out = _retention(qk_r, krT, v_ret, p["ret_gn_gain"], B, S).reshape(T, H * dv)

    x2 = _merge(x1, a_out, r_out, gates, bf(p["w_branch_nsa"]), bf(p["w_branch_ret"]), bf(p["w_out"]))
    x3 = _ffn(x2, p["ffn2_norm"], bf(p["ffn2_w_gate"]), bf(p["ffn2_w_up"]), bf(p["ffn2_w_down"]),
              final_norm, final=True)
    return x3.reshape(B, S, D)


def kernel(x, ffn1_norm, ffn1_w_gate, ffn1_w_up, ffn1_w_down, mix_norm, w_in, cmp_pos_emb, cmp_k_w1,
           cmp_k_w2, cmp_v_w1, cmp_v_w2, ret_gn_gain, w_branch_nsa, w_branch_ret, w_out, ffn2_norm,
           ffn2_w_gate, ffn2_w_up, ffn2_w_down, final_norm):
    assert ffn1_norm.shape[0] == 1, "single-layer stack"
    names = ("ffn1_norm", "ffn1_w_gate", "ffn1_w_up", "ffn1_w_down", "mix_norm", "w_in", "cmp_pos_emb",
             "cmp_k_w1", "cmp_k_w2", "cmp_v_w1", "cmp_v_w2", "ret_gn_gain", "w_branch_nsa",
             "w_branch_ret", "w_out", "ffn2_norm", "ffn2_w_gate", "ffn2_w_up", "ffn2_w_down")
    vals = (ffn1_norm, ffn1_w_gate, ffn1_w_up, ffn1_w_down, mix_norm, w_in, cmp_pos_emb, cmp_k_w1,
            cmp_k_w2, cmp_v_w1, cmp_v_w2, ret_gn_gain, w_branch_nsa, w_branch_ret, w_out, ffn2_norm,
            ffn2_w_gate, ffn2_w_up, ffn2_w_down)
    p = {n: v[0] for n, v in zip(names, vals)}
    return _layer(x, p, final_norm)
```

```python
import functools
import math

import numpy as np
import jax
import jax.numpy as jnp
from jax import lax
from jax.experimental import pallas as pl
from jax.experimental.pallas import tpu as pltpu

F32 = jnp.float32
BF16 = jnp.bfloat16

NSA_HEADS = 16
NSA_GROUPS = 2
NSA_HG = NSA_HEADS // NSA_GROUPS
NSA_DH = 64
CMP_BLOCK = 32
CMP_STRIDE = 16
SLC_BLOCK = 64
SLC_TOPN = 16
WINDOW = 512
ROPE_THETA = 500000.0
ROPE_DIM = NSA_DH // 4
FORCED_SCORE = 1.0e4
RET_HEADS = 8
RET_DK = 64
RET_DV = 128
RET_CHUNK = 128
RET_ROPE_THETA = 10000.0
EPS = 1e-6
GN_EPS = 1e-5
NEG_INF = -1e30
LOG2E = math.log2(math.e)
NORM_SLACK = 1.0 + 2.0 ** -6
MIN_DENOMINATOR = 2.0 ** -60

LANES = 128
SUBLANES = 8
BF16_ROWS = 16
MXU_N = 256
VMEM_LIMIT = 56 * 1024 * 1024

NT_DIMS = (((1,), (1,)), ((), ()))


def _rms(x, g, eps=EPS):
    return x * lax.rsqrt(jnp.mean(x * x, axis=-1, keepdims=True) + eps) * g


def _resident(shape):
    nd = len(shape)
    return pl.BlockSpec(shape, lambda *_: (0,) * nd, pipeline_mode=pl.Buffered(1))


def _ffn_body(x_ref, g_ref, wg_ref, wu_ref, wd_ref, fg_ref, o_ref, *, cuts, final):
    x = x_ref[...]
    h = _rms(x, g_ref[...]).astype(BF16)
    acc = jnp.zeros(x.shape, F32)
    for lo, hi in zip(cuts[:-1], cuts[1:]):
        sl = slice(lo, hi)
        g = jnp.dot(h, wg_ref[:, sl], preferred_element_type=F32)
        u = jnp.dot(h, wu_ref[:, sl], preferred_element_type=F32)
        a = (g * jax.nn.sigmoid(g) * u).astype(BF16)
        acc = acc + jnp.dot(a, wd_ref[sl, :], preferred_element_type=F32)
    y = x + 0.5 * acc
    if final:
        y = _rms(y, fg_ref[...])
    o_ref[...] = y


def _ffn(x2d, gain, wg, wu, wd, final_gain, *, final, tm=512):
    T, D = x2d.shape
    F = wg.shape[1]
    assert F % MXU_N == 0
    n_tiles = F // MXU_N
    cuts = (0, (n_tiles + 1) // 2 * MXU_N, F)
    tok = pl.BlockSpec((tm, D), lambda i: (i, 0))
    return pl.pallas_call(
        functools.partial(_ffn_body, cuts=cuts, final=final),
        grid=(T // tm,),
        in_specs=[tok, _resident((1, D)), _resident((D, F)), _resident((D, F)), _resident((F, D)),
                  _resident((1, D))],
        out_specs=tok,
        out_shape=jax.ShapeDtypeStruct((T, D), F32),
        compiler_params=pltpu.CompilerParams(dimension_semantics=("parallel",),
                                             vmem_limit_bytes=VMEM_LIMIT),
        name="ffn_final" if final else "ffn",
    )(x2d, gain.reshape(1, D), wg, wu, wd, final_gain.reshape(1, D))


def _in_segments(D):
    widths = dict(q_a=NSA_HEADS * NSA_DH, kc=NSA_GROUPS * NSA_DH, vc=NSA_GROUPS * NSA_DH,
                  ks=NSA_GROUPS * NSA_DH, vs=NSA_GROUPS * NSA_DH, kw=NSA_GROUPS * NSA_DH,
                  vw=NSA_GROUPS * NSA_DH, g_a=3 * NSA_HEADS, q_r=RET_HEADS * RET_DK, k_r=RET_HEADS * RET_DK,
                  v_r=RET_HEADS * RET_DV, g_r=RET_HEADS * RET_DV, g_m=2 * D)
    offs, o = {}, 0
    for name, wd in widths.items():
        offs[name] = o
        o += wd
    offs["end"] = o
    return offs


def _rope_slab(y, tab_ref, shift):
    return (y * tab_ref[0] + pltpu.roll(y, LANES - shift, 1) * tab_ref[1]
            + pltpu.roll(y, shift, 1) * tab_ref[2])


def _proj_body(x_ref, g_ref, w_ref, ta_ref, tb_ref,
               oa_ref, ob_ref, okrT_ref, ov_ref, okc_ref, ovc_ref, ovsT_ref, ovwT_ref, od_ref):
    x = x_ref[...]
    h = _rms(x, g_ref[...]).astype(BF16)
    tm = h.shape[0]
    seg = _in_segments(x.shape[1])
    per_chunk = MXU_N // LANES

    def put(o_ref, i):
        def route(ys):
            o_ref[:, i * LANES:(i + 1) * LANES] = ys.astype(o_ref.dtype)
        return route

    def put_t(o_ref, *lead):
        def route(ys):
            for r in range(tm // LANES):
                o_ref[(0,) + lead + (r,)] = ys[r * LANES:(r + 1) * LANES, :].T.astype(o_ref.dtype)
        return route

    def put_chunked(o_ref):
        def route(ys):
            o_ref[...] = ys.reshape(tm // CMP_STRIDE, CMP_STRIDE * LANES).astype(o_ref.dtype)
        return route

    def roped(route, tab_ref, shift, scale=None):
        def wrapped(ys):
            route(_rope_slab(ys if scale is None else ys * scale, tab_ref, shift))
        return wrapped

    nsa_rope = functools.partial(roped, tab_ref=ta_ref, shift=ROPE_DIM // 2)
    ret_rope = functools.partial(roped, tab_ref=tb_ref, shift=RET_DK // 2)
    n_gr = (seg["g_m"] - seg["g_r"]) // LANES
    n_gm = (seg["end"] - seg["g_m"]) // LANES
    nq = (seg["kc"] - seg["q_a"]) // LANES
    plan = []
    for i in range(0, nq, per_chunk):
        plan.append((seg["q_a"] + i * LANES,
                     [nsa_rope(put(oa_ref, i + s), scale=NSA_DH ** -0.5 * LOG2E) for s in range(per_chunk)]))
    plan.append((seg["kc"], [put_chunked(okc_ref), put_chunked(ovc_ref)]))
    plan.append((seg["ks"], [nsa_rope(put(oa_ref, nq)), put_t(ovsT_ref)]))
    plan.append((seg["kw"], [nsa_rope(put(oa_ref, nq + 1)), put_t(ovwT_ref)]))
    plan.append((seg["g_a"], [put(od_ref, n_gr + n_gm + s) for s in range(per_chunk)]))
    for name, count in (("q_r", ob_ref.shape[1] // LANES), ("k_r", okrT_ref.shape[1]),
                        ("v_r", ov_ref.shape[1] // LANES), ("g_r", n_gr), ("g_m", n_gm)):
        for i in range(0, count, per_chunk):
            routes = []
            for s in range(per_chunk):
                if name == "q_r":
                    routes.append(ret_rope(put(ob_ref, i + s), scale=RET_DK ** -0.5))
                elif name == "k_r":
                    routes.append(ret_rope(put_t(okrT_ref, i + s)))
                elif name == "v_r":
                    routes.append(put(ov_ref, i + s))
                else:
                    routes.append(put(od_ref, i + s + (n_gr if name == "g_m" else 0)))
            plan.append((seg[name] + i * LANES, routes))

    for row0, routes in plan:
        y = lax.dot_general(h, w_ref[row0:row0 + MXU_N, :], NT_DIMS, preferred_element_type=F32)
        for s, route in enumerate(routes):
            route(y[:, s * LANES:(s + 1) * LANES])


def _proj(x2d, gain, wT, tab_a, tab_b, B, S, *, tm=512):
    T, D = x2d.shape
    spt = S // tm
    seg = _in_segments(D)
    assert wT.shape == (seg["end"], D) and NSA_GROUPS * NSA_DH == LANES and seg["g_a"] + MXU_N <= seg["end"]
    assert all(o % BF16_ROWS == 0 for o in seg.values())
    na = seg["kc"] - seg["q_a"] + 2 * LANES
    nbw = seg["k_r"] - seg["q_r"]
    nkr = (seg["v_r"] - seg["k_r"]) // LANES
    nv = seg["g_r"] - seg["v_r"]
    nd = seg["end"] - seg["g_r"] + MXU_N
    tok = lambda n: pl.BlockSpec((tm, n), lambda i: (i, 0))
    tab = pl.BlockSpec((3, tm, LANES), lambda i: (0, i % spt, 0))
    tr = pl.BlockSpec((1, tm // LANES, LANES, LANES), lambda i: (i // spt, i % spt, 0, 0))
    chunked = pl.BlockSpec((tm // CMP_STRIDE, CMP_STRIDE * LANES), lambda i: (i, 0))
    tr_kr = pl.BlockSpec((1, nkr, tm // LANES, LANES, LANES), lambda i: (i // spt, 0, i % spt, 0, 0))
    return pl.pallas_call(
        _proj_body,
        grid=(T // tm,),
        in_specs=[tok(D), _resident((1, D)), _resident(wT.shape), tab, tab],
        out_specs=[tok(na), tok(nbw), tr_kr, tok(nv), chunked, chunked,
                   tr, tr, tok(nd)],
        out_shape=[jax.ShapeDtypeStruct((T, na), BF16),
                   jax.ShapeDtypeStruct((T, nbw), BF16),
                   jax.ShapeDtypeStruct((B, nkr, S // LANES, LANES, LANES), BF16),
                   jax.ShapeDtypeStruct((T, nv), BF16),
                   jax.ShapeDtypeStruct((T // CMP_STRIDE, CMP_STRIDE * LANES), BF16),
                   jax.ShapeDtypeStruct((T // CMP_STRIDE, CMP_STRIDE * LANES), BF16),
                   jax.ShapeDtypeStruct((B, S // LANES, LANES, LANES), BF16),
                   jax.ShapeDtypeStruct((B, S // LANES, LANES, LANES), BF16),
                   jax.ShapeDtypeStruct((T, nd), BF16)],
        compiler_params=pltpu.CompilerParams(dimension_semantics=("parallel",),
                                             vmem_limit_bytes=VMEM_LIMIT),
        name="proj",
    )(x2d, gain.reshape(1, D), wT, tab_a, tab_b)


def _rope_tables(pos, rot_dim, head_dim, theta):
    half = rot_dim // 2
    freqs = theta ** (-(np.arange(half, dtype=np.float64) * 2.0 / rot_dim))
    ang = np.asarray(pos, np.float64)[:, None] * freqs[None, :]
    cos, sin = np.cos(ang), np.sin(ang)
    n = ang.shape[0]
    rest = head_dim - rot_dim
    cos_h = np.concatenate([cos, cos, np.ones((n, rest))], axis=-1)
    sl_h = np.concatenate([-sin, np.zeros((n, half + rest))], axis=-1)
    sr_h = np.concatenate([np.zeros((n, half)), sin, np.zeros((n, rest))], axis=-1)
    rep = LANES // head_dim
    return jnp.asarray(np.stack([np.tile(t, (1, rep)) for t in (cos_h, sl_h, sr_h)]), F32)


def _gelu_tanh(x):
    return 0.5 * x * (1.0 + jnp.tanh(math.sqrt(2.0 / math.pi) * (x + 0.044715 * (x * x * x))))


def _compress_body(kt_ref, vt_ref, pe_ref, kw1_ref, kw2_ref, vw1_ref, vw2_ref, tab_ref,
                   kc_ref, vcT_ref):
    pe = pe_ref[...]
    n = kt_ref.shape[1]

    def mlp(tok_ref, w1_ref, w2_ref):
        c = tok_ref[0].astype(F32)
        lo = (c + pe[0:1]).astype(BF16)
        hi = (c + pe[1:2]).astype(BF16)
        out = jnp.zeros((n, LANES), F32)
        for g in range(NSA_GROUPS):
            top = jnp.dot(lo, w1_ref[g, 0], preferred_element_type=F32)
            bot = jnp.dot(hi, w1_ref[g, 1], preferred_element_type=F32)
            hid = _gelu_tanh(top + pltpu.roll(bot, n - 1, 0))
            out = out + jnp.dot(hid.astype(BF16), w2_ref[g], preferred_element_type=F32)
        return out

    kc_ref[0] = _rope_slab(mlp(kt_ref, kw1_ref, kw2_ref), tab_ref, ROPE_DIM // 2).astype(kc_ref.dtype)
    vc = mlp(vt_ref, vw1_ref, vw2_ref)
    for r in range(n // LANES):
        vcT_ref[0, :, r * LANES:(r + 1) * LANES] = vc[r * LANES:(r + 1) * LANES, :].T.astype(vcT_ref.dtype)


def _compress(k_tok, v_tok, pe, kw1, kw2, vw1, vw2, tab):
    B, n, w = k_tok.shape
    tok = pl.BlockSpec((1, n, w), lambda b: (b, 0, 0))
    return pl.pallas_call(
        _compress_body,
        grid=(B,),
        in_specs=[tok, tok, _resident(pe.shape), _resident(kw1.shape), _resident(kw2.shape),
                  _resident(vw1.shape), _resident(vw2.shape), _resident(tab.shape)],
        out_specs=[pl.BlockSpec((1, n, LANES), lambda b: (b, 0, 0)),
                   pl.BlockSpec((1, LANES, n), lambda b: (b, 0, 0))],
        out_shape=[jax.ShapeDtypeStruct((B, n, LANES), BF16), jax.ShapeDtypeStruct((B, LANES, n), BF16)],
        compiler_params=pltpu.CompilerParams(dimension_semantics=("parallel",),
                                             vmem_limit_bytes=VMEM_LIMIT),
        name="compress",
    )(k_tok, v_tok, pe, kw1, kw2, vw1, vw2, tab)


def _compress_weights(w1, w2, pe):
    G, dh = NSA_GROUPS, NSA_DH
    hid = w1.shape[1]
    halves = w1.reshape(2, CMP_STRIDE, dh, hid)
    w1e = jnp.zeros((G, 2, CMP_STRIDE, G, dh, hid), F32)
    w2e = jnp.zeros((G, hid, G, dh), F32)
    for g in range(G):
        w1e = w1e.at[g, :, :, g].set(halves)
        w2e = w2e.at[g, :, g].set(w2)
    pe_e = jnp.broadcast_to(pe.reshape(2, CMP_STRIDE, 1, dh), (2, CMP_STRIDE, G, dh))
    return (w1e.reshape(G, 2, CMP_STRIDE * G * dh, hid).astype(BF16), w2e.reshape(G, hid, G * dh).astype(BF16),
            pe_e.reshape(2, CMP_STRIDE * G * dh))


def _nsa_body(q_ref, gate_ref, kc_ref, vcT_ref, ks_ref, vsT_ref, kw_ref, vwT_ref, c2sT_ref, et_ref,
              o_ref, qT_scr, gT_scr, m_scr, acc_scr, rank_scr, kn_scr, ow_scr, *, kt, top_n, n_slc, hg, dh):
    qb = q_ref.shape[0]
    nh = q_ref.shape[1] // dh
    groups = nh // hg
    ncp = kc_ref.shape[1]
    c = pl.program_id(1)
    q0 = c * qb
    ones_rows = jnp.ones((BF16_ROWS, 1), BF16)
    grp_rows = lambda g: slice(g * dh, (g + 1) * dh)

    def v_tiles(vT_ref, k0, n):
        t = vT_ref[0, pl.ds(k0 // LANES, n // LANES)]
        vT = jnp.concatenate([t[i] for i in range(n // LANES)], axis=1)
        ones = jnp.broadcast_to(ones_rows, (BF16_ROWS, n))
        return [jnp.concatenate([vT[grp_rows(g)], ones], axis=0) for g in range(groups)]

    def tq(n):
        return q0 + lax.broadcasted_iota(jnp.int32, (n, qb), 1)

    def krow(n, k0=0):
        return k0 + lax.broadcasted_iota(jnp.int32, (n, qb), 0)

    @pl.when(c == 0)
    def _():
        lane_grp = lax.broadcasted_iota(jnp.int32, (LANES, LANES), 0) // dh
        col = lax.broadcasted_iota(jnp.int32, (LANES, LANES), 1)
        group_sum = jnp.where(lane_grp == col, 1.0, 0.0).astype(BF16)
        for row, k_ref in enumerate((ks_ref, kw_ref)):
            k = k_ref[0].astype(F32)
            n2 = jnp.dot((k * k).astype(BF16), group_sum, preferred_element_type=F32)
            kn_scr[row:row + 1, :] = jnp.max(n2, axis=0, keepdims=True)

    qT_scr[...] = jnp.zeros(qT_scr.shape, BF16)
    for r in range(qb // LANES):
        rows = slice(r * LANES, (r + 1) * LANES)
        for j in range(nh // 2):
            t = q_ref[rows, j * LANES:(j + 1) * LANES].astype(F32).T.astype(BF16)
            for half in range(2):
                h = 2 * j + half
                qT_scr[grp_rows(h // hg), h * qb + r * LANES:h * qb + (r + 1) * LANES] = (
                    t[half * dh:(half + 1) * dh])
        gT_scr[:, rows] = jax.nn.sigmoid(gate_ref[rows, :LANES].astype(F32).T)

    qT = qT_scr[:LANES, :]

    span = WINDOW + LANES

    def window_tile(r, stabiliser):
        t0 = q0 + r * LANES
        w0 = pl.multiple_of(jnp.maximum(t0 - WINDOW, 0), LANES)
        dist = (t0 + lax.broadcasted_iota(jnp.int32, (span, LANES), 1)
                - (w0 + lax.broadcasted_iota(jnp.int32, (span, LANES), 0)))
        bias_w = jnp.where((dist >= 0) & (dist < WINDOW), 0.0, NEG_INF)
        bias_w = jnp.concatenate([bias_w, bias_w], axis=1)
        qT_r = jnp.concatenate([qT[:, h * qb + r * LANES:h * qb + (r + 1) * LANES] for h in range(nh)], axis=1)
        s_w = jnp.dot(kw_ref[0, pl.ds(w0, span), :], qT_r, preferred_element_type=F32)
        vwT = v_tiles(vwT_ref, w0, span)
        l_min = jnp.full((1, 2 * LANES), 1.0, F32)
        for hp in range(nh // 2):
            heads = (2 * hp, 2 * hp + 1)
            cols = [slice(h * qb + r * LANES, h * qb + (r + 1) * LANES) for h in heads]
            sh = s_w[:, hp * 2 * LANES:(hp + 1) * 2 * LANES] + bias_w
            if stabiliser is None:
                stab = jnp.max(sh, axis=0, keepdims=True)
            else:
                stab = jnp.concatenate([stabiliser[:, cs] for cs in cols], axis=1)
            acc_w = jnp.dot(vwT[2 * hp // hg], jnp.exp2(sh - stab).astype(BF16), preferred_element_type=F32)
            l_w = acc_w[dh:dh + 1]
            l_min = jnp.minimum(l_min, l_w)
            ow = acc_w[:dh] * (1.0 / l_w)
            for i, cs in enumerate(cols):
                ow_scr[:, cs] = ow[:, i * LANES:(i + 1) * LANES]
        return l_min

    n_lt = qb // LANES

    s_c = jnp.dot(kc_ref[0], qT, preferred_element_type=F32)
    cend = krow(ncp) * CMP_STRIDE + (CMP_BLOCK - 1)
    bias_c = jnp.where(cend <= tq(ncp), 0.0, NEG_INF)
    bias_c = jnp.concatenate([bias_c, bias_c], axis=1)
    nb = dh
    vcT = vcT_ref[0]
    lhs_c = [jnp.concatenate([vcT[grp_rows(g)], jnp.broadcast_to(ones_rows, (BF16_ROWS, ncp)),
                              c2sT_ref[:nb, :]], axis=0) for g in range(groups)]
    imps = [jnp.zeros((nb, qb), F32) for _ in range(groups)]
    o_c = []
    for hp in range(nh // 2):
        g = 2 * hp // hg
        sh = s_c[:, hp * 2 * qb:(hp + 1) * 2 * qb] + bias_c
        e = jnp.exp2(sh - jnp.max(sh, axis=0, keepdims=True)).astype(BF16)
        r = jnp.dot(lhs_c[g], e, preferred_element_type=F32)
        inv_l = 1.0 / r[dh:dh + 1]
        o_c.append(r[:dh] * inv_l)
        w = r[dh + BF16_ROWS:] * inv_l
        imps[g] = imps[g] + w[:, :qb] + w[:, qb:]
    o_c = jnp.concatenate(o_c, axis=1)
    sb = krow(nb)
    cur = jnp.right_shift(tq(nb), int(math.log2(SLC_BLOCK)))
    forced = (sb == 0) | (sb == cur) | (sb == cur - 1)
    future = sb > cur
    imps = [jnp.where(forced, FORCED_SCORE, jnp.where(future, -FORCED_SCORE, imp)) for imp in imps]
    sub = SUBLANES
    rank_scr[...] = jnp.zeros(rank_scr.shape, jnp.int32)
    last_block = (q0 + qb - 1) // SLC_BLOCK
    sb_l = sb[:, :LANES]
    for grp in range(n_slc // sub):
        @pl.when((grp * sub <= last_block) & (last_block >= top_n))
        def _():
            for g in range(groups):
                for lc in range(qb // LANES):
                    imp_l = imps[g][:, lc * LANES:(lc + 1) * LANES]
                    chunks = [imp_l[r * sub:(r + 1) * sub] for r in range(nb // sub)]
                    ranks = [jnp.zeros((sub, LANES), jnp.int32) for _ in chunks]
                    for sp in range(grp * sub, (grp + 1) * sub):
                        row = imp_l[sp:sp + 1, :]
                        for r, blk in enumerate(chunks):
                            if r * sub > sp:
                                beats = row >= blk
                            elif (r + 1) * sub - 1 <= sp:
                                beats = row > blk
                            else:
                                beats = (row > blk) | ((row == blk) & (sb_l[r * sub:(r + 1) * sub] > sp))
                            ranks[r] = ranks[r] + jnp.where(beats, 1, 0)
                    rank_scr[g, :, lc * LANES:(lc + 1) * LANES] += jnp.concatenate(ranks, axis=0)
    for g in range(groups):
        selneg = jnp.where((rank_scr[g] < top_n) & jnp.logical_not(future), 0.0, NEG_INF).astype(BF16)
        for h in range(g * hg, (g + 1) * hg):
            qT_scr[LANES:LANES + nb, h * qb:(h + 1) * qb] = selneg

    qT_aug = qT_scr[...]

    def scores(j):
        k0 = pl.multiple_of(j * kt, kt)
        ka = jnp.concatenate([ks_ref[0, pl.ds(k0, kt), :], et_ref[pl.ds(k0, kt), :]], axis=1)
        return jnp.dot(ka, qT_aug, preferred_element_type=F32)

    def fold(s, j, causal):
        vT = v_tiles(vsT_ref, pl.multiple_of(j * kt, kt), kt)
        if causal:
            bias = jnp.where(krow(kt, j * kt) <= tq(kt), 0.0, NEG_INF)
            bias = jnp.concatenate([bias, bias], axis=1)
        for hp in range(nh // 2):
            sl = slice(hp * 2 * qb, (hp + 1) * 2 * qb)
            sh = s[:, sl] + bias if causal else s[:, sl]
            m_old = m_scr[:, sl]
            m_new = jnp.maximum(m_old, jnp.max(sh, axis=0, keepdims=True))
            pv = jnp.dot(vT[2 * hp // hg], jnp.exp2(sh - m_new).astype(BF16), preferred_element_type=F32)
            acc_scr[:, sl] = jnp.exp2(m_old - m_new) * acc_scr[:, sl] + pv
            m_scr[:, sl] = m_new

    q2 = qT * qT
    qn2 = jnp.dot(jnp.broadcast_to(ones_rows, (BF16_ROWS, LANES)), q2, preferred_element_type=F32)[0:1]
    def score_bound(kn2):
        return NORM_SLACK * jnp.concatenate(
            [jnp.sqrt(qn2[:, g * hg * qb:(g + 1) * hg * qb] * kn2[:, g:g + 1]) for g in range(groups)], axis=1)

    bound = score_bound(kn_scr[0:1, :])
    bound_w = score_bound(kn_scr[1:2, :])

    def fold_bounded(s, j, causal):
        vT = v_tiles(vsT_ref, pl.multiple_of(j * kt, kt), kt)
        if causal:
            bias = jnp.where(krow(kt, j * kt) <= tq(kt), 0.0, NEG_INF)
            bias = jnp.concatenate([bias, bias], axis=1)
        for hp in range(nh // 2):
            sl = slice(hp * 2 * qb, (hp + 1) * 2 * qb)
            sh = s[:, sl] + bias if causal else s[:, sl]
            p = jnp.exp2(sh - bound[:, sl]).astype(BF16)
            acc_scr[:, sl] += jnp.dot(vT[2 * hp // hg], p, preferred_element_type=F32)

    def selected(fold_fn):
        acc_scr[...] = jnp.zeros(acc_scr.shape, F32)

        def main_pair(i, carry):
            k0 = pl.multiple_of(2 * i * kt, kt)
            ka = jnp.concatenate([ks_ref[0, pl.ds(k0, 2 * kt), :], et_ref[pl.ds(k0, 2 * kt), :]], axis=1)
            ss = jnp.dot(ka, qT_aug, preferred_element_type=F32)
            for t in range(2):
                fold_fn(ss[t * kt:(t + 1) * kt], 2 * i + t, False)
            return carry

        lax.fori_loop(0, c // 2, main_pair, 0)

        @pl.when(c % 2 == 1)
        def _():
            fold_fn(scores(c - 1), c - 1, False)

        fold_fn(scores(c), c, True)

    selected(fold_bounded)
    lw_min = functools.reduce(jnp.minimum, [window_tile(r, bound_w) for r in range(n_lt)])

    @pl.when(jnp.min(acc_scr[dh:dh + 1, :]) < MIN_DENOMINATOR)
    def _():
        m_scr[...] = jnp.full(m_scr.shape, NEG_INF, F32)
        selected(fold)

    @pl.when(jnp.min(lw_min) < MIN_DENOMINATOR)
    def _():
        for r in range(n_lt):
            window_tile(r, None)

    o_s = acc_scr[:dh, :] * (1.0 / acc_scr[dh:dh + 1, :])
    o_w = ow_scr[...]

    has_cmp = (tq(1) >= CMP_BLOCK - 1).astype(F32)
    gts = gT_scr[:3 * nh, :]
    outs = []
    for h in range(nh):
        sl = slice(h * qb, (h + 1) * qb)
        outs.append(gts[3 * h:3 * h + 1] * has_cmp * o_c[:, sl] + gts[3 * h + 1:3 * h + 2] * o_s[:, sl]
                    + gts[3 * h + 2:3 * h + 3] * o_w[:, sl])
    for j in range(nh // 2):
        pair = jnp.concatenate([outs[2 * j], outs[2 * j + 1]], axis=0)
        for r in range(qb // LANES):
            rows = slice(r * LANES, (r + 1) * LANES)
            o_ref[rows, j * LANES:(j + 1) * LANES] = pair[:, rows].T.astype(o_ref.dtype)


def _nsa(qk, gates, kc, vcT, vsT, vwT, c2sT, et, B, S, *, qb=256):
    G, Hg, dh = NSA_GROUPS, NSA_HG, NSA_DH
    T = B * S
    nq = S // qb
    n_slc = S // SLC_BLOCK
    kt = qb
    assert n_slc <= dh and 2 * dh == LANES and qb % LANES == 0 and S % qb == 0 and WINDOW % qb == 0
    assert S >= WINDOW + qb
    qw = G * Hg * dh
    assert 3 * G * Hg <= LANES and G * dh == LANES
    ncp = kc.shape[1]
    k_col = qw // LANES
    g_col = gates.shape[1] // MXU_N - 1
    seq_k = lambda col: pl.BlockSpec((1, S, LANES), lambda b, c: (b, 0, col))
    seq_vT = pl.BlockSpec((1, S // LANES, LANES, LANES), lambda b, c: (b, 0, 0, 0))
    qk3 = qk.reshape(B, S, qk.shape[1])
    cols = G * Hg * qb
    return pl.pallas_call(
        functools.partial(_nsa_body, kt=kt, top_n=min(SLC_TOPN, n_slc), n_slc=n_slc, hg=Hg, dh=dh),
        grid=(B, nq),
        in_specs=[pl.BlockSpec((qb, qw), lambda b, c: (b * nq + c, 0)),
                  pl.BlockSpec((qb, MXU_N), lambda b, c: (b * nq + c, g_col)),
                  pl.BlockSpec((1, ncp, LANES), lambda b, c: (b, 0, 0)),
                  pl.BlockSpec((1, LANES, ncp), lambda b, c: (b, 0, 0)),
                  seq_k(k_col), seq_vT, seq_k(k_col + 1), seq_vT,
                  _resident(c2sT.shape), _resident(et.shape)],
        out_specs=pl.BlockSpec((qb, qw), lambda b, c: (b * nq + c, 0)),
        out_shape=jax.ShapeDtypeStruct((T, qw), BF16),
        scratch_shapes=[pltpu.VMEM((2 * LANES, cols), BF16), pltpu.VMEM((LANES, qb), F32),
                        pltpu.VMEM((1, cols), F32), pltpu.VMEM((dh + BF16_ROWS, cols), F32),
                        pltpu.VMEM((G, dh, qb), jnp.int32), pltpu.VMEM((SUBLANES, LANES), F32),
                        pltpu.VMEM((dh, cols), F32)],
        compiler_params=pltpu.CompilerParams(dimension_semantics=("parallel", "arbitrary"),
                                             vmem_limit_bytes=VMEM_LIMIT),
        name="nsa",
    )(qk, gates, kc, vcT, qk3, vsT, qk3, vwT, c2sT, et)


def _nsa_constants(S):
    ncp = S // CMP_STRIDE
    n_slc = S // SLC_BLOCK
    cmp_start = np.arange(ncp) * CMP_STRIDE
    s_start = np.arange(n_slc) * SLC_BLOCK
    overlap = np.clip(np.minimum(cmp_start[:, None] + CMP_BLOCK, s_start[None, :] + SLC_BLOCK)
                      - np.maximum(cmp_start[:, None], s_start[None, :]), 0, None)
    c2sT = np.zeros((LANES, ncp), np.float32)
    c2sT[:n_slc, :] = overlap.T.astype(np.float32) / CMP_BLOCK
    c2sT[:, ncp - 1] = 0.0
    et = np.zeros((S, LANES), np.float32)
    et[np.arange(S), np.arange(S) // SLC_BLOCK] = 1.0
    return jnp.asarray(c2sT, BF16), jnp.asarray(et, BF16)


def _ret_body(q_ref, kT_ref, v_ref, decay_ref, zeta_ref, xi_ref, gch_ref, gn_ref, o_ref, *, C, dk):
    S = q_ref.shape[1]
    dv = gn_ref.shape[2]
    heads = LANES // dk
    N = S // C
    units = [(n, a) for n in range(N) for a in range(heads)]
    rows = lambda n: slice(n * C, (n + 1) * C)
    val = lambda n, a: v_ref[0, rows(n), a * dv:(a + 1) * dv]
    qp = lambda n: q_ref[0, rows(n), :]

    def own_rows(x, a):
        return jnp.concatenate([x if b == a else jnp.zeros_like(x) for b in range(heads)], axis=0)

    kT = {(n, a): kT_ref[0, 0, n, a * dk:(a + 1) * dk, :] for n, a in units}
    inner = {u: jnp.dot(qp(u[0]), own_rows(kT[u], u[1]), preferred_element_type=F32) for u in units}
    inner = {u: (inner[u] * decay_ref[u[1]]).astype(BF16) for u in units}
    o = {u: jnp.dot(inner[u], val(*u), preferred_element_type=F32) for u in units}
    kv = {(n, a): jnp.dot((kT[(n, a)].astype(F32) * zeta_ref[a]).astype(BF16), val(n, a),
                          preferred_element_type=F32) for n, a in units}
    state = {}
    for a in range(heads):
        R = jnp.zeros((dk, dv), F32)
        for n in range(N):
            state[(n, a)] = own_rows(R.astype(BF16), a)
            R = gch_ref[a] * R + kv[(n, a)]
    o = {u: o[u] + xi_ref[u[1]] * jnp.dot(qp(u[0]), state[u], preferred_element_type=F32) for u in units}
    mu = {u: jnp.mean(o[u], axis=-1, keepdims=True) for u in units}
    d = {u: o[u] - mu[u] for u in units}
    var = {u: jnp.mean(d[u] * d[u], axis=-1, keepdims=True) for u in units}
    for n, a in units:
        o_ref[0, rows(n), a * dv:(a + 1) * dv] = (d[(n, a)] * lax.rsqrt(var[(n, a)] + GN_EPS)
                                                  * gn_ref[a]).astype(o_ref.dtype)


def _retention(qk, kT, v, gn_gain, B, S, *, C=RET_CHUNK):
    H, dk, dv = RET_HEADS, RET_DK, RET_DV
    hp = LANES // dk
    assert C == LANES and dv == LANES
    log_g = np.log(1.0 - 2.0 ** (-5.0 - np.arange(H, dtype=np.float64)))
    i = np.arange(C, dtype=np.float64)
    diff = i[:, None] - i[None, :]
    const = lambda a: jnp.asarray(a, F32)
    decay = const(np.where(diff >= 0, np.exp(np.maximum(diff, 0.0) * log_g[:, None, None]), 0.0))
    zeta = const(np.exp((C - 1.0 - i)[None, :] * log_g[:, None]).reshape(H, 1, C))
    xi = const(np.broadcast_to(np.exp((i + 1.0)[None, :] * log_g[:, None])[:, :, None], (H, C, LANES)))
    gch = const(np.broadcast_to(np.exp(C * log_g)[:, None, None], (H, 1, dv)))
    per_pair = lambda shape: pl.BlockSpec((hp,) + shape, lambda b, p: (p, 0, 0))
    qk3 = qk.reshape(B, S, qk.shape[1])
    return pl.pallas_call(
        functools.partial(_ret_body, C=C, dk=dk),
        grid=(B, H // hp),
        in_specs=[pl.BlockSpec((1, S, LANES), lambda b, p: (b, 0, p)),
                  pl.BlockSpec((1, 1, S // LANES, LANES, LANES), lambda b, p: (b, p, 0, 0, 0)),
                  pl.BlockSpec((1, S, hp * dv), lambda b, p: (b, 0, p)),
                  per_pair((C, C)), per_pair((1, C)), per_pair((C, LANES)), per_pair((1, dv)),
                  per_pair((1, dv))],
        out_specs=pl.BlockSpec((1, S, hp * dv), lambda b, p: (b, 0, p)),
        out_shape=jax.ShapeDtypeStruct((B, S, H * dv), BF16),
        compiler_params=pltpu.CompilerParams(dimension_semantics=("parallel", "parallel"),
                                             vmem_limit_bytes=VMEM_LIMIT),
        name="retention",
    )(qk3, kT, v.reshape(B, S, H * dv), decay, zeta, xi, gch, gn_gain.reshape(H, 1, dv))


def _merge_body(x_ref, a_ref, r_ref, gr_ref, ga_ref, gb_ref, wa_ref, wr_ref, wo_ref, o_ref):
    g = gr_ref[...].astype(F32)
    r = (g * jax.nn.sigmoid(g) * r_ref[...].astype(F32)).astype(BF16)
    ya = jnp.dot(a_ref[...], wa_ref[...], preferred_element_type=F32)
    yr = jnp.dot(r, wr_ref[...], preferred_element_type=F32)
    mixed = jax.nn.sigmoid(ga_ref[...].astype(F32)) * ya + jax.nn.sigmoid(gb_ref[...].astype(F32)) * yr
    o_ref[...] = x_ref[...] + jnp.dot(mixed.astype(BF16), wo_ref[...], preferred_element_type=F32)


def _merge(x2d, a, r, gates, wa, wr, wo, *, tm=512):
    T, D = x2d.shape
    col = lambda j: pl.BlockSpec((tm, D), lambda i: (i, j))
    return pl.pallas_call(
        _merge_body,
        grid=(T // tm,),
        in_specs=[col(0), col(0), col(0), col(0), col(1), col(2),
                  _resident(wa.shape), _resident(wr.shape), _resident(wo.shape)],
        out_specs=col(0),
        out_shape=jax.ShapeDtypeStruct((T, D), F32),
        compiler_params=pltpu.CompilerParams(dimension_semantics=("parallel",),
                                             vmem_limit_bytes=VMEM_LIMIT),
        name="merge",
    )(x2d, a, r, gates, gates, gates, wa, wr, wo)


def _layer(x, p, final_norm):
    B, S, D = x.shape
    T = B * S
    G, Hg, dh = NSA_GROUPS, NSA_HG, NSA_DH
    H, dk, dv = RET_HEADS, RET_DK, RET_DV
    bf = lambda w: w.astype(BF16)

    x1 = _ffn(x.reshape(T, D), p["ffn1_norm"], bf(p["ffn1_w_gate"]), bf(p["ffn1_w_up"]),
              bf(p["ffn1_w_down"]), final_norm, final=False)

    pos = np.arange(S)
    tab_a = _rope_tables(pos, ROPE_DIM, dh, ROPE_THETA)
    tab_b = _rope_tables(pos, dk, dk, RET_ROPE_THETA)
    qk_n, qk_r, krT, v_ret, kc_tok, vc_tok, vsT, vwT, gates = _proj(
        x1, p["mix_norm"], bf(p["w_in"].T), tab_a, tab_b, B, S)

    chunk = lambda t: t.reshape(B, S // CMP_STRIDE, CMP_STRIDE * G * dh)
    kw1, kw2, pe = _compress_weights(p["cmp_k_w1"], p["cmp_k_w2"], p["cmp_pos_emb"])
    vw1, vw2, _ = _compress_weights(p["cmp_v_w1"], p["cmp_v_w2"], p["cmp_pos_emb"])
    cmp_end = np.arange(S // CMP_STRIDE) * CMP_STRIDE + (CMP_BLOCK - 1)
    kc, vcT = _compress(chunk(kc_tok), chunk(vc_tok), pe, kw1, kw2, vw1, vw2,
                        _rope_tables(cmp_end, ROPE_DIM, dh, ROPE_THETA))
    c2sT, et = _nsa_constants(S)
    a_out = _nsa(qk_n, gates, kc, vcT, vsT, vwT, c2sT, et, B, S)
    r_out = _retention(qk_r, krT, v_ret, p["ret_gn_gain"], B, S).reshape(T, H * dv)

    x2 = _merge(x1, a_out, r_out, gates, bf(p["w_branch_nsa"]), bf(p["w_branch_ret"]), bf(p["w_out"]))
    x3 = _ffn(x2, p["ffn2_norm"], bf(p["ffn2_w_gate"]), bf(p["ffn2_w_up"]), bf(p["ffn2_w_down"]),
              final_norm, final=True)
    return x3.reshape(B, S, D)


def kernel(x, ffn1_norm, ffn1_w_gate, ffn1_w_up, ffn1_w_down, mix_norm, w_in, cmp_pos_emb, cmp_k_w1,
           cmp_k_w2, cmp_v_w1, cmp_v_w2, ret_gn_gain, w_branch_nsa, w_branch_ret, w_out, ffn2_norm,
           ffn2_w_gate, ffn2_w_up, ffn2_w_down, final_norm):
    assert ffn1_norm.shape[0] == 1, "single-layer stack"
    names = ("ffn1_norm", "ffn1_w_gate", "ffn1_w_up", "ffn1_w_down", "mix_norm", "w_in", "cmp_pos_emb",
             "cmp_k_w1", "cmp_k_w2", "cmp_v_w1", "cmp_v_w2", "ret_gn_gain", "w_branch_nsa",
             "w_branch_ret", "w_out", "ffn2_norm", "ffn2_w_gate", "ffn2_w_up", "ffn2_w_down")
    vals = (ffn1_norm, ffn1_w_gate, ffn1_w_up, ffn1_w_down, mix_norm, w_in, cmp_pos_emb, cmp_k_w1,
            cmp_k_w2, cmp_v_w1, cmp_v_w2, ret_gn_gain, w_branch_nsa, w_branch_ret, w_out, ffn2_norm,
            ffn2_w_gate, ffn2_w_up, ffn2_w_down)
    p = {n: v[0] for n, v in zip(names, vals)}
    return _layer(x, p, final_norm)
```

```python
import functools
import math

import numpy as np
import jax
import jax.numpy as jnp
from jax import lax
from jax.experimental import pallas as pl
from jax.experimental.pallas import tpu as pltpu

F32 = jnp.float32
BF16 = jnp.bfloat16

NSA_HEADS = 16
NSA_GROUPS = 2
NSA_HG = NSA_HEADS // NSA_GROUPS
NSA_DH = 64
CMP_BLOCK = 32
CMP_STRIDE = 16
SLC_BLOCK = 64
SLC_TOPN = 16
WINDOW = 512
ROPE_THETA = 500000.0
ROPE_DIM = NSA_DH // 4
FORCED_SCORE = 1.0e4
RET_HEADS = 8
RET_DK = 64
RET_DV = 128
RET_CHUNK = 128
RET_ROPE_THETA = 10000.0
EPS = 1e-6
GN_EPS = 1e-5
NEG_INF = -1e30
LOG2E = math.log2(math.e)
NORM_SLACK = 1.0 + 2.0 ** -6
MIN_DENOMINATOR = 2.0 ** -60

LANES = 128
SUBLANES = 8
BF16_ROWS = 16
MXU_N = 256
VMEM_LIMIT = 56 * 1024 * 1024

NT_DIMS = (((1,), (1,)), ((), ()))


def _rms(x, g, eps=EPS):
    return x * lax.rsqrt(jnp.mean(x * x, axis=-1, keepdims=True) + eps) * g


def _resident(shape):
    nd = len(shape)
    return pl.BlockSpec(shape, lambda *_: (0,) * nd, pipeline_mode=pl.Buffered(1))


def _ffn_body(x_ref, g_ref, wg_ref, wu_ref, wd_ref, fg_ref, o_ref, *, cuts, final):
    x = x_ref[...]
    h = _rms(x, g_ref[...]).astype(BF16)
    acc = jnp.zeros(x.shape, F32)
    for lo, hi in zip(cuts[:-1], cuts[1:]):
        sl = slice(lo, hi)
        g = jnp.dot(h, wg_ref[:, sl], preferred_element_type=F32)
        u = jnp.dot(h, wu_ref[:, sl], preferred_element_type=F32)
        a = (g * jax.nn.sigmoid(g) * u).astype(BF16)
        acc = acc + jnp.dot(a, wd_ref[sl, :], preferred_element_type=F32)
    y = x + 0.5 * acc
    if final:
        y = _rms(y, fg_ref[...])
    o_ref[...] = y


def _ffn(x2d, gain, wg, wu, wd, final_gain, *, final, tm=512):
    T, D = x2d.shape
    F = wg.shape[1]
    assert F % MXU_N == 0
    n_tiles = F // MXU_N
    cuts = (0, (n_tiles + 1) // 2 * MXU_N, F)
    tok = pl.BlockSpec((tm, D), lambda i: (i, 0))
    return pl.pallas_call(
        functools.partial(_ffn_body, cuts=cuts, final=final),
        grid=(T // tm,),
        in_specs=[tok, _resident((1, D)), _resident((D, F)), _resident((D, F)), _resident((F, D)),
                  _resident((1, D))],
        out_specs=tok,
        out_shape=jax.ShapeDtypeStruct((T, D), F32),
        compiler_params=pltpu.CompilerParams(dimension_semantics=("parallel",),
                                             vmem_limit_bytes=VMEM_LIMIT),
        name="ffn_final" if final else "ffn",
    )(x2d, gain.reshape(1, D), wg, wu, wd, final_gain.reshape(1, D))


def _in_segments(D):
    widths = dict(q_a=NSA_HEADS * NSA_DH, kc=NSA_GROUPS * NSA_DH, vc=NSA_GROUPS * NSA_DH,
                  ks=NSA_GROUPS * NSA_DH, vs=NSA_GROUPS * NSA_DH, kw=NSA_GROUPS * NSA_DH,
                  vw=NSA_GROUPS * NSA_DH, g_a=3 * NSA_HEADS, q_r=RET_HEADS * RET_DK, k_r=RET_HEADS * RET_DK,
                  v_r=RET_HEADS * RET_DV, g_r=RET_HEADS * RET_DV, g_m=2 * D)
    offs, o = {}, 0
    for name, wd in widths.items():
        offs[name] = o
        o += wd
    offs["end"] = o
    return offs


def _rope_slab(y, tab_ref, shift):
    return (y * tab_ref[0] + pltpu.roll(y, LANES - shift, 1) * tab_ref[1]
            + pltpu.roll(y, shift, 1) * tab_ref[2])


def _proj_body(x_ref, g_ref, w_ref, ta_ref, tb_ref,
               oa_ref, ob_ref, okrT_ref, ov_ref, okc_ref, ovc_ref, ovsT_ref, ovwT_ref, od_ref):
    x = x_ref[...]
    h = _rms(x, g_ref[...]).astype(BF16)
    tm = h.shape[0]
    seg = _in_segments(x.shape[1])
    per_chunk = MXU_N // LANES

    def put(o_ref, i):
        def route(ys):
            o_ref[:, i * LANES:(i + 1) * LANES] = ys.astype(o_ref.dtype)
        return route

    def put_t(o_ref, *lead):
        def route(ys):
            for r in range(tm // LANES):
                o_ref[(0,) + lead + (r,)] = ys[r * LANES:(r + 1) * LANES, :].T.astype(o_ref.dtype)
        return route

    def put_chunked(o_ref):
        def route(ys):
            o_ref[...] = ys.reshape(tm // CMP_STRIDE, CMP_STRIDE * LANES).astype(o_ref.dtype)
        return route

    def roped(route, tab_ref, shift, scale=None):
        def wrapped(ys):
            route(_rope_slab(ys if scale is None else ys * scale, tab_ref, shift))
        return wrapped

    nsa_rope = functools.partial(roped, tab_ref=ta_ref, shift=ROPE_DIM // 2)
    ret_rope = functools.partial(roped, tab_ref=tb_ref, shift=RET_DK // 2)
    n_gr = (seg["g_m"] - seg["g_r"]) // LANES
    n_gm = (seg["end"] - seg["g_m"]) // LANES
    nq = (seg["kc"] - seg["q_a"]) // LANES
    plan = []
    for i in range(0, nq, per_chunk):
        plan.append((seg["q_a"] + i * LANES,
                     [nsa_rope(put(oa_ref, i + s), scale=NSA_DH ** -0.5 * LOG2E) for s in range(per_chunk)]))
    plan.append((seg["kc"], [put_chunked(okc_ref), put_chunked(ovc_ref)]))
    plan.append((seg["ks"], [nsa_rope(put(oa_ref, nq)), put_t(ovsT_ref)]))
    plan.append((seg["kw"], [nsa_rope(put(oa_ref, nq + 1)), put_t(ovwT_ref)]))
    plan.append((seg["g_a"], [put(od_ref, n_gr + n_gm + s) for s in range(per_chunk)]))
    for name, count in (("q_r", ob_ref.shape[1] // LANES), ("k_r", okrT_ref.shape[1]),
                        ("v_r", ov_ref.shape[1] // LANES), ("g_r", n_gr), ("g_m", n_gm)):
        for i in range(0, count, per_chunk):
            routes = []
            for s in range(per_chunk):
                if name == "q_r":
                    routes.append(ret_rope(put(ob_ref, i + s), scale=RET_DK ** -0.5))
                elif name == "k_r":
                    routes.append(ret_rope(put_t(okrT_ref, i + s)))
                elif name == "v_r":
                    routes.append(put(ov_ref, i + s))
                else:
                    routes.append(put(od_ref, i + s + (n_gr if name == "g_m" else 0)))
            plan.append((seg[name] + i * LANES, routes))

    for row0, routes in plan:
        y = lax.dot_general(h, w_ref[row0:row0 + MXU_N, :], NT_DIMS, preferred_element_type=F32)
        for s, route in enumerate(routes):
            route(y[:, s * LANES:(s + 1) * LANES])


def _proj(x2d, gain, wT, tab_a, tab_b, B, S, *, tm=512):
    T, D = x2d.shape
    spt = S // tm
    seg = _in_segments(D)
    assert wT.shape == (seg["end"], D) and NSA_GROUPS * NSA_DH == LANES and seg["g_a"] + MXU_N <= seg["end"]
    assert all(o % BF16_ROWS == 0 for o in seg.values())
    na = seg["kc"] - seg["q_a"] + 2 * LANES
    nbw = seg["k_r"] - seg["q_r"]
    nkr = (seg["v_r"] - seg["k_r"]) // LANES
    nv = seg["g_r"] - seg["v_r"]
    nd = seg["end"] - seg["g_r"] + MXU_N
    tok = lambda n: pl.BlockSpec((tm, n), lambda i: (i, 0))
    tab = pl.BlockSpec((3, tm, LANES), lambda i: (0, i % spt, 0))
    tr = pl.BlockSpec((1, tm // LANES, LANES, LANES), lambda i: (i // spt, i % spt, 0, 0))
    chunked = pl.BlockSpec((tm // CMP_STRIDE, CMP_STRIDE * LANES), lambda i: (i, 0))
    tr_kr = pl.BlockSpec((1, nkr, tm // LANES, LANES, LANES), lambda i: (i // spt, 0, i % spt, 0, 0))
    return pl.pallas_call(
        _proj_body,
        grid=(T // tm,),
        in_specs=[tok(D), _resident((1, D)), _resident(wT.shape), tab, tab],
        out_specs=[tok(na), tok(nbw), tr_kr, tok(nv), chunked, chunked,
                   tr, tr, tok(nd)],
        out_shape=[jax.ShapeDtypeStruct((T, na), BF16),
                   jax.ShapeDtypeStruct((T, nbw), BF16),
                   jax.ShapeDtypeStruct((B, nkr, S // LANES, LANES, LANES), BF16),
                   jax.ShapeDtypeStruct((T, nv), BF16),
                   jax.ShapeDtypeStruct((T // CMP_STRIDE, CMP_STRIDE * LANES), BF16),
                   jax.ShapeDtypeStruct((T // CMP_STRIDE, CMP_STRIDE * LANES), BF16),
                   jax.ShapeDtypeStruct((B, S // LANES, LANES, LANES), BF16),
                   jax.ShapeDtypeStruct((B, S // LANES, LANES, LANES), BF16),
                   jax.ShapeDtypeStruct((T, nd), BF16)],
        compiler_params=pltpu.CompilerParams(dimension_semantics=("parallel",),
                                             vmem_limit_bytes=VMEM_LIMIT),
        name="proj",
    )(x2d, gain.reshape(1, D), wT, tab_a, tab_b)


def _rope_tables(pos, rot_dim, head_dim, theta):
    half = rot_dim // 2
    freqs = theta ** (-(np.arange(half, dtype=np.float64) * 2.0 / rot_dim))
    ang = np.asarray(pos, np.float64)[:, None] * freqs[None, :]
    cos, sin = np.cos(ang), np.sin(ang)
    n = ang.shape[0]
    rest = head_dim - rot_dim
    cos_h = np.concatenate([cos, cos, np.ones((n, rest))], axis=-1)
    sl_h = np.concatenate([-sin, np.zeros((n, half + rest))], axis=-1)
    sr_h = np.concatenate([np.zeros((n, half)), sin, np.zeros((n, rest))], axis=-1)
    rep = LANES // head_dim
    return jnp.asarray(np.stack([np.tile(t, (1, rep)) for t in (cos_h, sl_h, sr_h)]), F32)


def _gelu_tanh(x):
    return 0.5 * x * (1.0 + jnp.tanh(math.sqrt(2.0 / math.pi) * (x + 0.044715 * (x * x * x))))


def _compress_body(kt_ref, vt_ref, pe_ref, kw1_ref, kw2_ref, vw1_ref, vw2_ref, tab_ref,
                   kc_ref, vcT_ref):
    pe = pe_ref[...]
    n = kt_ref.shape[1]

    def mlp(tok_ref, w1_ref, w2_ref):
        c = tok_ref[0].astype(F32)
        lo = (c + pe[0:1]).astype(BF16)
        hi = (c + pe[1:2]).astype(BF16)
        out = jnp.zeros((n, LANES), F32)
        for g in range(NSA_GROUPS):
            top = jnp.dot(lo, w1_ref[g, 0], preferred_element_type=F32)
            bot = jnp.dot(hi, w1_ref[g, 1], preferred_element_type=F32)
            hid = _gelu_tanh(top + pltpu.roll(bot, n - 1, 0))
            out = out + jnp.dot(hid.astype(BF16), w2_ref[g], preferred_element_type=F32)
        return out

    kc_ref[0] = _rope_slab(mlp(kt_ref, kw1_ref, kw2_ref), tab_ref, ROPE_DIM // 2).astype(kc_ref.dtype)
    vc = mlp(vt_ref, vw1_ref, vw2_ref)
    for r in range(n // LANES):
        vcT_ref[0, :, r * LANES:(r + 1) * LANES] = vc[r * LANES:(r + 1) * LANES, :].T.astype(vcT_ref.dtype)


def _compress(k_tok, v_tok, pe, kw1, kw2, vw1, vw2, tab):
    B, n, w = k_tok.shape
    tok = pl.BlockSpec((1, n, w), lambda b: (b, 0, 0))
    return pl.pallas_call(
        _compress_body,
        grid=(B,),
        in_specs=[tok, tok, _resident(pe.shape), _resident(kw1.shape), _resident(kw2.shape),
                  _resident(vw1.shape), _resident(vw2.shape), _resident(tab.shape)],
        out_specs=[pl.BlockSpec((1, n, LANES), lambda b: (b, 0, 0)),
                   pl.BlockSpec((1, LANES, n), lambda b: (b, 0, 0))],
        out_shape=[jax.ShapeDtypeStruct((B, n, LANES), BF16), jax.ShapeDtypeStruct((B, LANES, n), BF16)],
        compiler_params=pltpu.CompilerParams(dimension_semantics=("parallel",),
                                             vmem_limit_bytes=VMEM_LIMIT),
        name="compress",
    )(k_tok, v_tok, pe, kw1, kw2, vw1, vw2, tab)


def _compress_weights(w1, w2, pe):
    G, dh = NSA_GROUPS, NSA_DH
    hid = w1.shape[1]
    halves = w1.reshape(2, CMP_STRIDE, dh, hid)
    w1e = jnp.zeros((G, 2, CMP_STRIDE, G, dh, hid), F32)
    w2e = jnp.zeros((G, hid, G, dh), F32)
    for g in range(G):
        w1e = w1e.at[g, :, :, g].set(halves)
        w2e = w2e.at[g, :, g].set(w2)
    pe_e = jnp.broadcast_to(pe.reshape(2, CMP_STRIDE, 1, dh), (2, CMP_STRIDE, G, dh))
    return (w1e.reshape(G, 2, CMP_STRIDE * G * dh, hid).astype(BF16), w2e.reshape(G, hid, G * dh).astype(BF16),
            pe_e.reshape(2, CMP_STRIDE * G * dh))


def _nsa_body(q_ref, gate_ref, kc_ref, vcT_ref, ks_ref, vsT_ref, kw_ref, vwT_ref, c2sT_ref, et_ref,
              o_ref, qT_scr, gT_scr, m_scr, acc_scr, rank_scr, kn_scr, ow_scr, oc_scr, imp_scr,
              *, kt, top_n, n_slc, hg, dh):
    qb = q_ref.shape[0]
    nh = q_ref.shape[1] // dh
    groups = nh // hg
    ncp = kc_ref.shape[1]
    c = pl.program_id(1)
    q0 = c * qb
    ones_rows = jnp.ones((BF16_ROWS, 1), BF16)
    grp_rows = lambda g: slice(g * dh, (g + 1) * dh)

    def v_tiles(vT_ref, k0, n):
        t = vT_ref[0, pl.ds(k0 // LANES, n // LANES)]
        vT = jnp.concatenate([t[i] for i in range(n // LANES)], axis=1)
        ones = jnp.broadcast_to(ones_rows, (BF16_ROWS, n))
        return [jnp.concatenate([vT[grp_rows(g)], ones], axis=0) for g in range(groups)]

    def tq(n):
        return q0 + lax.broadcasted_iota(jnp.int32, (n, qb), 1)

    def krow(n, k0=0):
        return k0 + lax.broadcasted_iota(jnp.int32, (n, qb), 0)

    @pl.when(c == 0)
    def _():
        lane_grp = lax.broadcasted_iota(jnp.int32, (LANES, LANES), 0) // dh
        col = lax.broadcasted_iota(jnp.int32, (LANES, LANES), 1)
        group_sum = jnp.where(lane_grp == col, 1.0, 0.0).astype(BF16)
        for row, k_ref in enumerate((ks_ref, kw_ref, kc_ref)):
            k = k_ref[0].astype(F32)
            n2 = jnp.dot((k * k).astype(BF16), group_sum, preferred_element_type=F32)
            kn_scr[row:row + 1, :] = jnp.max(n2, axis=0, keepdims=True)

    qT_scr[...] = jnp.zeros(qT_scr.shape, BF16)
    for r in range(qb // LANES):
        rows = slice(r * LANES, (r + 1) * LANES)
        for j in range(nh // 2):
            t = q_ref[rows, j * LANES:(j + 1) * LANES].astype(F32).T.astype(BF16)
            for half in range(2):
                h = 2 * j + half
                qT_scr[grp_rows(h // hg), h * qb + r * LANES:h * qb + (r + 1) * LANES] = (
                    t[half * dh:(half + 1) * dh])
        gT_scr[:, rows] = jax.nn.sigmoid(gate_ref[rows, :LANES].astype(F32).T)

    qT = qT_scr[:LANES, :]

    span = WINDOW + LANES

    def window_tile(r, stabiliser):
        t0 = q0 + r * LANES
        w0 = pl.multiple_of(jnp.maximum(t0 - WINDOW, 0), LANES)
        dist = (t0 + lax.broadcasted_iota(jnp.int32, (span, LANES), 1)
                - (w0 + lax.broadcasted_iota(jnp.int32, (span, LANES), 0)))
        bias_w = jnp.where((dist >= 0) & (dist < WINDOW), 0.0, NEG_INF)
        bias_w = jnp.concatenate([bias_w, bias_w], axis=1)
        qT_r = jnp.concatenate([qT[:, h * qb + r * LANES:h * qb + (r + 1) * LANES] for h in range(nh)], axis=1)
        s_w = jnp.dot(kw_ref[0, pl.ds(w0, span), :], qT_r, preferred_element_type=F32)
        vwT = v_tiles(vwT_ref, w0, span)
        l_min = jnp.full((1, 2 * LANES), 1.0, F32)
        for hp in range(nh // 2):
            heads = (2 * hp, 2 * hp + 1)
            cols = [slice(h * qb + r * LANES, h * qb + (r + 1) * LANES) for h in heads]
            sh = s_w[:, hp * 2 * LANES:(hp + 1) * 2 * LANES] + bias_w
            if stabiliser is None:
                stab = jnp.max(sh, axis=0, keepdims=True)
            else:
                stab = jnp.concatenate([stabiliser[:, cs] for cs in cols], axis=1)
            acc_w = jnp.dot(vwT[2 * hp // hg], jnp.exp2(sh - stab).astype(BF16), preferred_element_type=F32)
            l_w = acc_w[dh:dh + 1]
            l_min = jnp.minimum(l_min, l_w)
            ow = acc_w[:dh] * (1.0 / l_w)
            for i, cs in enumerate(cols):
                ow_scr[:, cs] = ow[:, i * LANES:(i + 1) * LANES]
        return l_min

    n_lt = qb // LANES

    qn2 = jnp.dot(jnp.broadcast_to(ones_rows, (BF16_ROWS, LANES)), qT * qT, preferred_element_type=F32)[0:1]

    def score_bound(kn2):
        return NORM_SLACK * jnp.concatenate(
            [jnp.sqrt(qn2[:, g * hg * qb:(g + 1) * hg * qb] * kn2[:, g:g + 1]) for g in range(groups)], axis=1)

    s_c = jnp.dot(kc_ref[0], qT, preferred_element_type=F32)
    cend = krow(ncp) * CMP_STRIDE + (CMP_BLOCK - 1)
    bias_c = jnp.where(cend <= tq(ncp), 0.0, NEG_INF)
    bias_c = jnp.concatenate([bias_c, bias_c], axis=1)
    nb = dh
    vcT = vcT_ref[0]
    lhs_c = [jnp.concatenate([vcT[grp_rows(g)], jnp.broadcast_to(ones_rows, (BF16_ROWS, ncp)),
                              c2sT_ref[:nb, :]], axis=0) for g in range(groups)]

    def compressed(stabiliser):
        imps = [jnp.zeros((nb, qb), F32) for _ in range(groups)]
        l_min = jnp.full((1, 2 * qb), 1.0, F32)
        for hp in range(nh // 2):
            g = 2 * hp // hg
            sl = slice(hp * 2 * qb, (hp + 1) * 2 * qb)
            sh = s_c[:, sl] + bias_c
            stab = jnp.max(sh, axis=0, keepdims=True) if stabiliser is None else stabiliser[:, sl]
            r = jnp.dot(lhs_c[g], jnp.exp2(sh - stab).astype(BF16), preferred_element_type=F32)
            l_min = jnp.minimum(l_min, r[dh:dh + 1])
            inv_l = 1.0 / r[dh:dh + 1]
            oc_scr[:, sl] = r[:dh] * inv_l
            w = r[dh + BF16_ROWS:] * inv_l
            imps[g] = imps[g] + w[:, :qb] + w[:, qb:]
        for g in range(groups):
            imp_scr[g] = imps[g]
        return l_min

    lc_min = compressed(score_bound(kn_scr[2:3, :]))

    @pl.when(jnp.min(lc_min) < MIN_DENOMINATOR)
    def _():
        compressed(None)

    o_c = oc_scr[...]
    imps = [imp_scr[g] for g in range(groups)]
    sb = krow(nb)
    cur = jnp.right_shift(tq(nb), int(math.log2(SLC_BLOCK)))
    forced = (sb == 0) | (sb == cur) | (sb == cur - 1)
    future = sb > cur
    imps = [jnp.where(forced, FORCED_SCORE, jnp.where(future, -FORCED_SCORE, imp)) for imp in imps]
    sub = SUBLANES
    rank_scr[...] = jnp.zeros(rank_scr.shape, jnp.int32)
    last_block = (q0 + qb - 1) // SLC_BLOCK
    sb_l = sb[:, :LANES]
    for grp in range(n_slc // sub):
        @pl.when((grp * sub <= last_block) & (last_block >= top_n))
        def _():
            for g in range(groups):
                for lc in range(qb // LANES):
                    imp_l = imps[g][:, lc * LANES:(lc + 1) * LANES]
                    chunks = [imp_l[r * sub:(r + 1) * sub] for r in range(nb // sub)]
                    ranks = [jnp.zeros((sub, LANES), jnp.int32) for _ in chunks]
                    for sp in range(grp * sub, (grp + 1) * sub):
                        row = imp_l[sp:sp + 1, :]
                        for r, blk in enumerate(chunks):
                            if r * sub > sp:
                                beats = row >= blk
                            elif (r + 1) * sub - 1 <= sp:
                                beats = row > blk
                            else:
                                beats = (row > blk) | ((row == blk) & (sb_l[r * sub:(r + 1) * sub] > sp))
                            ranks[r] = ranks[r] + jnp.where(beats, 1, 0)
                    rank_scr[g, :, lc * LANES:(lc + 1) * LANES] += jnp.concatenate(ranks, axis=0)
    for g in range(groups):
        selneg = jnp.where((rank_scr[g] < top_n) & jnp.logical_not(future), 0.0, NEG_INF).astype(BF16)
        for h in range(g * hg, (g + 1) * hg):
            qT_scr[LANES:LANES + nb, h * qb:(h + 1) * qb] = selneg

    qT_aug = qT_scr[...]

    def scores(j):
        k0 = pl.multiple_of(j * kt, kt)
        ka = jnp.concatenate([ks_ref[0, pl.ds(k0, kt), :], et_ref[pl.ds(k0, kt), :]], axis=1)
        return jnp.dot(ka, qT_aug, preferred_element_type=F32)

    def fold(s, j, causal):
        vT = v_tiles(vsT_ref, pl.multiple_of(j * kt, kt), kt)
        if causal:
            bias = jnp.where(krow(kt, j * kt) <= tq(kt), 0.0, NEG_INF)
            bias = jnp.concatenate([bias, bias], axis=1)
        for hp in range(nh // 2):
            sl = slice(hp * 2 * qb, (hp + 1) * 2 * qb)
            sh = s[:, sl] + bias if causal else s[:, sl]
            m_old = m_scr[:, sl]
            m_new = jnp.maximum(m_old, jnp.max(sh, axis=0, keepdims=True))
            pv = jnp.dot(vT[2 * hp // hg], jnp.exp2(sh - m_new).astype(BF16), preferred_element_type=F32)
            acc_scr[:, sl] = jnp.exp2(m_old - m_new) * acc_scr[:, sl] + pv
            m_scr[:, sl] = m_new

    bound = score_bound(kn_scr[0:1, :])
    bound_w = score_bound(kn_scr[1:2, :])

    def fold_bounded(s, j, causal):
        vT = v_tiles(vsT_ref, pl.multiple_of(j * kt, kt), kt)
        if causal:
            bias = jnp.where(krow(kt, j * kt) <= tq(kt), 0.0, NEG_INF)
            bias = jnp.concatenate([bias, bias], axis=1)
        for hp in range(nh // 2):
            sl = slice(hp * 2 * qb, (hp + 1) * 2 * qb)
            sh = s[:, sl] + bias if causal else s[:, sl]
            p = jnp.exp2(sh - bound[:, sl]).astype(BF16)
            acc_scr[:, sl] += jnp.dot(vT[2 * hp // hg], p, preferred_element_type=F32)

    def selected(fold_fn):
        acc_scr[...] = jnp.zeros(acc_scr.shape, F32)

        def main_pair(i, carry):
            k0 = pl.multiple_of(2 * i * kt, kt)
            ka = jnp.concatenate([ks_ref[0, pl.ds(k0, 2 * kt), :], et_ref[pl.ds(k0, 2 * kt), :]], axis=1)
            ss = jnp.dot(ka, qT_aug, preferred_element_type=F32)
            for t in range(2):
                fold_fn(ss[t * kt:(t + 1) * kt], 2 * i + t, False)
            return carry

        lax.fori_loop(0, c // 2, main_pair, 0)

        @pl.when(c % 2 == 1)
        def _():
            fold_fn(scores(c - 1), c - 1, False)

        fold_fn(scores(c), c, True)

    selected(fold_bounded)
    lw_min = functools.reduce(jnp.minimum, [window_tile(r, bound_w) for r in range(n_lt)])

    @pl.when(jnp.min(acc_scr[dh:dh + 1, :]) < MIN_DENOMINATOR)
    def _():
        m_scr[...] = jnp.full(m_scr.shape, NEG_INF, F32)
        selected(fold)

    @pl.when(jnp.min(lw_min) < MIN_DENOMINATOR)
    def _():
        for r in range(n_lt):
            window_tile(r, None)

    o_s = acc_scr[:dh, :] * (1.0 / acc_scr[dh:dh + 1, :])
    o_w = ow_scr[...]

    has_cmp = (tq(1) >= CMP_BLOCK - 1).astype(F32)
    gts = gT_scr[:3 * nh, :]
    outs = []
    for h in range(nh):
        sl = slice(h * qb, (h + 1) * qb)
        outs.append(gts[3 * h:3 * h + 1] * has_cmp * o_c[:, sl] + gts[3 * h + 1:3 * h + 2] * o_s[:, sl]
                    + gts[3 * h + 2:3 * h + 3] * o_w[:, sl])
    for j in range(nh // 2):
        pair = jnp.concatenate([outs[2 * j], outs[2 * j + 1]], axis=0)
        for r in range(qb // LANES):
            rows = slice(r * LANES, (r + 1) * LANES)
            o_ref[rows, j * LANES:(j + 1) * LANES] = pair[:, rows].T.astype(o_ref.dtype)


def _nsa(qk, gates, kc, vcT, vsT, vwT, c2sT, et, B, S, *, qb=256):
    G, Hg, dh = NSA_GROUPS, NSA_HG, NSA_DH
    T = B * S
    nq = S // qb
    n_slc = S // SLC_BLOCK
    kt = qb
    assert n_slc <= dh and 2 * dh == LANES and qb % LANES == 0 and S % qb == 0 and WINDOW % qb == 0
    assert S >= WINDOW + qb
    qw = G * Hg * dh
    assert 3 * G * Hg <= LANES and G * dh == LANES
    ncp = kc.shape[1]
    k_col = qw // LANES
    g_col = gates.shape[1] // MXU_N - 1
    seq_k = lambda col: pl.BlockSpec((1, S, LANES), lambda b, c: (b, 0, col))
    seq_vT = pl.BlockSpec((1, S // LANES, LANES, LANES), lambda b, c: (b, 0, 0, 0))
    qk3 = qk.reshape(B, S, qk.shape[1])
    cols = G * Hg * qb
    return pl.pallas_call(
        functools.partial(_nsa_body, kt=kt, top_n=min(SLC_TOPN, n_slc), n_slc=n_slc, hg=Hg, dh=dh),
        grid=(B, nq),
        in_specs=[pl.BlockSpec((qb, qw), lambda b, c: (b * nq + c, 0)),
                  pl.BlockSpec((qb, MXU_N), lambda b, c: (b * nq + c, g_col)),
                  pl.BlockSpec((1, ncp, LANES), lambda b, c: (b, 0, 0)),
                  pl.BlockSpec((1, LANES, ncp), lambda b, c: (b, 0, 0)),
                  seq_k(k_col), seq_vT, seq_k(k_col + 1), seq_vT,
                  _resident(c2sT.shape), _resident(et.shape)],
        out_specs=pl.BlockSpec((qb, qw), lambda b, c: (b * nq + c, 0)),
        out_shape=jax.ShapeDtypeStruct((T, qw), BF16),
        scratch_shapes=[pltpu.VMEM((2 * LANES, cols), BF16), pltpu.VMEM((LANES, qb), F32),
                        pltpu.VMEM((1, cols), F32), pltpu.VMEM((dh + BF16_ROWS, cols), F32),
                        pltpu.VMEM((G, dh, qb), jnp.int32), pltpu.VMEM((SUBLANES, LANES), F32),
                        pltpu.VMEM((dh, cols), F32), pltpu.VMEM((dh, cols), F32), pltpu.VMEM((G, dh, qb), F32)],
        compiler_params=pltpu.CompilerParams(dimension_semantics=("parallel", "arbitrary"),
                                             vmem_limit_bytes=VMEM_LIMIT),
        name="nsa",
    )(qk, gates, kc, vcT, qk3, vsT, qk3, vwT, c2sT, et)


def _nsa_constants(S):
    ncp = S // CMP_STRIDE
    n_slc = S // SLC_BLOCK
    cmp_start = np.arange(ncp) * CMP_STRIDE
    s_start = np.arange(n_slc) * SLC_BLOCK
    overlap = np.clip(np.minimum(cmp_start[:, None] + CMP_BLOCK, s_start[None, :] + SLC_BLOCK)
                      - np.maximum(cmp_start[:, None], s_start[None, :]), 0, None)
    c2sT = np.zeros((LANES, ncp), np.float32)
    c2sT[:n_slc, :] = overlap.T.astype(np.float32) / CMP_BLOCK
    c2sT[:, ncp - 1] = 0.0
    et = np.zeros((S, LANES), np.float32)
    et[np.arange(S), np.arange(S) // SLC_BLOCK] = 1.0
    return jnp.asarray(c2sT, BF16), jnp.asarray(et, BF16)


def _ret_body(q_ref, kT_ref, v_ref, decay_ref, zeta_ref, xi_ref, gch_ref, gn_ref, o_ref, *, C, dk):
    S = q_ref.shape[1]
    dv = gn_ref.shape[2]
    heads = LANES // dk
    N = S // C
    units = [(n, a) for n in range(N) for a in range(heads)]
    rows = lambda n: slice(n * C, (n + 1) * C)
    val = lambda n, a: v_ref[0, rows(n), a * dv:(a + 1) * dv]
    qp = lambda n: q_ref[0, rows(n), :]

    def own_rows(x, a):
        return jnp.concatenate([x if b == a else jnp.zeros_like(x) for b in range(heads)], axis=0)

    kT = {(n, a): kT_ref[0, 0, n, a * dk:(a + 1) * dk, :] for n, a in units}
    inner = {u: jnp.dot(qp(u[0]), own_rows(kT[u], u[1]), preferred_element_type=F32) for u in units}
    inner = {u: (inner[u] * decay_ref[u[1]]).astype(BF16) for u in units}
    o = {u: jnp.dot(inner[u], val(*u), preferred_element_type=F32) for u in units}
    kv = {(n, a): jnp.dot((kT[(n, a)].astype(F32) * zeta_ref[a]).astype(BF16), val(n, a),
                          preferred_element_type=F32) for n, a in units}
    state = {}
    for a in range(heads):
        R = jnp.zeros((dk, dv), F32)
        for n in range(N):
            state[(n, a)] = own_rows(R.astype(BF16), a)
            R = gch_ref[a] * R + kv[(n, a)]
    o = {u: o[u] + xi_ref[u[1]] * jnp.dot(qp(u[0]), state[u], preferred_element_type=F32) for u in units}
    mu = {u: jnp.mean(o[u], axis=-1, keepdims=True) for u in units}
    d = {u: o[u] - mu[u] for u in units}
    var = {u: jnp.mean(d[u] * d[u], axis=-1, keepdims=True) for u in units}
    for n, a in units:
        o_ref[0, rows(n), a * dv:(a + 1) * dv] = (d[(n, a)] * lax.rsqrt(var[(n, a)] + GN_EPS)
                                                  * gn_ref[a]).astype(o_ref.dtype)


def _retention(qk, kT, v, gn_gain, B, S, *, C=RET_CHUNK):
    H, dk, dv = RET_HEADS, RET_DK, RET_DV
    hp = LANES // dk
    assert C == LANES and dv == LANES
    log_g = np.log(1.0 - 2.0 ** (-5.0 - np.arange(H, dtype=np.float64)))
    i = np.arange(C, dtype=np.float64)
    diff = i[:, None] - i[None, :]
    const = lambda a: jnp.asarray(a, F32)
    decay = const(np.where(diff >= 0, np.exp(np.maximum(diff, 0.0) * log_g[:, None, None]), 0.0))
    zeta = const(np.exp((C - 1.0 - i)[None, :] * log_g[:, None]).reshape(H, 1, C))
    xi = const(np.broadcast_to(np.exp((i + 1.0)[None, :] * log_g[:, None])[:, :, None], (H, C, LANES)))
    gch = const(np.broadcast_to(np.exp(C * log_g)[:, None, None], (H, 1, dv)))
    per_pair = lambda shape: pl.BlockSpec((hp,) + shape, lambda b, p: (p, 0, 0))
    qk3 = qk.reshape(B, S, qk.shape[1])
    return pl.pallas_call(
        functools.partial(_ret_body, C=C, dk=dk),
        grid=(B, H // hp),
        in_specs=[pl.BlockSpec((1, S, LANES), lambda b, p: (b, 0, p)),
                  pl.BlockSpec((1, 1, S // LANES, LANES, LANES), lambda b, p: (b, p, 0, 0, 0)),
                  pl.BlockSpec((1, S, hp * dv), lambda b, p: (b, 0, p)),
                  per_pair((C, C)), per_pair((1, C)), per_pair((C, LANES)), per_pair((1, dv)),
                  per_pair((1, dv))],
        out_specs=pl.BlockSpec((1, S, hp * dv), lambda b, p: (b, 0, p)),
        out_shape=jax.ShapeDtypeStruct((B, S, H * dv), BF16),
        compiler_params=pltpu.CompilerParams(dimension_semantics=("parallel", "parallel"),
                                             vmem_limit_bytes=VMEM_LIMIT),
        name="retention",
    )(qk3, kT, v.reshape(B, S, H * dv), decay, zeta, xi, gch, gn_gain.reshape(H, 1, dv))


def _merge_body(x_ref, a_ref, r_ref, gr_ref, ga_ref, gb_ref, wa_ref, wr_ref, wo_ref, o_ref):
    g = gr_ref[...].astype(F32)
    r = (g * jax.nn.sigmoid(g) * r_ref[...].astype(F32)).astype(BF16)
    ya = jnp.dot(a_ref[...], wa_ref[...], preferred_element_type=F32)
    yr = jnp.dot(r, wr_ref[...], preferred_element_type=F32)
    mixed = jax.nn.sigmoid(ga_ref[...].astype(F32)) * ya + jax.nn.sigmoid(gb_ref[...].astype(F32)) * yr
    o_ref[...] = x_ref[...] + jnp.dot(mixed.astype(BF16), wo_ref[...], preferred_element_type=F32)


def _merge(x2d, a, r, gates, wa, wr, wo, *, tm=512):
    T, D = x2d.shape
    col = lambda j: pl.BlockSpec((tm, D), lambda i: (i, j))
    return pl.pallas_call(
        _merge_body,
        grid=(T // tm,),
        in_specs=[col(0), col(0), col(0), col(0), col(1), col(2),
                  _resident(wa.shape), _resident(wr.shape), _resident(wo.shape)],
        out_specs=col(0),
        out_shape=jax.ShapeDtypeStruct((T, D), F32),
        compiler_params=pltpu.CompilerParams(dimension_semantics=("parallel",),
                                             vmem_limit_bytes=VMEM_LIMIT),
        name="merge",
    )(x2d, a, r, gates, gates, gates, wa, wr, wo)


def _layer(x, p, final_norm):
    B, S, D = x.shape
    T = B * S
    G, Hg, dh = NSA_GROUPS, NSA_HG, NSA_DH
    H, dk, dv = RET_HEADS, RET_DK, RET_DV
    bf = lambda w: w.astype(BF16)

    x1 = _ffn(x.reshape(T, D), p["ffn1_norm"], bf(p["ffn1_w_gate"]), bf(p["ffn1_w_up"]),
              bf(p["ffn1_w_down"]), final_norm, final=False)

    pos = np.arange(S)
    tab_a = _rope_tables(pos, ROPE_DIM, dh, ROPE_THETA)
    tab_b = _rope_tables(pos, dk, dk, RET_ROPE_THETA)
    qk_n, qk_r, krT, v_ret, kc_tok, vc_tok, vsT, vwT, gates = _proj(
        x1, p["mix_norm"], bf(p["w_in"].T), tab_a, tab_b, B, S)

    chunk = lambda t: t.reshape(B, S // CMP_STRIDE, CMP_STRIDE * G * dh)
    kw1, kw2, pe = _compress_weights(p["cmp_k_w1"], p["cmp_k_w2"], p["cmp_pos_emb"])
    vw1, vw2, _ = _compress_weights(p["cmp_v_w1"], p["cmp_v_w2"], p["cmp_pos_emb"])
    cmp_end = np.arange(S // CMP_STRIDE) * CMP_STRIDE + (CMP_BLOCK - 1)
    kc, vcT = _compress(chunk(kc_tok), chunk(vc_tok), pe, kw1, kw2, vw1, vw2,
                        _rope_tables(cmp_end, ROPE_DIM, dh, ROPE_THETA))
    c2sT, et = _nsa_constants(S)
    a_out = _nsa(qk_n, gates, kc, vcT, vsT, vwT, c2sT, et, B, S)
    r_out = _retention(qk_r, krT, v_ret, p["ret_gn_gain"], B, S).reshape(T, H * dv)

    x2 = _merge(x1, a_out, r_out, gates, bf(p["w_branch_nsa"]), bf(p["w_branch_ret"]), bf(p["w_out"]))
    x3 = _ffn(x2, p["ffn2_norm"], bf(p["ffn2_w_gate"]), bf(p["ffn2_w_up"]), bf(p["ffn2_w_down"]),
              final_norm, final=True)
    return x3.reshape(B, S, D)


def kernel(x, ffn1_norm, ffn1_w_gate, ffn1_w_up, ffn1_w_down, mix_norm, w_in, cmp_pos_emb, cmp_k_w1,
           cmp_k_w2, cmp_v_w1, cmp_v_w2, ret_gn_gain, w_branch_nsa, w_branch_ret, w_out, ffn2_norm,
           ffn2_w_gate, ffn2_w_up, ffn2_w_down, final_norm):
    assert ffn1_norm.shape[0] == 1, "single-layer stack"
    names = ("ffn1_norm", "ffn1_w_gate", "ffn1_w_up", "ffn1_w_down", "mix_norm", "w_in", "cmp_pos_emb",
             "cmp_k_w1", "cmp_k_w2", "cmp_v_w1", "cmp_v_w2", "ret_gn_gain", "w_branch_nsa",
             "w_branch_ret", "w_out", "ffn2_norm", "ffn2_w_gate", "ffn2_w_up", "ffn2_w_down")
    vals = (ffn1_norm, ffn1_w_gate, ffn1_w_up, ffn1_w_down, mix_norm, w_in, cmp_pos_emb, cmp_k_w1,
            cmp_k_w2, cmp_v_w1, cmp_v_w2, ret_gn_gain, w_branch_nsa, w_branch_ret, w_out, ffn2_norm,
            ffn2_w_gate, ffn2_w_up, ffn2_w_down)
    p = {n: v[0] for n, v in zip(names, vals)}
    return _layer(x, p, final_norm)
```
